```python
import math
import jax, jax.numpy as jnp
from jax import lax
import numpy as np

D_MODEL = 1024
BATCH = 8
SEQ = 8192
DEPTH = 2
DEC_BATCH = 32
DEC_SEQ = 32
PAST_LEN = 4096

CHUNK = 64
Q_BLOCK = 128
N_EVEN = (DEPTH + 1) // 2
N_ODD = DEPTH // 2
H_DIFF = 4
DH_QK = 64
DV_DIFF = 2 * DH_QK
D_DIFF = H_DIFF * DV_DIFF
Q_DIFF = 2 * H_DIFF * DH_QK
D_CONV = D_MODEL - D_DIFF
CONV_W = 3
SPLIT_EVEN = [Q_DIFF, 2 * Q_DIFF, 2 * Q_DIFF + D_DIFF, 2 * Q_DIFF + D_DIFF + D_CONV, 2 * Q_DIFF + D_DIFF + 2 * D_CONV]
EVEN_IN = 2 * Q_DIFF + D_DIFF + 3 * D_CONV
H_SB = 16
DH_SB = 64
D_SB = H_SB * DH_SB
N_MEM = 256
H_MEM = 4
DH_MEM = D_MODEL // H_MEM
D_FF = 4 * D_MODEL
ROPE_THETA = 10000.0
EPS = 1e-6
SUBLN_EPS = 1e-5
NEG_INF = -1e30

kernel_name = 'hybrid_stream_diffconv_stickbreak_step'


def rmsnorm(x, g, eps=EPS):
    xf = x.astype(jnp.float32)
    y = xf * lax.rsqrt(jnp.mean(xf * xf, axis=-1, keepdims=True) + eps)
    return (y * g.astype(jnp.float32)).astype(x.dtype)


def rope(x, pos):
    d = x.shape[-1]
    half = d // 2
    inv = jnp.power(ROPE_THETA, -jnp.arange(half, dtype=jnp.float32) * (2.0 / d))
    ang = pos.astype(jnp.float32)[:, None] * inv[None, :]
    cos = jnp.cos(ang)[None, :, None, :]
    sin = jnp.sin(ang)[None, :, None, :]
    xf = x.astype(jnp.float32)
    x1, x2 = xf[..., :half], xf[..., half:]
    return jnp.concatenate([x1 * cos - x2 * sin, x2 * cos + x1 * sin], axis=-1).astype(x.dtype)


def sweep_queries(fn, q, q_pos):
    b, t = q.shape[0], q.shape[1]
    if t <= Q_BLOCK:
        return fn(q, q_pos)
    nb = t // Q_BLOCK
    qb = jnp.swapaxes(q.reshape((b, nb, Q_BLOCK) + q.shape[2:]), 0, 1)
    pb = q_pos.reshape(nb, Q_BLOCK)
    out = lax.map(lambda args: fn(args[0], args[1]), (qb, pb))
    return jnp.swapaxes(out, 0, 1).reshape((b, t) + out.shape[3:])


def diff_attn_block(q, k, v, q_pos, k_pos, lam):
    s = jnp.einsum('bqhd,bkhd->bhqk', q, k).astype(jnp.float32) * (DH_QK ** -0.5)
    visible = (k_pos[None, :] // CHUNK) <= (q_pos[:, None] // CHUNK)
    p = jax.nn.softmax(jnp.where(visible, s, NEG_INF), axis=-1)
    b, _, tq, tk = p.shape
    p = p.reshape(b, H_DIFF, 2, tq, tk)
    a = p[:, :, 0] - lam * p[:, :, 1]
    return jnp.einsum('bhqk,bkhd->bqhd', a.astype(v.dtype), v)


def stick_breaking_block(q, k, v, q_pos, k_pos):
    z = jnp.einsum('bqhd,bkhd->bhqk', q, k).astype(jnp.float32) * (DH_SB ** -0.5)
    visible = k_pos[None, :] < q_pos[:, None]
    log_stay = jnp.where(visible, jax.nn.log_sigmoid(-z), 0.0)
    after = lax.cumsum(log_stay, axis=3, reverse=True) - log_stay
    w = jnp.where(visible, jnp.exp(jax.nn.log_sigmoid(z) + after), 0.0)
    return jnp.einsum('bhqk,bkhd->bqhd', w.astype(v.dtype), v)


def causal_short_conv(u, prev, w):
    t = u.shape[1]
    ext = jnp.concatenate([prev, u], axis=1)
    y = ext[:, 0:t] * w[0]
    for j in range(1, CONV_W):
        y = y + ext[:, j:j + t] * w[j]
    return y, ext[:, ext.shape[1] - (CONV_W - 1):]


def even_mixer(h, pos, past_k, past_v, prev_conv, w_in, w_out, lq1, lk1, lq2, lk2, g_sub, conv_w, lam_init):
    b, t, _ = h.shape
    q, k, v, gate_b, gate_c, u = jnp.split(h @ w_in, SPLIT_EVEN, axis=-1)
    q = rope(q.reshape(b, t, 2 * H_DIFF, DH_QK), pos)
    k = rope(k.reshape(b, t, 2 * H_DIFF, DH_QK), pos)
    v = v.reshape(b, t, H_DIFF, DV_DIFF)
    k_all = k if past_k is None else jnp.concatenate([past_k, k], axis=1)
    v_all = v if past_v is None else jnp.concatenate([past_v, v], axis=1)
    k_pos = jnp.arange(k_all.shape[1])
    f32 = jnp.float32
    lam = (jnp.exp(jnp.sum(lq1.astype(f32) * lk1.astype(f32)))
           - jnp.exp(jnp.sum(lq2.astype(f32) * lk2.astype(f32))) + lam_init)
    o = sweep_queries(lambda qb, pb: diff_attn_block(qb, k_all, v_all, pb, k_pos, lam), q, pos)
    o = (rmsnorm(o, g_sub, SUBLN_EPS) * (1.0 - lam_init)).reshape(b, t, D_DIFF)
    conv_out, new_conv = causal_short_conv(gate_c * u, prev_conv, conv_w)
    y = jnp.concatenate([o, gate_b * conv_out], axis=-1) @ w_out
    return y, k, v, new_conv


def odd_mixer(h, pos, past_k, past_v, w_in, w_out):
    b, t, _ = h.shape
    q, k, v = jnp.split(h @ w_in, 3, axis=-1)
    q = q.reshape(b, t, H_SB, DH_SB)
    k = k.reshape(b, t, H_SB, DH_SB)
    v = v.reshape(b, t, H_SB, DH_SB)
    k_all = k if past_k is None else jnp.concatenate([past_k, k], axis=1)
    v_all = v if past_v is None else jnp.concatenate([past_v, v], axis=1)
    k_pos = jnp.arange(k_all.shape[1])
    o = sweep_queries(lambda qb, pb: stick_breaking_block(qb, k_all, v_all, pb, k_pos), q, pos)
    return o.reshape(b, t, D_SB) @ w_out, k, v


def mem_project(mem, g, wk, wv):
    b = mem.shape[0]
    m = rmsnorm(mem, g)
    return (m @ wk).reshape(b, N_MEM, H_MEM, DH_MEM), (m @ wv).reshape(b, N_MEM, H_MEM, DH_MEM)


def mem_attend(h, mk, mv, wq, wo):
    b, t, _ = h.shape
    q = (h @ wq).reshape(b, t, H_MEM, DH_MEM)
    s = jnp.einsum('bqhd,bkhd->bhqk', q, mk).astype(jnp.float32) * (DH_MEM ** -0.5)
    p = jax.nn.softmax(s, axis=-1)
    o = jnp.einsum('bhqk,bkhd->bqhd', p.astype(mv.dtype), mv).reshape(b, t, D_MODEL)
    return o @ wo


def setup_inputs(seed: int = 0) -> dict:
    key = jax.random.key(seed)
    ks = list(jax.random.split(key, 40))

    def nrm(shape, scale=1.0):
        return jax.random.normal(ks.pop(), shape, jnp.float32) * scale

    def gain(shape):
        return 1.0 + 0.02 * nrm(shape)

    return {
        'x_prompt': nrm((BATCH, SEQ, D_MODEL)),
        'x_sample': nrm((DEC_BATCH, DEC_SEQ, D_MODEL)),
        'cache_diff_k': nrm((N_EVEN, DEC_BATCH, PAST_LEN, 2 * H_DIFF, DH_QK)),
        'cache_diff_v': nrm((N_EVEN, DEC_BATCH, PAST_LEN, H_DIFF, DV_DIFF)),
        'state_conv': nrm((N_EVEN, DEC_BATCH, CONV_W - 1, D_CONV)),
        'cache_sb_k': nrm((N_ODD, DEC_BATCH, PAST_LEN, H_SB, DH_SB)),
        'cache_sb_v': nrm((N_ODD, DEC_BATCH, PAST_LEN, H_SB, DH_SB)),
        'cache_mem_k': nrm((DEPTH, DEC_BATCH, N_MEM, H_MEM, DH_MEM)),
        'cache_mem_v': nrm((DEPTH, DEC_BATCH, N_MEM, H_MEM, DH_MEM)),
        'mem_prompt': nrm((BATCH, N_MEM, D_MODEL)),
        'w_in_even': nrm((N_EVEN, D_MODEL, EVEN_IN), D_MODEL ** -0.5),
        'w_out_even': nrm((N_EVEN, D_DIFF + D_CONV, D_MODEL), (D_DIFF + D_CONV) ** -0.5),
        'lambda_q1': nrm((N_EVEN, DH_QK), 0.1),
        'lambda_k1': nrm((N_EVEN, DH_QK), 0.1),
        'lambda_q2': nrm((N_EVEN, DH_QK), 0.1),
        'lambda_k2': nrm((N_EVEN, DH_QK), 0.1),
        'subln_gain': gain((N_EVEN, DV_DIFF)),
        'conv_w': nrm((N_EVEN, CONV_W, D_CONV), CONV_W ** -0.5),
        'w_in_odd': nrm((N_ODD, D_MODEL, 3 * D_SB), D_MODEL ** -0.5),
        'w_out_odd': nrm((N_ODD, D_SB, D_MODEL), D_SB ** -0.5),
        'norm_mix': gain((DEPTH, D_MODEL)),
        'norm_mem': gain((DEPTH, D_MODEL)),
        'norm_cross': gain((DEPTH, D_MODEL)),
        'w_q_mem': nrm((DEPTH, D_MODEL, D_MODEL), D_MODEL ** -0.5),
        'w_k_mem': nrm((DEPTH, D_MODEL, D_MODEL), D_MODEL ** -0.5),
        'w_v_mem': nrm((DEPTH, D_MODEL, D_MODEL), D_MODEL ** -0.5),
        'w_o_mem': nrm((DEPTH, D_MODEL, D_MODEL), D_MODEL ** -0.5),
        'norm_ffn': gain((DEPTH, D_MODEL)),
        'w_ffn_up': nrm((DEPTH, D_MODEL, D_FF), D_MODEL ** -0.5),
        'w_ffn_down': nrm((DEPTH, D_FF, D_MODEL), D_FF ** -0.5),
        'norm_final': gain((D_MODEL,)),
    }


def reference(x_prompt, x_sample, cache_diff_k, cache_diff_v, state_conv, cache_sb_k, cache_sb_v,
              cache_mem_k, cache_mem_v, mem_prompt, w_in_even, w_out_even, lambda_q1, lambda_k1,
              lambda_q2, lambda_k2, subln_gain, conv_w, w_in_odd, w_out_odd, norm_mix, norm_mem,
              norm_cross, w_q_mem, w_k_mem, w_v_mem, w_o_mem, norm_ffn, w_ffn_up, w_ffn_down, norm_final):

    def run(x, mem_k, mem_v, past):
        b, t, _ = x.shape
        past_len = 0 if past is None else past[0].shape[2]
        pos = jnp.arange(past_len, past_len + t)
        dks, dvs, convs, sks, svs = [], [], [], [], []
        for i in range(DEPTH):
            h = rmsnorm(x, norm_mix[i])
            if i % 2 == 0:
                e = i // 2
                if past is None:
                    pk, pv, pc = None, None, jnp.zeros((b, CONV_W - 1, D_CONV), x.dtype)
                else:
                    pk, pv, pc = past[0][e], past[1][e], past[2][e]
                lam_init = 0.8 - 0.6 * math.exp(-0.3 * i)
                mix, nk, nv, nc = even_mixer(h, pos, pk, pv, pc, w_in_even[e], w_out_even[e],
                                             lambda_q1[e], lambda_k1[e], lambda_q2[e], lambda_k2[e],
                                             subln_gain[e], conv_w[e], lam_init)
                dks.append(nk)
                dvs.append(nv)
                convs.append(nc)
            else:
                o = i // 2
                pk = None if past is None else past[3][o]
                pv = None if past is None else past[4][o]
                mix, nk, nv = odd_mixer(h, pos, pk, pv, w_in_odd[o], w_out_odd[o])
                sks.append(nk)
                svs.append(nv)
            x = x + mix
            x = x + mem_attend(rmsnorm(x, norm_cross[i]), mem_k[i], mem_v[i], w_q_mem[i], w_o_mem[i])
            hf = rmsnorm(x, norm_ffn[i])
            x = x + jnp.square(jax.nn.relu(hf @ w_ffn_up[i])) @ w_ffn_down[i]
        return (rmsnorm(x, norm_final), jnp.stack(dks), jnp.stack(dvs), jnp.stack(convs),
                jnp.stack(sks), jnp.stack(svs))

    mkv = [mem_project(mem_prompt, norm_mem[i], w_k_mem[i], w_v_mem[i]) for i in range(DEPTH)]
    p_mem_k = jnp.stack([m[0] for m in mkv])
    p_mem_v = jnp.stack([m[1] for m in mkv])

    y_prompt, p_diff_k, p_diff_v, p_conv, p_sb_k, p_sb_v = run(x_prompt, p_mem_k, p_mem_v, None)
    y_sample, s_diff_k, s_diff_v, s_conv, s_sb_k, s_sb_v = run(
        x_sample, cache_mem_k, cache_mem_v,
        (cache_diff_k, cache_diff_v, state_conv, cache_sb_k, cache_sb_v))

    return (y_prompt, y_sample, p_diff_k, p_diff_v, p_conv, p_sb_k, p_sb_v, p_mem_k, p_mem_v,
            s_diff_k, s_diff_v, s_conv, s_sb_k, s_sb_v)
```

```python
import functools
import math

import jax
import jax.numpy as jnp
from jax import lax
from jax.experimental import pallas as pl
from jax.experimental.pallas import tpu as pltpu

F32 = jnp.float32
BF16 = jnp.bfloat16

D_MODEL = 1024
DEPTH = 2
CHUNK = 64
H_DIFF = 4
DH_QK = 64
DV_DIFF = 2 * DH_QK
D_DIFF = H_DIFF * DV_DIFF
Q_DIFF = 2 * H_DIFF * DH_QK
D_CONV = D_MODEL - D_DIFF
CONV_W = 3
EVEN_IN = 2 * Q_DIFF + D_DIFF + 3 * D_CONV
H_SB = 16
DH_SB = 64
D_SB = H_SB * DH_SB
N_MEM = 256
H_MEM = 4
DH_MEM = D_MODEL // H_MEM
D_FF = 4 * D_MODEL
ROPE_THETA = 10000.0
EPS = 1e-6
SUBLN_EPS = 1e-5
NEG_INF = -1e30

LANE = 128
SUBLANE = 8
VMEM_LIMIT = 52 * 1024 * 1024


def _params(*sem):
    return pltpu.CompilerParams(dimension_semantics=sem, vmem_limit_bytes=VMEM_LIMIT)


def _rms(x, g, eps):
    ms = jnp.mean(x * x, axis=-1, keepdims=True)
    return x * lax.rsqrt(ms + eps) * g


def _dot(a, b):
    return jnp.dot(a, b, preferred_element_type=F32)


def _dot_nt(a, b):
    return lax.dot_general(a, b, (((1,), (1,)), ((), ())), preferred_element_type=F32)


def _full(shape):
    n = len(shape)
    return pl.BlockSpec(shape, lambda *_: (0,) * n)


def _chunk_of(pos):
    shift = CHUNK.bit_length() - 1
    assert 1 << shift == CHUNK
    return lax.shift_right_logical(pos, shift)


def _row_query_index(groups, t, width):
    r = lax.broadcasted_iota(jnp.int32, (t, width), 0)
    return jnp.concatenate([r] * groups, axis=0)


def _lane_band(lane, s, width):
    return (lane >= s * width) & (lane < (s + 1) * width)


def _even_in_kernel(x_ref, g_ref, w_ref, cos_ref, sin_ref, cw_ref, st_ref,
                    q_ref, k_ref, kb_ref, v_ref, vb_ref, gc_ref, nc_ref, ext_ref, *, tm, nt):
    t = pl.program_id(1)
    h = _rms(x_ref[0], g_ref[...], EPS).astype(BF16)

    def proj(lo, width):
        return _dot(h, w_ref[:, lo:lo + width])

    cos = cos_ref[...]
    sin = sin_ref[...]
    lane = lax.broadcasted_iota(jnp.int32, (tm, LANE), 1)
    first_half = (lane & (DH_QK // 2)) == 0

    def rope_group(yj):
        sw = jnp.where(first_half, pltpu.roll(yj, LANE - DH_QK // 2, 1), pltpu.roll(yj, DH_QK // 2, 1))
        return yj * cos + sw * sin

    yq = proj(0, Q_DIFF)
    yk = proj(Q_DIFF, Q_DIFF)
    for j in range(Q_DIFF // LANE):
        sl = slice(j * LANE, (j + 1) * LANE)
        q_ref[0, :, sl] = (rope_group(yq[:, sl]) * (DH_QK ** -0.5)).astype(BF16)
        kj = rope_group(yk[:, sl])
        k_ref[0, :, sl] = kj
        kb_ref[0, :, sl] = kj.astype(BF16)
    yv = proj(2 * Q_DIFF, D_DIFF)
    v_ref[0] = yv
    vb_ref[0] = yv.astype(BF16)

    base = 2 * Q_DIFF + D_DIFF
    gate_b = proj(base, D_CONV)
    cu = proj(base + D_CONV, D_CONV) * proj(base + 2 * D_CONV, D_CONV)

    @pl.when(t == 0)
    def _():
        ext_ref[SUBLANE - 2:SUBLANE, :] = st_ref[0]

    ext_ref[SUBLANE:SUBLANE + tm, :] = cu
    cw = cw_ref[...]
    conv = (ext_ref[SUBLANE - 2:SUBLANE - 2 + tm, :] * cw[0:1, :]
            + ext_ref[SUBLANE - 1:SUBLANE - 1 + tm, :] * cw[1:2, :]
            + cu * cw[2:3, :])
    gc_ref[0] = (gate_b * conv).astype(BF16)
    ext_ref[0:SUBLANE, :] = ext_ref[tm:tm + SUBLANE, :]

    @pl.when(t == nt - 1)
    def _():
        nc_ref[0] = ext_ref[tm + SUBLANE - 2:tm + SUBLANE, :]


def _even_in(x, g, wb, cos, sin, cw, state, tm):
    b, t, _ = x.shape
    nt = t // tm
    tok = lambda w: pl.BlockSpec((1, tm, w), lambda i, j: (i, j, 0))
    outs = [
        jax.ShapeDtypeStruct((b, t, Q_DIFF), BF16),
        jax.ShapeDtypeStruct((b, t, Q_DIFF), F32),
        jax.ShapeDtypeStruct((b, t, Q_DIFF), BF16),
        jax.ShapeDtypeStruct((b, t, D_DIFF), F32),
        jax.ShapeDtypeStruct((b, t, D_DIFF), BF16),
        jax.ShapeDtypeStruct((b, t, D_CONV), BF16),
        jax.ShapeDtypeStruct((b, CONV_W - 1, D_CONV), F32),
    ]
    return pl.pallas_call(
        functools.partial(_even_in_kernel, tm=tm, nt=nt),
        grid=(b, nt),
        in_specs=[
            tok(D_MODEL),
            _full((1, D_MODEL)),
            _full((D_MODEL, EVEN_IN)),
            pl.BlockSpec((tm, LANE), lambda i, j: (j, 0)),
            pl.BlockSpec((tm, LANE), lambda i, j: (j, 0)),
            _full((CONV_W, D_CONV)),
            pl.BlockSpec((1, CONV_W - 1, D_CONV), lambda i, j: (i, 0, 0)),
        ],
        out_specs=[tok(Q_DIFF), tok(Q_DIFF), tok(Q_DIFF), tok(D_DIFF), tok(D_DIFF), tok(D_CONV),
                   pl.BlockSpec((1, CONV_W - 1, D_CONV), lambda i, j: (i, 0, 0))],
        out_shape=outs,
        scratch_shapes=[pltpu.VMEM((tm + 2 * SUBLANE, D_CONV), F32)],
        compiler_params=_params("parallel", "arbitrary"),
        name="even_in",
    )(x, g, wb, cos, sin, cw, state)


def _norm_proj_kernel(x_ref, g_ref, w_ref, *out_refs, segs):
    h = _rms(x_ref[0], g_ref[0], EPS).astype(BF16)
    i = 0
    for lo, width, want_f32, want_bf16, scale in segs:
        y = _dot(h, w_ref[0, :, lo:lo + width])
        if scale != 1.0:
            y = y * scale
        if want_f32:
            out_refs[i][0] = y
            i += 1
        if want_bf16:
            out_refs[i][0] = y.astype(BF16)
            i += 1


def _norm_proj(x, g, wb, segs, tm):
    b, t, d = x.shape
    n = wb.shape[-1]
    ng = wb.shape[0]
    nt = t // tm
    outs, specs = [], []
    for lo, width, want_f32, want_bf16, _ in segs:
        for want, dt in ((want_f32, F32), (want_bf16, BF16)):
            if want:
                outs.append(jax.ShapeDtypeStruct((ng * b, t, width), dt))
                specs.append(pl.BlockSpec((1, tm, width), lambda i, j: (i, j, 0)))
    return pl.pallas_call(
        functools.partial(_norm_proj_kernel, segs=segs),
        grid=(ng * b, nt),
        in_specs=[
            pl.BlockSpec((1, tm, d), lambda i, j: (i % b, j, 0)),
            pl.BlockSpec((1, 1, d), lambda i, j: (i // b, 0, 0)),
            pl.BlockSpec((1, d, n), lambda i, j: (i // b, 0, 0)),
        ],
        out_specs=specs,
        out_shape=outs,
        compiler_params=_params("parallel", "parallel"),
        name="norm_proj",
    )(x, g, wb)


def _lambda_full(lq1_ref, lk1_ref, lq2_ref, lk2_ref, lam_init):
    s1 = jnp.sum(lq1_ref[...] * lk1_ref[...], axis=1, keepdims=True)
    s2 = jnp.sum(lq2_ref[...] * lk2_ref[...], axis=1, keepdims=True)
    return jnp.exp(s1) - jnp.exp(s2) + lam_init


def _subln(o, gsub_ref, lam_init):
    return _rms(o, gsub_ref[...], SUBLN_EPS) * (1.0 - lam_init)


def _diff_prompt_kernel(q_ref, k_ref, v_ref, lq1_ref, lk1_ref, lq2_ref, lk2_ref, gsub_ref,
                        o_ref, *, tq, lam_init):
    qi = pl.program_id(2)
    q = q_ref[0]
    lane = lax.broadcasted_iota(jnp.int32, (tq, LANE), 1)
    zero = jnp.zeros_like(q)
    qm = (jnp.where(lane < DH_QK, q, zero), jnp.where(lane >= DH_QK, q, zero))

    def step(j, carry, masked):
        start = pl.multiple_of(j * tq, tq)
        kb = k_ref[0, pl.ds(start, tq), :]
        vb = v_ref[0, pl.ds(start, tq), :]
        if masked:
            r = _chunk_of(lax.broadcasted_iota(jnp.int32, (tq, tq), 0))
            c = _chunk_of(lax.broadcasted_iota(jnp.int32, (tq, tq), 1))
            vis = c <= r
        new = []
        for mi in range(2):
            m, l, acc = carry[mi]
            s = _dot_nt(qm[mi], kb)
            if masked:
                s = jnp.where(vis, s, NEG_INF)
            m_new = jnp.maximum(m, jnp.max(s, axis=1, keepdims=True))
            alpha = jnp.exp(m - m_new)
            p = jnp.exp(s - m_new)
            l = alpha * l + jnp.sum(p, axis=1, keepdims=True)
            acc = alpha * acc + _dot(p.astype(BF16), vb)
            new.append((m_new, l, acc))
        return tuple(new)

    init = tuple((jnp.full((tq, 1), NEG_INF, F32), jnp.zeros((tq, 1), F32),
                  jnp.zeros((tq, LANE), F32)) for _ in range(2))
    carry = lax.fori_loop(0, qi, lambda j, c: step(j, c, False), init)
    carry = step(qi, carry, True)
    lam = _lambda_full(lq1_ref, lk1_ref, lq2_ref, lk2_ref, lam_init)
    (_, l1, a1), (_, l2, a2) = carry
    o = a1 / l1 - lam * (a2 / l2)
    o_ref[0] = _subln(o, gsub_ref, lam_init).astype(BF16)


def _diff_prompt(q, kb, vb, lq1, lk1, lq2, lk2, gsub, lam_init, tq):
    b, t, _ = q.shape
    lam_spec = _full((1, DH_QK))
    return pl.pallas_call(
        functools.partial(_diff_prompt_kernel, tq=tq, lam_init=lam_init),
        grid=(b, H_DIFF, t // tq),
        in_specs=[
            pl.BlockSpec((1, tq, LANE), lambda i, h, j: (i, j, h)),
            pl.BlockSpec((1, t, LANE), lambda i, h, j: (i, 0, h)),
            pl.BlockSpec((1, t, LANE), lambda i, h, j: (i, 0, h)),
            lam_spec, lam_spec, lam_spec, lam_spec,
            _full((1, DV_DIFF)),
        ],
        out_specs=pl.BlockSpec((1, tq, LANE), lambda i, h, j: (i, j, h)),
        out_shape=jax.ShapeDtypeStruct((b, t, D_DIFF), BF16),
        compiler_params=_params("parallel", "parallel", "arbitrary"),
        name="diff_prompt",
    )(q, kb, vb, lq1, lk1, lq2, lk2, gsub)


def _diff_decode_kernel(q_ref, kp_ref, vp_ref, kn_ref, vn_ref, lq1_ref, lk1_ref, lq2_ref, lk2_ref,
                        gsub_ref, o_ref, qs_ref, m_ref, l_ref, acc_ref, *, t, tk, nk, past_len,
                        lam_init):
    j = pl.program_id(1)
    nsub = 2 * H_DIFF
    rows = nsub * t

    @pl.when(j == 0)
    def _():
        q = q_ref[0]
        lane = lax.broadcasted_iota(jnp.int32, (t, Q_DIFF), 1)
        for s in range(nsub):
            qs_ref[s * t:(s + 1) * t, :] = jnp.where(_lane_band(lane, s, DH_QK), q, jnp.zeros_like(q))
        m_ref[...] = jnp.full(m_ref.shape, NEG_INF, F32)
        l_ref[...] = jnp.zeros(l_ref.shape, F32)
        acc_ref[...] = jnp.zeros(acc_ref.shape, F32)

    def update(kb, vb, k_start, width):
        s = _dot_nt(qs_ref[...], kb)
        q_pos = past_len + _row_query_index(nsub, t, width)
        k_pos = k_start + lax.broadcasted_iota(jnp.int32, (rows, width), 1)
        s = jnp.where(_chunk_of(k_pos) <= _chunk_of(q_pos), s, NEG_INF)
        m = m_ref[...]
        m_new = jnp.maximum(m, jnp.max(s, axis=1, keepdims=True))
        alpha = jnp.exp(m - m_new)
        p = jnp.exp(s - m_new)
        l_ref[...] = alpha * l_ref[...] + jnp.sum(p, axis=1, keepdims=True)
        m_ref[...] = m_new
        pb = p.astype(BF16)
        for h in range(H_DIFF):
            rs = slice(2 * h * t, (2 * h + 2) * t)
            acc_ref[rs, :] = alpha[rs] * acc_ref[rs, :] + _dot(pb[rs], vb[:, h * LANE:(h + 1) * LANE])

    update(kp_ref[0].astype(BF16), vp_ref[0].astype(BF16), j * tk, tk)

    @pl.when(j == nk - 1)
    def _():
        update(kn_ref[0], vn_ref[0], past_len, t)
        lam = _lambda_full(lq1_ref, lk1_ref, lq2_ref, lk2_ref, lam_init)
        on = acc_ref[...] / l_ref[...]
        for h in range(H_DIFF):
            o = on[2 * h * t:(2 * h + 1) * t] - lam * on[(2 * h + 1) * t:(2 * h + 2) * t]
            o_ref[0, :, h * LANE:(h + 1) * LANE] = _subln(o, gsub_ref, lam_init).astype(BF16)


def _diff_decode(q, k_past, v_past, kn, vn, lq1, lk1, lq2, lk2, gsub, lam_init, tk):
    b, t, _ = q.shape
    past_len = k_past.shape[1]
    nk = past_len // tk
    rows = 2 * H_DIFF * t
    lam_spec = _full((1, DH_QK))
    tokb = lambda w: pl.BlockSpec((1, t, w), lambda i, j: (i, 0, 0))
    return pl.pallas_call(
        functools.partial(_diff_decode_kernel, t=t, tk=tk, nk=nk, past_len=past_len,
                          lam_init=lam_init),
        grid=(b, nk),
        in_specs=[
            tokb(Q_DIFF),
            pl.BlockSpec((1, tk, Q_DIFF), lambda i, j: (i, j, 0)),
            pl.BlockSpec((1, tk, D_DIFF), lambda i, j: (i, j, 0)),
            tokb(Q_DIFF), tokb(D_DIFF),
            lam_spec, lam_spec, lam_spec, lam_spec,
            _full((1, DV_DIFF)),
        ],
        out_specs=tokb(D_DIFF),
        out_shape=jax.ShapeDtypeStruct((b, t, D_DIFF), BF16),
        scratch_shapes=[pltpu.VMEM((rows, Q_DIFF), BF16), pltpu.VMEM((rows, 1), F32),
                        pltpu.VMEM((rows, 1), F32), pltpu.VMEM((rows, DV_DIFF), F32)],
        compiler_params=_params("parallel", "arbitrary"),
        name="diff_decode",
    )(q, k_past, v_past, kn, vn, lq1, lk1, lq2, lk2, gsub)


def _softplus(z):
    return jnp.maximum(z, 0.0) + jnp.log1p(jnp.exp(-jnp.abs(z)))


def _rev_cumsum(sp, tri):
    hi = sp.astype(BF16)
    lo = (sp - hi.astype(F32)).astype(BF16)
    return _dot(hi, tri) + _dot(lo, tri)


def _tri(n):
    r = lax.broadcasted_iota(jnp.int32, (n, n), 0)
    c = lax.broadcasted_iota(jnp.int32, (n, n), 1)
    return jnp.where(r >= c, 1.0, 0.0).astype(BF16)


def _sb_prompt_kernel(q_ref, k_ref, v_ref, o_ref, *, tq):
    qi = pl.program_id(2)
    q = q_ref[0]
    lane = lax.broadcasted_iota(jnp.int32, (tq, LANE), 1)
    zero = jnp.zeros_like(q)
    qm = (jnp.where(lane < DH_SB, q, zero), jnp.where(lane >= DH_SB, q, zero))
    tri = _tri(tq)

    def step(j, carry, diag):
        start = pl.multiple_of(j * tq, tq)
        kb = k_ref[0, pl.ds(start, tq), :]
        vb = v_ref[0, pl.ds(start, tq), :]
        if diag:
            r = lax.broadcasted_iota(jnp.int32, (tq, tq), 0)
            c = lax.broadcasted_iota(jnp.int32, (tq, tq), 1)
            vis = c < r
        new = []
        for hi in range(2):
            ccar, acc = carry[hi]
            z = _dot_nt(qm[hi], kb)
            sp = _softplus(z)
            if diag:
                sp = jnp.where(vis, sp, 0.0)
            cl = _rev_cumsum(sp, tri)
            w = jnp.exp(z - ccar - cl)
            if diag:
                w = jnp.where(vis, w, 0.0)
            acc = acc + _dot(w.astype(BF16), vb)
            new.append((ccar + cl[:, 0:1], acc))
        return tuple(new)

    init = tuple((jnp.zeros((tq, 1), F32), jnp.zeros((tq, LANE), F32)) for _ in range(2))
    carry = step(qi, init, True)
    carry = lax.fori_loop(0, qi, lambda i, c: step(qi - 1 - i, c, False), carry)
    o_ref[0] = jnp.where(lane < DH_SB, carry[0][1], carry[1][1]).astype(BF16)


def _sb_prompt(q, kb, vb, tq):
    b, t, _ = q.shape
    return pl.pallas_call(
        functools.partial(_sb_prompt_kernel, tq=tq),
        grid=(b, D_SB // LANE, t // tq),
        in_specs=[
            pl.BlockSpec((1, tq, LANE), lambda i, h, j: (i, j, h)),
            pl.BlockSpec((1, t, LANE), lambda i, h, j: (i, 0, h)),
            pl.BlockSpec((1, t, LANE), lambda i, h, j: (i, 0, h)),
        ],
        out_specs=pl.BlockSpec((1, tq, LANE), lambda i, h, j: (i, j, h)),
        out_shape=jax.ShapeDtypeStruct((b, t, D_SB), BF16),
        compiler_params=_params("parallel", "parallel", "arbitrary"),
        name="sb_prompt",
    )(q, kb, vb)


def _sb_decode_kernel(q_ref, kp_ref, vp_ref, kn_ref, vn_ref, o_ref, qs_ref, c_ref, acc_ref,
                      *, t, tk, nk, hg):
    j = pl.program_id(2)
    rows = hg * t
    width_all = hg * DH_SB

    @pl.when(j == 0)
    def _():
        q = q_ref[0]
        lane = lax.broadcasted_iota(jnp.int32, (t, width_all), 1)
        for s in range(hg):
            qs_ref[s * t:(s + 1) * t, :] = jnp.where(_lane_band(lane, s, DH_SB), q, jnp.zeros_like(q))
        kn = kn_ref[0]
        z = _dot_nt(qs_ref[...], kn)
        r = _row_query_index(hg, t, t)
        c = lax.broadcasted_iota(jnp.int32, (rows, t), 1)
        vis = c < r
        sp = jnp.where(vis, _softplus(z), 0.0)
        cl = _rev_cumsum(sp, _tri(t))
        w = jnp.where(vis, jnp.exp(z - cl), 0.0).astype(BF16)
        vn = vn_ref[0]
        for s in range(hg):
            rs = slice(s * t, (s + 1) * t)
            acc_ref[rs, :] = _dot(w[rs], vn[:, s * DH_SB:(s + 1) * DH_SB])
        c_ref[...] = cl[:, 0:1]

    kb = kp_ref[0].astype(BF16)
    vb = vp_ref[0].astype(BF16)
    z = _dot_nt(qs_ref[...], kb)
    cl = _rev_cumsum(_softplus(z), _tri(tk))
    ccar = c_ref[...]
    w = jnp.exp(z - ccar - cl).astype(BF16)
    for s in range(hg):
        rs = slice(s * t, (s + 1) * t)
        acc_ref[rs, :] += _dot(w[rs], vb[:, s * DH_SB:(s + 1) * DH_SB])
    c_ref[...] = ccar + cl[:, 0:1]

    @pl.when(j == nk - 1)
    def _():
        for s in range(hg):
            o_ref[0, :, s * DH_SB:(s + 1) * DH_SB] = acc_ref[s * t:(s + 1) * t, :].astype(BF16)


def _sb_decode(q, k_past, v_past, kn, vn, tk, hg):
    b, t, _ = q.shape
    past_len = k_past.shape[1]
    nk = past_len // tk
    ng = H_SB // hg
    wg = hg * DH_SB
    rows = hg * t
    tokb = lambda: pl.BlockSpec((1, t, wg), lambda i, g, j: (i, 0, g))
    past = lambda: pl.BlockSpec((1, tk, wg), lambda i, g, j: (i, nk - 1 - j, g))
    return pl.pallas_call(
        functools.partial(_sb_decode_kernel, t=t, tk=tk, nk=nk, hg=hg),
        grid=(b, ng, nk),
        in_specs=[tokb(), past(), past(), tokb(), tokb()],
        out_specs=tokb(),
        out_shape=jax.ShapeDtypeStruct((b, t, D_SB), BF16),
        scratch_shapes=[pltpu.VMEM((rows, wg), BF16), pltpu.VMEM((rows, 1), F32),
                        pltpu.VMEM((rows, DH_SB), F32)],
        compiler_params=_params("parallel", "parallel", "arbitrary"),
        name="sb_decode",
    )(q, k_past, v_past, kn, vn)


def _post_mix_kernel(*refs, n_in):
    x_ref = refs[0]
    a_refs = refs[1:1 + n_in]
    w_ref, g_ref, wq_ref, mk_ref, mv_ref, wo_ref, o_ref = refs[1 + n_in:]
    a = a_refs[0][0] if n_in == 1 else jnp.concatenate([r[0] for r in a_refs], axis=1)
    x = x_ref[0] + _dot(a, w_ref[...])
    hq = _rms(x, g_ref[...], EPS).astype(BF16)
    q = (_dot(hq, wq_ref[...]) * (DH_MEM ** -0.5)).astype(BF16)
    y = x
    for h in range(H_MEM):
        hs = slice(h * DH_MEM, (h + 1) * DH_MEM)
        s = _dot_nt(q[:, hs], mk_ref[0, :, hs])
        p = jnp.exp(s - jnp.max(s, axis=1, keepdims=True))
        l = jnp.sum(p, axis=1, keepdims=True)
        oh = _dot(p.astype(BF16), mv_ref[0, :, hs]) / l
        y = y + _dot(oh.astype(BF16), wo_ref[hs, :])
    o_ref[0] = y


def _post_mix(x, acts, w, g, wq, mk, mv, wo, tm):
    b, t, d = x.shape
    n_in = len(acts)
    tok = lambda w: pl.BlockSpec((1, tm, w), lambda i, j: (i, j, 0))
    mem = pl.BlockSpec((1, N_MEM, D_MODEL), lambda i, j: (i, 0, 0))
    return pl.pallas_call(
        functools.partial(_post_mix_kernel, n_in=n_in),
        grid=(b, t // tm),
        in_specs=([tok(d)] + [tok(a.shape[-1]) for a in acts]
                  + [_full(w.shape), _full((1, d)), _full((d, d)), mem, mem, _full((d, d))]),
        out_specs=tok(d),
        out_shape=jax.ShapeDtypeStruct((b, t, d), F32),
        compiler_params=_params("parallel", "parallel"),
        name="post_mix",
    )(x, *acts, w, g, wq, mk, mv, wo)


def _ffn_kernel(x_ref, g_ref, wu_ref, wd_ref, gf_ref, o_ref, *, fc, final):
    x = x_ref[...]
    h = _rms(x, g_ref[...], EPS).astype(BF16)
    y = x
    for c in range(D_FF // fc):
        u = jnp.maximum(_dot(h, wu_ref[:, c * fc:(c + 1) * fc]), 0.0)
        y = y + _dot((u * u).astype(BF16), wd_ref[c * fc:(c + 1) * fc, :])
    if final:
        y = _rms(y, gf_ref[...], EPS)
    o_ref[...] = y


def _ffn(x2d, g, wu, wd, gf, final, tm, fc):
    m, d = x2d.shape
    return pl.pallas_call(
        functools.partial(_ffn_kernel, fc=fc, final=final),
        grid=(m // tm,),
        in_specs=[pl.BlockSpec((tm, d), lambda i: (i, 0)), _full((1, d)), _full((d, D_FF)),
                  _full((D_FF, d)), _full((1, d))],
        out_specs=pl.BlockSpec((tm, d), lambda i: (i, 0)),
        out_shape=jax.ShapeDtypeStruct((m, d), F32),
        compiler_params=_params("parallel"),
        name="ffn",
    )(x2d, g, wu, wd, gf)


def _rope_tables(pos):
    half = DH_QK // 2
    inv = jnp.power(ROPE_THETA, -jnp.arange(half, dtype=F32) * (2.0 / DH_QK))
    ang = pos.astype(F32)[:, None] * inv[None, :]
    cos, sin = jnp.cos(ang), jnp.sin(ang)
    reps = LANE // DH_QK
    return (jnp.tile(jnp.concatenate([cos, cos], axis=1), (1, reps)),
            jnp.tile(jnp.concatenate([-sin, sin], axis=1), (1, reps)))


def _row(v):
    return v.reshape(1, -1)


def _run(x, mem_kb, mem_vb, past, p, tm, tq):
    b, t, d = x.shape
    past_len = 0 if past is None else past[0].shape[2]
    pos = jnp.arange(past_len, past_len + t)
    cos, sin = _rope_tables(pos)
    outs = {}
    for i in range(DEPTH):
        if i % 2 == 0:
            e = i // 2
            lam_init = 0.8 - 0.6 * math.exp(-0.3 * i)
            state = jnp.zeros((b, CONV_W - 1, D_CONV), F32) if past is None else past[2][e]
            q, k, kb, v, vb, gc, nc = _even_in(x, _row(p['norm_mix'][i]), p['w_in_even_b'][e], cos, sin,
                                               p['conv_w'][e], state, tm)
            lam_args = (_row(p['lambda_q1'][e]), _row(p['lambda_k1'][e]), _row(p['lambda_q2'][e]),
                        _row(p['lambda_k2'][e]), _row(p['subln_gain'][e]))
            if past is None:
                o = _diff_prompt(q, kb, vb, *lam_args, lam_init, tq)
            else:
                o = _diff_decode(q, past[0][e].reshape(b, past_len, Q_DIFF),
                                 past[1][e].reshape(b, past_len, D_DIFF), kb, vb, *lam_args, lam_init,
                                 min(512, past_len))
            outs['dk'] = k.reshape(1, b, t, 2 * H_DIFF, DH_QK)
            outs['dv'] = v.reshape(1, b, t, H_DIFF, DV_DIFF)
            outs['conv'] = nc[None]
            acts = [o, gc]
            w_out = p['w_out_even_b'][e]
        else:
            o_idx = i // 2
            segs = ((0, D_SB, False, True, DH_SB ** -0.5), (D_SB, D_SB, True, True, 1.0),
                    (2 * D_SB, D_SB, True, True, 1.0))
            q, k, kb, v, vb = _norm_proj(x, p['norm_mix'][i].reshape(1, 1, d),
                                         p['w_in_odd_b'][o_idx][None], segs, tm)
            if past is None:
                o = _sb_prompt(q, kb, vb, tq)
            else:
                o = _sb_decode(q, past[3][o_idx].reshape(b, past_len, D_SB),
                               past[4][o_idx].reshape(b, past_len, D_SB), kb, vb,
                               min(512, past_len), 4)
            outs['sk'] = k.reshape(1, b, t, H_SB, DH_SB)
            outs['sv'] = v.reshape(1, b, t, H_SB, DH_SB)
            acts = [o]
            w_out = p['w_out_odd_b'][o_idx]
        x = _post_mix(x, acts, w_out, _row(p['norm_cross'][i]), p['w_q_mem_b'][i], mem_kb[i], mem_vb[i],
                      p['w_o_mem_b'][i], tm)
        m = b * t
        x = _ffn(x.reshape(m, d), _row(p['norm_ffn'][i]), p['w_ffn_up_b'][i], p['w_ffn_down_b'][i],
                 _row(p['norm_final']), i == DEPTH - 1, min(512, m), 1024).reshape(b, t, d)
    return x, outs


def kernel(x_prompt, x_sample, cache_diff_k, cache_diff_v, state_conv, cache_sb_k, cache_sb_v, cache_mem_k, cache_mem_v, mem_prompt, w_in_even, w_out_even, lambda_q1, lambda_k1, lambda_q2, lambda_k2, subln_gain, conv_w, w_in_odd, w_out_odd, norm_mix, norm_mem, norm_cross, w_q_mem, w_k_mem, w_v_mem, w_o_mem, norm_ffn, w_ffn_up, w_ffn_down, norm_final):
    p = dict(norm_mix=norm_mix, norm_cross=norm_cross, norm_ffn=norm_ffn, norm_final=norm_final,
             lambda_q1=lambda_q1, lambda_k1=lambda_k1, lambda_q2=lambda_q2, lambda_k2=lambda_k2,
             subln_gain=subln_gain, conv_w=conv_w)
    for name, w in (('w_in_even', w_in_even), ('w_out_even', w_out_even), ('w_in_odd', w_in_odd),
                    ('w_out_odd', w_out_odd), ('w_q_mem', w_q_mem), ('w_o_mem', w_o_mem),
                    ('w_ffn_up', w_ffn_up), ('w_ffn_down', w_ffn_down)):
        p[name + '_b'] = w.astype(BF16)

    bp = mem_prompt.shape[0]
    w_kv = jnp.concatenate([w_k_mem, w_v_mem], axis=-1).astype(BF16)
    segs = ((0, D_MODEL, True, True, 1.0), (D_MODEL, D_MODEL, True, True, 1.0))
    mk, mkb, mv, mvb = _norm_proj(mem_prompt, norm_mem.reshape(DEPTH, 1, D_MODEL), w_kv, segs, N_MEM)
    p_mem_k = mk.reshape(DEPTH, bp, N_MEM, H_MEM, DH_MEM)
    p_mem_v = mv.reshape(DEPTH, bp, N_MEM, H_MEM, DH_MEM)
    mkb = mkb.reshape(DEPTH, bp, N_MEM, D_MODEL)
    mvb = mvb.reshape(DEPTH, bp, N_MEM, D_MODEL)

    t_p = x_prompt.shape[1]
    y_prompt, po = _run(x_prompt, mkb, mvb, None, p, min(512, t_p), min(256, t_p))

    bs = x_sample.shape[0]
    t_s = x_sample.shape[1]
    cmk = cache_mem_k.reshape(DEPTH, bs, N_MEM, D_MODEL).astype(BF16)
    cmv = cache_mem_v.reshape(DEPTH, bs, N_MEM, D_MODEL).astype(BF16)
    y_sample, so = _run(x_sample, cmk, cmv,
                        (cache_diff_k, cache_diff_v, state_conv, cache_sb_k, cache_sb_v), p, t_s, t_s)

    return (y_prompt, y_sample, po['dk'], po['dv'], po['conv'], po['sk'], po['sv'], p_mem_k, p_mem_v,
            so['dk'], so['dv'], so['conv'], so['sk'], so['sv'])
```

```python
import functools
import math

import jax
import jax.numpy as jnp
from jax import lax
from jax.experimental import pallas as pl
from jax.experimental.pallas import tpu as pltpu

F32 = jnp.float32
BF16 = jnp.bfloat16

D_MODEL = 1024
DEPTH = 2
CHUNK = 64
H_DIFF = 4
DH_QK = 64
DV_DIFF = 2 * DH_QK
D_DIFF = H_DIFF * DV_DIFF
Q_DIFF = 2 * H_DIFF * DH_QK
D_CONV = D_MODEL - D_DIFF
CONV_W = 3
EVEN_IN = 2 * Q_DIFF + D_DIFF + 3 * D_CONV
H_SB = 16
DH_SB = 64
D_SB = H_SB * DH_SB
N_MEM = 256
H_MEM = 4
DH_MEM = D_MODEL // H_MEM
D_FF = 4 * D_MODEL
ROPE_THETA = 10000.0
EPS = 1e-6
SUBLN_EPS = 1e-5
NEG_INF = -1e30
SB_DEAD = 105.0
SB_BOUND_SLACK = 1.01

LANE = 128
SUBLANE = 8
VMEM_LIMIT = 52 * 1024 * 1024


def _params(*sem):
    return pltpu.CompilerParams(dimension_semantics=sem, vmem_limit_bytes=VMEM_LIMIT)


def _rms(x, g, eps):
    ms = jnp.mean(x * x, axis=-1, keepdims=True)
    return x * lax.rsqrt(ms + eps) * g


def _dot(a, b):
    return jnp.dot(a, b, preferred_element_type=F32)


def _dot_nt(a, b):
    return lax.dot_general(a, b, (((1,), (1,)), ((), ())), preferred_element_type=F32)


def _full(shape):
    n = len(shape)
    return pl.BlockSpec(shape, lambda *_: (0,) * n)


def _chunk_of(pos):
    shift = CHUNK.bit_length() - 1
    assert 1 << shift == CHUNK
    return lax.shift_right_logical(pos, shift)


def _row_query_index(groups, t, width):
    r = lax.broadcasted_iota(jnp.int32, (t, width), 0)
    return jnp.concatenate([r] * groups, axis=0)


def _lane_band(lane, s, width):
    return (lane >= s * width) & (lane < (s + 1) * width)


def _even_in_kernel(x_ref, g_ref, w_ref, cos_ref, sin_ref, cw_ref, st_ref,
                    q_ref, k_ref, kb_ref, v_ref, vb_ref, gc_ref, nc_ref, ext_ref, *, tm, nt):
    t = pl.program_id(1)
    h = _rms(x_ref[0], g_ref[...], EPS).astype(BF16)

    def proj(lo, width):
        return _dot(h, w_ref[:, lo:lo + width])

    cos = cos_ref[...]
    sin = sin_ref[...]
    lane = lax.broadcasted_iota(jnp.int32, (tm, LANE), 1)
    first_half = (lane & (DH_QK // 2)) == 0

    def rope_group(yj):
        sw = jnp.where(first_half, pltpu.roll(yj, LANE - DH_QK // 2, 1), pltpu.roll(yj, DH_QK // 2, 1))
        return yj * cos + sw * sin

    yq = proj(0, Q_DIFF)
    yk = proj(Q_DIFF, Q_DIFF)
    for j in range(Q_DIFF // LANE):
        sl = slice(j * LANE, (j + 1) * LANE)
        q_ref[0, :, sl] = (rope_group(yq[:, sl]) * (DH_QK ** -0.5)).astype(BF16)
        kj = rope_group(yk[:, sl])
        k_ref[0, :, sl] = kj
        kb_ref[0, :, sl] = kj.astype(BF16)
    yv = proj(2 * Q_DIFF, D_DIFF)
    v_ref[0] = yv
    vb_ref[0] = yv.astype(BF16)

    base = 2 * Q_DIFF + D_DIFF
    gate_b = proj(base, D_CONV)
    cu = proj(base + D_CONV, D_CONV) * proj(base + 2 * D_CONV, D_CONV)

    @pl.when(t == 0)
    def _():
        ext_ref[SUBLANE - 2:SUBLANE, :] = st_ref[0]

    ext_ref[SUBLANE:SUBLANE + tm, :] = cu
    cw = cw_ref[...]
    conv = (ext_ref[SUBLANE - 2:SUBLANE - 2 + tm, :] * cw[0:1, :]
            + ext_ref[SUBLANE - 1:SUBLANE - 1 + tm, :] * cw[1:2, :]
            + cu * cw[2:3, :])
    gc_ref[0] = (gate_b * conv).astype(BF16)
    ext_ref[0:SUBLANE, :] = ext_ref[tm:tm + SUBLANE, :]

    @pl.when(t == nt - 1)
    def _():
        nc_ref[0] = ext_ref[tm + SUBLANE - 2:tm + SUBLANE, :]


def _even_in(x, g, wb, cos, sin, cw, state, tm):
    b, t, _ = x.shape
    nt = t // tm
    tok = lambda w: pl.BlockSpec((1, tm, w), lambda i, j: (i, j, 0))
    outs = [
        jax.ShapeDtypeStruct((b, t, Q_DIFF), BF16),
        jax.ShapeDtypeStruct((b, t, Q_DIFF), F32),
        jax.ShapeDtypeStruct((b, t, Q_DIFF), BF16),
        jax.ShapeDtypeStruct((b, t, D_DIFF), F32),
        jax.ShapeDtypeStruct((b, t, D_DIFF), BF16),
        jax.ShapeDtypeStruct((b, t, D_CONV), BF16),
        jax.ShapeDtypeStruct((b, CONV_W - 1, D_CONV), F32),
    ]
    return pl.pallas_call(
        functools.partial(_even_in_kernel, tm=tm, nt=nt),
        grid=(b, nt),
        in_specs=[
            tok(D_MODEL),
            _full((1, D_MODEL)),
            _full((D_MODEL, EVEN_IN)),
            pl.BlockSpec((tm, LANE), lambda i, j: (j, 0)),
            pl.BlockSpec((tm, LANE), lambda i, j: (j, 0)),
            _full((CONV_W, D_CONV)),
            pl.BlockSpec((1, CONV_W - 1, D_CONV), lambda i, j: (i, 0, 0)),
        ],
        out_specs=[tok(Q_DIFF), tok(Q_DIFF), tok(Q_DIFF), tok(D_DIFF), tok(D_DIFF), tok(D_CONV),
                   pl.BlockSpec((1, CONV_W - 1, D_CONV), lambda i, j: (i, 0, 0))],
        out_shape=outs,
        scratch_shapes=[pltpu.VMEM((tm + 2 * SUBLANE, D_CONV), F32)],
        compiler_params=_params("parallel", "arbitrary"),
        name="even_in",
    )(x, g, wb, cos, sin, cw, state)


def _norm_proj_kernel(x_ref, g_ref, w_ref, *out_refs, segs):
    h = _rms(x_ref[0], g_ref[0], EPS).astype(BF16)
    i = 0
    for lo, width, want_f32, want_bf16, scale in segs:
        y = _dot(h, w_ref[0, :, lo:lo + width])
        if scale != 1.0:
            y = y * scale
        if want_f32:
            out_refs[i][0] = y
            i += 1
        if want_bf16:
            out_refs[i][0] = y.astype(BF16)
            i += 1


def _norm_proj(x, g, wb, segs, tm):
    b, t, d = x.shape
    n = wb.shape[-1]
    ng = wb.shape[0]
    nt = t // tm
    outs, specs = [], []
    for lo, width, want_f32, want_bf16, _ in segs:
        for want, dt in ((want_f32, F32), (want_bf16, BF16)):
            if want:
                outs.append(jax.ShapeDtypeStruct((ng * b, t, width), dt))
                specs.append(pl.BlockSpec((1, tm, width), lambda i, j: (i, j, 0)))
    return pl.pallas_call(
        functools.partial(_norm_proj_kernel, segs=segs),
        grid=(ng * b, nt),
        in_specs=[
            pl.BlockSpec((1, tm, d), lambda i, j: (i % b, j, 0)),
            pl.BlockSpec((1, 1, d), lambda i, j: (i // b, 0, 0)),
            pl.BlockSpec((1, d, n), lambda i, j: (i // b, 0, 0)),
        ],
        out_specs=specs,
        out_shape=outs,
        compiler_params=_params("parallel", "parallel"),
        name="norm_proj",
    )(x, g, wb)


def _lambda_full(lq1_ref, lk1_ref, lq2_ref, lk2_ref, lam_init):
    s1 = jnp.sum(lq1_ref[...] * lk1_ref[...], axis=1, keepdims=True)
    s2 = jnp.sum(lq2_ref[...] * lk2_ref[...], axis=1, keepdims=True)
    return jnp.exp(s1) - jnp.exp(s2) + lam_init


def _subln(o, gsub_ref, lam_init):
    return _rms(o, gsub_ref[...], SUBLN_EPS) * (1.0 - lam_init)


def _diff_prompt_kernel(q_ref, k_ref, v_ref, lq1_ref, lk1_ref, lq2_ref, lk2_ref, gsub_ref,
                        o_ref, *, tq, tk, lam_init):
    qi = pl.program_id(2)
    ratio = tk // tq
    q = q_ref[0]
    lane = lax.broadcasted_iota(jnp.int32, (tq, LANE), 1)
    zero = jnp.zeros_like(q)
    qm = (jnp.where(lane < DH_QK, q, zero), jnp.where(lane >= DH_QK, q, zero))
    ones = jnp.ones((tk, LANE), BF16)

    def step(j, carry, masked):
        start = pl.multiple_of(j * tk, tk)
        kb = k_ref[0, pl.ds(start, tk), :]
        vb = jnp.concatenate([v_ref[0, pl.ds(start, tk), :], ones], axis=1)
        if masked:
            r = _chunk_of(lax.broadcasted_iota(jnp.int32, (tq, tk), 0)) + lax.rem(qi, ratio) * (tq // CHUNK)
            c = _chunk_of(lax.broadcasted_iota(jnp.int32, (tq, tk), 1))
            vis = c <= r
        new = []
        for mi in range(2):
            m, acc = carry[mi]
            s = _dot_nt(qm[mi], kb)
            if masked:
                s = jnp.where(vis, s, NEG_INF)
            m_new = jnp.maximum(m, jnp.max(s, axis=1, keepdims=True))
            alpha = jnp.exp(m - m_new)
            p = jnp.exp(s - m_new)
            acc = alpha * acc + _dot(p.astype(BF16), vb)
            new.append((m_new, acc))
        return tuple(new)

    init = tuple((jnp.full((tq, 1), NEG_INF, F32), jnp.zeros((tq, 2 * LANE), F32)) for _ in range(2))
    n_full = lax.div(qi, ratio)
    carry = lax.fori_loop(0, n_full, lambda j, c: step(j, c, False), init)
    carry = step(n_full, carry, True)
    lam = _lambda_full(lq1_ref, lk1_ref, lq2_ref, lk2_ref, lam_init)
    (_, a1), (_, a2) = carry
    o = a1[:, :LANE] / a1[:, LANE:] - lam * (a2[:, :LANE] / a2[:, LANE:])
    o_ref[0] = _subln(o, gsub_ref, lam_init).astype(BF16)


def _diff_prompt(q, kb, vb, lq1, lk1, lq2, lk2, gsub, lam_init, tq, tk):
    b, t, _ = q.shape
    lam_spec = _full((1, DH_QK))
    return pl.pallas_call(
        functools.partial(_diff_prompt_kernel, tq=tq, tk=tk, lam_init=lam_init),
        grid=(b, H_DIFF, t // tq),
        in_specs=[
            pl.BlockSpec((1, tq, LANE), lambda i, h, j: (i, j, h)),
            pl.BlockSpec((1, t, LANE), lambda i, h, j: (i, 0, h)),
            pl.BlockSpec((1, t, LANE), lambda i, h, j: (i, 0, h)),
            lam_spec, lam_spec, lam_spec, lam_spec,
            _full((1, DV_DIFF)),
        ],
        out_specs=pl.BlockSpec((1, tq, LANE), lambda i, h, j: (i, j, h)),
        out_shape=jax.ShapeDtypeStruct((b, t, D_DIFF), BF16),
        compiler_params=_params("parallel", "parallel", "arbitrary"),
        name="diff_prompt",
    )(q, kb, vb, lq1, lk1, lq2, lk2, gsub)


def _diff_decode_kernel(q_ref, kp_ref, vp_ref, kn_ref, vn_ref, lq1_ref, lk1_ref, lq2_ref, lk2_ref,
                        gsub_ref, o_ref, qs_ref, m_ref, l_ref, acc_ref, *, t, tk, nk, past_len,
                        lam_init):
    j = pl.program_id(1)
    nsub = 2 * H_DIFF
    rows = nsub * t

    @pl.when(j == 0)
    def _():
        q = q_ref[0]
        lane = lax.broadcasted_iota(jnp.int32, (t, Q_DIFF), 1)
        for s in range(nsub):
            qs_ref[s * t:(s + 1) * t, :] = jnp.where(_lane_band(lane, s, DH_QK), q, jnp.zeros_like(q))
        m_ref[...] = jnp.full(m_ref.shape, NEG_INF, F32)
        l_ref[...] = jnp.zeros(l_ref.shape, F32)
        acc_ref[...] = jnp.zeros(acc_ref.shape, F32)

    def update(kb, vb, k_start, width):
        s = _dot_nt(qs_ref[...], kb)
        q_pos = past_len + _row_query_index(nsub, t, width)
        k_pos = k_start + lax.broadcasted_iota(jnp.int32, (rows, width), 1)
        s = jnp.where(_chunk_of(k_pos) <= _chunk_of(q_pos), s, NEG_INF)
        m = m_ref[...]
        m_new = jnp.maximum(m, jnp.max(s, axis=1, keepdims=True))
        alpha = jnp.exp(m - m_new)
        p = jnp.exp(s - m_new)
        l_ref[...] = alpha * l_ref[...] + jnp.sum(p, axis=1, keepdims=True)
        m_ref[...] = m_new
        pb = p.astype(BF16)
        for h in range(H_DIFF):
            rs = slice(2 * h * t, (2 * h + 2) * t)
            acc_ref[rs, :] = alpha[rs] * acc_ref[rs, :] + _dot(pb[rs], vb[:, h * LANE:(h + 1) * LANE])

    update(kp_ref[0].astype(BF16), vp_ref[0].astype(BF16), j * tk, tk)

    @pl.when(j == nk - 1)
    def _():
        update(kn_ref[0], vn_ref[0], past_len, t)
        lam = _lambda_full(lq1_ref, lk1_ref, lq2_ref, lk2_ref, lam_init)
        on = acc_ref[...] / l_ref[...]
        for h in range(H_DIFF):
            o = on[2 * h * t:(2 * h + 1) * t] - lam * on[(2 * h + 1) * t:(2 * h + 2) * t]
            o_ref[0, :, h * LANE:(h + 1) * LANE] = _subln(o, gsub_ref, lam_init).astype(BF16)


def _diff_decode(q, k_past, v_past, kn, vn, lq1, lk1, lq2, lk2, gsub, lam_init, tk):
    b, t, _ = q.shape
    past_len = k_past.shape[1]
    nk = past_len // tk
    rows = 2 * H_DIFF * t
    lam_spec = _full((1, DH_QK))
    tokb = lambda w: pl.BlockSpec((1, t, w), lambda i, j: (i, 0, 0))
    return pl.pallas_call(
        functools.partial(_diff_decode_kernel, t=t, tk=tk, nk=nk, past_len=past_len,
                          lam_init=lam_init),
        grid=(b, nk),
        in_specs=[
            tokb(Q_DIFF),
            pl.BlockSpec((1, tk, Q_DIFF), lambda i, j: (i, j, 0)),
            pl.BlockSpec((1, tk, D_DIFF), lambda i, j: (i, j, 0)),
            tokb(Q_DIFF), tokb(D_DIFF),
            lam_spec, lam_spec, lam_spec, lam_spec,
            _full((1, DV_DIFF)),
        ],
        out_specs=tokb(D_DIFF),
        out_shape=jax.ShapeDtypeStruct((b, t, D_DIFF), BF16),
        scratch_shapes=[pltpu.VMEM((rows, Q_DIFF), BF16), pltpu.VMEM((rows, 1), F32),
                        pltpu.VMEM((rows, 1), F32), pltpu.VMEM((rows, DV_DIFF), F32)],
        compiler_params=_params("parallel", "arbitrary"),
        name="diff_decode",
    )(q, k_past, v_past, kn, vn, lq1, lk1, lq2, lk2, gsub)


def _softplus(z):
    return jnp.maximum(z, 0.0) + jnp.log(1.0 + jnp.exp(-jnp.abs(z)))


def _rev_cumsum(sp, tri):
    hi = sp.astype(BF16)
    lo = (sp - hi.astype(F32)).astype(BF16)
    return _dot(hi, tri) + _dot(lo, tri)


def _tri(n):
    r = lax.broadcasted_iota(jnp.int32, (n, n), 0)
    c = lax.broadcasted_iota(jnp.int32, (n, n), 1)
    return jnp.where(r >= c, 1.0, 0.0).astype(BF16)


def _head_ones(width):
    r = lax.broadcasted_iota(jnp.int32, (LANE, LANE), 0)
    c = lax.broadcasted_iota(jnp.int32, (LANE, LANE), 1)
    return jnp.where((r < width) == (c < width), 1.0, 0.0).astype(BF16)


def _head_sq_norms(x_bf16, ones_blk):
    xf = x_bf16.astype(F32)
    return _dot((xf * xf).astype(BF16), ones_blk)


def _sb_prompt_kernel(q_ref, k_ref, v_ref, o_ref, kmax_ref, *, tq, t):
    qi = pl.program_id(2)
    q = q_ref[0]
    lane = lax.broadcasted_iota(jnp.int32, (tq, LANE), 1)
    zero = jnp.zeros_like(q)
    qm = (jnp.where(lane < DH_SB, q, zero), jnp.where(lane >= DH_SB, q, zero))
    tri = _tri(tq)
    ones_blk = _head_ones(DH_SB)

    @pl.when(qi == 0)
    def _():
        def body(i, mx):
            kc = k_ref[0, pl.ds(pl.multiple_of(i * tq, tq), tq), :]
            return jnp.maximum(mx, _head_sq_norms(kc, ones_blk))
        mx = lax.fori_loop(0, t // tq, body, jnp.zeros((tq, LANE), F32))
        kmax_ref[...] = jnp.max(mx, axis=0, keepdims=True)

    bound = jnp.sqrt(_head_sq_norms(q, ones_blk) * kmax_ref[...]) * SB_BOUND_SLACK

    def block(j, ccar, acc, qmh, diag):
        start = pl.multiple_of(j * tq, tq)
        kb = k_ref[0, pl.ds(start, tq), :]
        vb = v_ref[0, pl.ds(start, tq), :]
        z = _dot_nt(qmh, kb)
        sp = _softplus(z)
        if diag:
            r = lax.broadcasted_iota(jnp.int32, (tq, tq), 0)
            c = lax.broadcasted_iota(jnp.int32, (tq, tq), 1)
            vis = c < r
            sp = jnp.where(vis, sp, 0.0)
        cl = _rev_cumsum(sp, tri)
        w = jnp.exp(z - ccar - cl)
        if diag:
            w = jnp.where(vis, w, 0.0)
        return ccar + cl[:, 0:1], acc + _dot(w.astype(BF16), vb)

    accs = []
    for hi in range(2):
        bz = bound[:, hi * DH_SB:hi * DH_SB + 1]

        def alive(ccar, bz=bz):
            return (jnp.max(bz - ccar) > -SB_DEAD).astype(jnp.int32)

        ccar, acc = block(qi, jnp.zeros((tq, 1), F32), jnp.zeros((tq, LANE), F32), qm[hi], True)

        def body(c, hi=hi, alive=alive):
            j, _, ccar, acc = c
            ccar, acc = block(j, ccar, acc, qm[hi], False)
            return j - 1, alive(ccar), ccar, acc

        out = lax.while_loop(lambda c: jnp.logical_and(c[0] >= 0, c[1] > 0), body,
                             (qi - 1, alive(ccar), ccar, acc))
        accs.append(out[3])
    o_ref[0] = jnp.where(lane < DH_SB, accs[0], accs[1]).astype(BF16)


def _sb_prompt(q, kb, vb, tq):
    b, t, _ = q.shape
    return pl.pallas_call(
        functools.partial(_sb_prompt_kernel, tq=tq, t=t),
        scratch_shapes=[pltpu.VMEM((1, LANE), F32)],
        grid=(b, D_SB // LANE, t // tq),
        in_specs=[
            pl.BlockSpec((1, tq, LANE), lambda i, h, j: (i, j, h)),
            pl.BlockSpec((1, t, LANE), lambda i, h, j: (i, 0, h)),
            pl.BlockSpec((1, t, LANE), lambda i, h, j: (i, 0, h)),
        ],
        out_specs=pl.BlockSpec((1, tq, LANE), lambda i, h, j: (i, j, h)),
        out_shape=jax.ShapeDtypeStruct((b, t, D_SB), BF16),
        compiler_params=_params("parallel", "parallel", "arbitrary"),
        name="sb_prompt",
    )(q, kb, vb)


def _sb_decode_kernel(q_ref, kp_ref, vp_ref, kn_ref, vn_ref, o_ref, qs_ref, c_ref, acc_ref,
                      *, t, tk, nk, hg):
    j = pl.program_id(2)
    rows = hg * t
    width_all = hg * DH_SB

    @pl.when(j == 0)
    def _():
        q = q_ref[0]
        lane = lax.broadcasted_iota(jnp.int32, (t, width_all), 1)
        for s in range(hg):
            qs_ref[s * t:(s + 1) * t, :] = jnp.where(_lane_band(lane, s, DH_SB), q, jnp.zeros_like(q))
        kn = kn_ref[0]
        z = _dot_nt(qs_ref[...], kn)
        r = _row_query_index(hg, t, t)
        c = lax.broadcasted_iota(jnp.int32, (rows, t), 1)
        vis = c < r
        sp = jnp.where(vis, _softplus(z), 0.0)
        cl = _rev_cumsum(sp, _tri(t))
        w = jnp.where(vis, jnp.exp(z - cl), 0.0).astype(BF16)
        vn = vn_ref[0]
        for s in range(hg):
            rs = slice(s * t, (s + 1) * t)
            acc_ref[rs, :] = _dot(w[rs], vn[:, s * DH_SB:(s + 1) * DH_SB])
        c_ref[...] = cl[:, 0:1]

    kb = kp_ref[0].astype(BF16)
    vb = vp_ref[0].astype(BF16)
    z = _dot_nt(qs_ref[...], kb)
    cl = _rev_cumsum(_softplus(z), _tri(tk))
    ccar = c_ref[...]
    w = jnp.exp(z - ccar - cl).astype(BF16)
    for s in range(hg):
        rs = slice(s * t, (s + 1) * t)
        acc_ref[rs, :] += _dot(w[rs], vb[:, s * DH_SB:(s + 1) * DH_SB])
    c_ref[...] = ccar + cl[:, 0:1]

    @pl.when(j == nk - 1)
    def _():
        for s in range(hg):
            o_ref[0, :, s * DH_SB:(s + 1) * DH_SB] = acc_ref[s * t:(s + 1) * t, :].astype(BF16)


def _sb_decode(q, k_past, v_past, kn, vn, tk, hg):
    b, t, _ = q.shape
    past_len = k_past.shape[1]
    nk = past_len // tk
    ng = H_SB // hg
    wg = hg * DH_SB
    rows = hg * t
    tokb = lambda: pl.BlockSpec((1, t, wg), lambda i, g, j: (i, 0, g))
    past = lambda: pl.BlockSpec((1, tk, wg), lambda i, g, j: (i, nk - 1 - j, g))
    return pl.pallas_call(
        functools.partial(_sb_decode_kernel, t=t, tk=tk, nk=nk, hg=hg),
        grid=(b, ng, nk),
        in_specs=[tokb(), past(), past(), tokb(), tokb()],
        out_specs=tokb(),
        out_shape=jax.ShapeDtypeStruct((b, t, D_SB), BF16),
        scratch_shapes=[pltpu.VMEM((rows, wg), BF16), pltpu.VMEM((rows, 1), F32),
                        pltpu.VMEM((rows, DH_SB), F32)],
        compiler_params=_params("parallel", "parallel", "arbitrary"),
        name="sb_decode",
    )(q, k_past, v_past, kn, vn)


def _post_mix_kernel(*refs, n_in):
    x_ref = refs[0]
    a_refs = refs[1:1 + n_in]
    w_ref, g_ref, wq_ref, mk_ref, mv_ref, wo_ref, o_ref = refs[1 + n_in:]
    a = a_refs[0][0] if n_in == 1 else jnp.concatenate([r[0] for r in a_refs], axis=1)
    x = x_ref[0] + _dot(a, w_ref[...])
    hq = _rms(x, g_ref[...], EPS).astype(BF16)
    q = (_dot(hq, wq_ref[...]) * (DH_MEM ** -0.5)).astype(BF16)
    y = x
    for h in range(H_MEM):
        hs = slice(h * DH_MEM, (h + 1) * DH_MEM)
        s = _dot_nt(q[:, hs], mk_ref[0, :, hs])
        p = jnp.exp(s - jnp.max(s, axis=1, keepdims=True))
        l = jnp.sum(p, axis=1, keepdims=True)
        oh = _dot(p.astype(BF16), mv_ref[0, :, hs]) / l
        y = y + _dot(oh.astype(BF16), wo_ref[hs, :])
    o_ref[0] = y


def _post_mix(x, acts, w, g, wq, mk, mv, wo, tm):
    b, t, d = x.shape
    n_in = len(acts)
    tok = lambda w: pl.BlockSpec((1, tm, w), lambda i, j: (i, j, 0))
    mem = pl.BlockSpec((1, N_MEM, D_MODEL), lambda i, j: (i, 0, 0))
    return pl.pallas_call(
        functools.partial(_post_mix_kernel, n_in=n_in),
        grid=(b, t // tm),
        in_specs=([tok(d)] + [tok(a.shape[-1]) for a in acts]
                  + [_full(w.shape), _full((1, d)), _full((d, d)), mem, mem, _full((d, d))]),
        out_specs=tok(d),
        out_shape=jax.ShapeDtypeStruct((b, t, d), F32),
        compiler_params=_params("parallel", "parallel"),
        name="post_mix",
    )(x, *acts, w, g, wq, mk, mv, wo)


def _ffn_kernel(x_ref, g_ref, wu_ref, wd_ref, gf_ref, o_ref, *, fc, final):
    x = x_ref[...]
    h = _rms(x, g_ref[...], EPS).astype(BF16)
    y = x
    for c in range(D_FF // fc):
        u = jnp.maximum(_dot(h, wu_ref[:, c * fc:(c + 1) * fc]), 0.0)
        y = y + _dot((u * u).astype(BF16), wd_ref[c * fc:(c + 1) * fc, :])
    if final:
        y = _rms(y, gf_ref[...], EPS)
    o_ref[...] = y


def _ffn(x2d, g, wu, wd, gf, final, tm, fc):
    m, d = x2d.shape
    return pl.pallas_call(
        functools.partial(_ffn_kernel, fc=fc, final=final),
        grid=(m // tm,),
        in_specs=[pl.BlockSpec((tm, d), lambda i: (i, 0)), _full((1, d)), _full((d, D_FF)),
                  _full((D_FF, d)), _full((1, d))],
        out_specs=pl.BlockSpec((tm, d), lambda i: (i, 0)),
        out_shape=jax.ShapeDtypeStruct((m, d), F32),
        compiler_params=_params("parallel"),
        name="ffn",
    )(x2d, g, wu, wd, gf)


def _rope_tables(pos):
    half = DH_QK // 2
    inv = jnp.power(ROPE_THETA, -jnp.arange(half, dtype=F32) * (2.0 / DH_QK))
    ang = pos.astype(F32)[:, None] * inv[None, :]
    cos, sin = jnp.cos(ang), jnp.sin(ang)
    reps = LANE // DH_QK
    return (jnp.tile(jnp.concatenate([cos, cos], axis=1), (1, reps)),
            jnp.tile(jnp.concatenate([-sin, sin], axis=1), (1, reps)))


def _row(v):
    return v.reshape(1, -1)


def _run(x, mem_kb, mem_vb, past, p, tm, tq):
    b, t, d = x.shape
    past_len = 0 if past is None else past[0].shape[2]
    pos = jnp.arange(past_len, past_len + t)
    cos, sin = _rope_tables(pos)
    outs = {}
    for i in range(DEPTH):
        if i % 2 == 0:
            e = i // 2
            lam_init = 0.8 - 0.6 * math.exp(-0.3 * i)
            state = jnp.zeros((b, CONV_W - 1, D_CONV), F32) if past is None else past[2][e]
            q, k, kb, v, vb, gc, nc = _even_in(x, _row(p['norm_mix'][i]), p['w_in_even_b'][e], cos, sin,
                                               p['conv_w'][e], state, tm)
            lam_args = (_row(p['lambda_q1'][e]), _row(p['lambda_k1'][e]), _row(p['lambda_q2'][e]),
                        _row(p['lambda_k2'][e]), _row(p['subln_gain'][e]))
            if past is None:
                o = _diff_prompt(q, kb, vb, *lam_args, lam_init, tq, min(1024, t))
            else:
                o = _diff_decode(q, past[0][e].reshape(b, past_len, Q_DIFF),
                                 past[1][e].reshape(b, past_len, D_DIFF), kb, vb, *lam_args, lam_init,
                                 min(512, past_len))
            outs['dk'] = k.reshape(1, b, t, 2 * H_DIFF, DH_QK)
            outs['dv'] = v.reshape(1, b, t, H_DIFF, DV_DIFF)
            outs['conv'] = nc[None]
            acts = [o, gc]
            w_out = p['w_out_even_b'][e]
        else:
            o_idx = i // 2
            segs = ((0, D_SB, False, True, DH_SB ** -0.5), (D_SB, D_SB, True, True, 1.0),
                    (2 * D_SB, D_SB, True, True, 1.0))
            q, k, kb, v, vb = _norm_proj(x, p['norm_mix'][i].reshape(1, 1, d),
                                         p['w_in_odd_b'][o_idx][None], segs, tm)
            if past is None:
                o = _sb_prompt(q, kb, vb, tq)
            else:
                o = _sb_decode(q, past[3][o_idx].reshape(b, past_len, D_SB),
                               past[4][o_idx].reshape(b, past_len, D_SB), kb, vb,
                               min(512, past_len), 4)
            outs['sk'] = k.reshape(1, b, t, H_SB, DH_SB)
            outs['sv'] = v.reshape(1, b, t, H_SB, DH_SB)
            acts = [o]
            w_out = p['w_out_odd_b'][o_idx]
        x = _post_mix(x, acts, w_out, _row(p['norm_cross'][i]), p['w_q_mem_b'][i], mem_kb[i], mem_vb[i],
                      p['w_o_mem_b'][i], tm)
        m = b * t
        x = _ffn(x.reshape(m, d), _row(p['norm_ffn'][i]), p['w_ffn_up_b'][i], p['w_ffn_down_b'][i],
                 _row(p['norm_final']), i == DEPTH - 1, min(512, m), 1024).reshape(b, t, d)
    return x, outs


def kernel(x_prompt, x_sample, cache_diff_k, cache_diff_v, state_conv, cache_sb_k, cache_sb_v, cache_mem_k, cache_mem_v, mem_prompt, w_in_even, w_out_even, lambda_q1, lambda_k1, lambda_q2, lambda_k2, subln_gain, conv_w, w_in_odd, w_out_odd, norm_mix, norm_mem, norm_cross, w_q_mem, w_k_mem, w_v_mem, w_o_mem, norm_ffn, w_ffn_up, w_ffn_down, norm_final):
    p = dict(norm_mix=norm_mix, norm_cross=norm_cross, norm_ffn=norm_ffn, norm_final=norm_final,
             lambda_q1=lambda_q1, lambda_k1=lambda_k1, lambda_q2=lambda_q2, lambda_k2=lambda_k2,
             subln_gain=subln_gain, conv_w=conv_w)
    for name, w in (('w_in_even', w_in_even), ('w_out_even', w_out_even), ('w_in_odd', w_in_odd),
                    ('w_out_odd', w_out_odd), ('w_q_mem', w_q_mem), ('w_o_mem', w_o_mem),
                    ('w_ffn_up', w_ffn_up), ('w_ffn_down', w_ffn_down)):
        p[name + '_b'] = w.astype(BF16)

    bp = mem_prompt.shape[0]
    w_kv = jnp.concatenate([w_k_mem, w_v_mem], axis=-1).astype(BF16)
    segs = ((0, D_MODEL, True, True, 1.0), (D_MODEL, D_MODEL, True, True, 1.0))
    mk, mkb, mv, mvb = _norm_proj(mem_prompt, norm_mem.reshape(DEPTH, 1, D_MODEL), w_kv, segs, N_MEM)
    p_mem_k = mk.reshape(DEPTH, bp, N_MEM, H_MEM, DH_MEM)
    p_mem_v = mv.reshape(DEPTH, bp, N_MEM, H_MEM, DH_MEM)
    mkb = mkb.reshape(DEPTH, bp, N_MEM, D_MODEL)
    mvb = mvb.reshape(DEPTH, bp, N_MEM, D_MODEL)

    t_p = x_prompt.shape[1]
    y_prompt, po = _run(x_prompt, mkb, mvb, None, p, min(512, t_p), min(256, t_p))

    bs = x_sample.shape[0]
    t_s = x_sample.shape[1]
    cmk = cache_mem_k.reshape(DEPTH, bs, N_MEM, D_MODEL).astype(BF16)
    cmv = cache_mem_v.reshape(DEPTH, bs, N_MEM, D_MODEL).astype(BF16)
    y_sample, so = _run(x_sample, cmk, cmv,
                        (cache_diff_k, cache_diff_v, state_conv, cache_sb_k, cache_sb_v), p, t_s, t_s)

    return (y_prompt, y_sample, po['dk'], po['dv'], po['conv'], po['sk'], po['sv'], p_mem_k, p_mem_v,
            so['dk'], so['dv'], so['conv'], so['sk'], so['sv'])
```

```python
import functools
import math

import jax
import jax.numpy as jnp
from jax import lax
from jax.experimental import pallas as pl
from jax.experimental.pallas import tpu as pltpu

F32 = jnp.float32
BF16 = jnp.bfloat16

D_MODEL = 1024
DEPTH = 2
CHUNK = 64
H_DIFF = 4
DH_QK = 64
DV_DIFF = 2 * DH_QK
D_DIFF = H_DIFF * DV_DIFF
Q_DIFF = 2 * H_DIFF * DH_QK
D_CONV = D_MODEL - D_DIFF
CONV_W = 3
EVEN_IN = 2 * Q_DIFF + D_DIFF + 3 * D_CONV
H_SB = 16
DH_SB = 64
D_SB = H_SB * DH_SB
N_MEM = 256
H_MEM = 4
DH_MEM = D_MODEL // H_MEM
D_FF = 4 * D_MODEL
ROPE_THETA = 10000.0
EPS = 1e-6
SUBLN_EPS = 1e-5
NEG_INF = -1e30
SB_DEAD = 105.0
SB_BOUND_SLACK = 1.01

LANE = 128
SUBLANE = 8
VMEM_LIMIT = 52 * 1024 * 1024


def _params(*sem):
    return pltpu.CompilerParams(dimension_semantics=sem, vmem_limit_bytes=VMEM_LIMIT)


def _rms(x, g, eps):
    ms = jnp.mean(x * x, axis=-1, keepdims=True)
    return x * lax.rsqrt(ms + eps) * g


def _dot(a, b):
    return jnp.dot(a, b, preferred_element_type=F32)


def _dot_nt(a, b):
    return lax.dot_general(a, b, (((1,), (1,)), ((), ())), preferred_element_type=F32)


def _full(shape):
    n = len(shape)
    return pl.BlockSpec(shape, lambda *_: (0,) * n)


def _chunk_of(pos):
    shift = CHUNK.bit_length() - 1
    assert 1 << shift == CHUNK
    return lax.shift_right_logical(pos, shift)


def _row_query_index(groups, t, width):
    r = lax.broadcasted_iota(jnp.int32, (t, width), 0)
    return jnp.concatenate([r] * groups, axis=0)


def _lane_band(lane, s, width):
    return (lane >= s * width) & (lane < (s + 1) * width)


def _even_in_kernel(x_ref, g_ref, w_ref, cos_ref, sin_ref, cw_ref, st_ref,
                    q_ref, k_ref, kb_ref, v_ref, vb_ref, gc_ref, nc_ref, ext_ref, *, tm, nt):
    t = pl.program_id(1)
    h = _rms(x_ref[0], g_ref[...], EPS).astype(BF16)

    def proj(lo, width):
        return _dot(h, w_ref[:, lo:lo + width])

    cos = cos_ref[...]
    sin = sin_ref[...]
    lane = lax.broadcasted_iota(jnp.int32, (tm, LANE), 1)
    first_half = (lane & (DH_QK // 2)) == 0

    def rope_group(yj):
        sw = jnp.where(first_half, pltpu.roll(yj, LANE - DH_QK // 2, 1), pltpu.roll(yj, DH_QK // 2, 1))
        return yj * cos + sw * sin

    yq = proj(0, Q_DIFF)
    yk = proj(Q_DIFF, Q_DIFF)
    for j in range(Q_DIFF // LANE):
        sl = slice(j * LANE, (j + 1) * LANE)
        q_ref[0, :, sl] = (rope_group(yq[:, sl]) * (DH_QK ** -0.5)).astype(BF16)
        kj = rope_group(yk[:, sl])
        k_ref[0, :, sl] = kj
        kb_ref[0, :, sl] = kj.astype(BF16)
    yv = proj(2 * Q_DIFF, D_DIFF)
    v_ref[0] = yv
    vb_ref[0] = yv.astype(BF16)

    base = 2 * Q_DIFF + D_DIFF
    gate_b = proj(base, D_CONV)
    cu = proj(base + D_CONV, D_CONV) * proj(base + 2 * D_CONV, D_CONV)

    @pl.when(t == 0)
    def _():
        ext_ref[SUBLANE - 2:SUBLANE, :] = st_ref[0]

    ext_ref[SUBLANE:SUBLANE + tm, :] = cu
    cw = cw_ref[...]
    conv = (ext_ref[SUBLANE - 2:SUBLANE - 2 + tm, :] * cw[0:1, :]
            + ext_ref[SUBLANE - 1:SUBLANE - 1 + tm, :] * cw[1:2, :]
            + cu * cw[2:3, :])
    gc_ref[0] = (gate_b * conv).astype(BF16)
    ext_ref[0:SUBLANE, :] = ext_ref[tm:tm + SUBLANE, :]

    @pl.when(t == nt - 1)
    def _():
        nc_ref[0] = ext_ref[tm + SUBLANE - 2:tm + SUBLANE, :]


def _even_in(x, g, wb, cos, sin, cw, state, tm):
    b, t, _ = x.shape
    nt = t // tm
    tok = lambda w: pl.BlockSpec((1, tm, w), lambda i, j: (i, j, 0))
    outs = [
        jax.ShapeDtypeStruct((b, t, Q_DIFF), BF16),
        jax.ShapeDtypeStruct((b, t, Q_DIFF), F32),
        jax.ShapeDtypeStruct((b, t, Q_DIFF), BF16),
        jax.ShapeDtypeStruct((b, t, D_DIFF), F32),
        jax.ShapeDtypeStruct((b, t, D_DIFF), BF16),
        jax.ShapeDtypeStruct((b, t, D_CONV), BF16),
        jax.ShapeDtypeStruct((b, CONV_W - 1, D_CONV), F32),
    ]
    return pl.pallas_call(
        functools.partial(_even_in_kernel, tm=tm, nt=nt),
        grid=(b, nt),
        in_specs=[
            tok(D_MODEL),
            _full((1, D_MODEL)),
            _full((D_MODEL, EVEN_IN)),
            pl.BlockSpec((tm, LANE), lambda i, j: (j, 0)),
            pl.BlockSpec((tm, LANE), lambda i, j: (j, 0)),
            _full((CONV_W, D_CONV)),
            pl.BlockSpec((1, CONV_W - 1, D_CONV), lambda i, j: (i, 0, 0)),
        ],
        out_specs=[tok(Q_DIFF), tok(Q_DIFF), tok(Q_DIFF), tok(D_DIFF), tok(D_DIFF), tok(D_CONV),
                   pl.BlockSpec((1, CONV_W - 1, D_CONV), lambda i, j: (i, 0, 0))],
        out_shape=outs,
        scratch_shapes=[pltpu.VMEM((tm + 2 * SUBLANE, D_CONV), F32)],
        compiler_params=_params("parallel", "arbitrary"),
        name="even_in",
    )(x, g, wb, cos, sin, cw, state)


def _norm_proj_kernel(x_ref, g_ref, w_ref, *out_refs, segs):
    h = _rms(x_ref[0], g_ref[0], EPS).astype(BF16)
    i = 0
    for lo, width, want_f32, want_bf16, scale in segs:
        y = _dot(h, w_ref[0, :, lo:lo + width])
        if scale != 1.0:
            y = y * scale
        if want_f32:
            out_refs[i][0] = y
            i += 1
        if want_bf16:
            out_refs[i][0] = y.astype(BF16)
            i += 1


def _norm_proj(x, g, wb, segs, tm):
    b, t, d = x.shape
    n = wb.shape[-1]
    ng = wb.shape[0]
    nt = t // tm
    outs, specs = [], []
    for lo, width, want_f32, want_bf16, _ in segs:
        for want, dt in ((want_f32, F32), (want_bf16, BF16)):
            if want:
                outs.append(jax.ShapeDtypeStruct((ng * b, t, width), dt))
                specs.append(pl.BlockSpec((1, tm, width), lambda i, j: (i, j, 0)))
    return pl.pallas_call(
        functools.partial(_norm_proj_kernel, segs=segs),
        grid=(ng * b, nt),
        in_specs=[
            pl.BlockSpec((1, tm, d), lambda i, j: (i % b, j, 0)),
            pl.BlockSpec((1, 1, d), lambda i, j: (i // b, 0, 0)),
            pl.BlockSpec((1, d, n), lambda i, j: (i // b, 0, 0)),
        ],
        out_specs=specs,
        out_shape=outs,
        compiler_params=_params("parallel", "parallel"),
        name="norm_proj",
    )(x, g, wb)


def _lambda_full(lq1_ref, lk1_ref, lq2_ref, lk2_ref, lam_init):
    s1 = jnp.sum(lq1_ref[...] * lk1_ref[...], axis=1, keepdims=True)
    s2 = jnp.sum(lq2_ref[...] * lk2_ref[...], axis=1, keepdims=True)
    return jnp.exp(s1) - jnp.exp(s2) + lam_init


def _subln(o, gsub_ref, lam_init):
    return _rms(o, gsub_ref[...], SUBLN_EPS) * (1.0 - lam_init)


def _diff_prompt_kernel(q_ref, k_ref, v_ref, lq1_ref, lk1_ref, lq2_ref, lk2_ref, gsub_ref,
                        o_ref, *, tq, tk, lam_init):
    qi = pl.program_id(2)
    ratio = tk // tq
    q = q_ref[0]
    lane = lax.broadcasted_iota(jnp.int32, (tq, LANE), 1)
    zero = jnp.zeros_like(q)
    qm = (jnp.where(lane < DH_QK, q, zero), jnp.where(lane >= DH_QK, q, zero))
    ones = jnp.ones((tk, LANE), BF16)

    def step(j, carry, masked):
        start = pl.multiple_of(j * tk, tk)
        kb = k_ref[0, pl.ds(start, tk), :]
        vb = jnp.concatenate([v_ref[0, pl.ds(start, tk), :], ones], axis=1)
        if masked:
            r = _chunk_of(lax.broadcasted_iota(jnp.int32, (tq, tk), 0)) + lax.rem(qi, ratio) * (tq // CHUNK)
            c = _chunk_of(lax.broadcasted_iota(jnp.int32, (tq, tk), 1))
            vis = c <= r
        ss = [_dot_nt(qm[mi], kb) for mi in range(2)]
        if masked:
            ss = [jnp.where(vis, s, NEG_INF) for s in ss]
        ms = [jnp.maximum(carry[mi][0], jnp.max(ss[mi], axis=1, keepdims=True)) for mi in range(2)]
        ps = [jnp.exp(ss[mi] - ms[mi]).astype(BF16) for mi in range(2)]
        pv = [_dot(ps[mi], vb) for mi in range(2)]
        return tuple((ms[mi], jnp.exp(carry[mi][0] - ms[mi]) * carry[mi][1] + pv[mi]) for mi in range(2))

    init = tuple((jnp.full((tq, 1), NEG_INF, F32), jnp.zeros((tq, 2 * LANE), F32)) for _ in range(2))
    n_full = lax.div(qi, ratio)
    carry = lax.fori_loop(0, n_full, lambda j, c: step(j, c, False), init)
    carry = step(n_full, carry, True)
    lam = _lambda_full(lq1_ref, lk1_ref, lq2_ref, lk2_ref, lam_init)
    (_, a1), (_, a2) = carry
    o = a1[:, :LANE] / a1[:, LANE:] - lam * (a2[:, :LANE] / a2[:, LANE:])
    o_ref[0] = _subln(o, gsub_ref, lam_init).astype(BF16)


def _diff_prompt(q, kb, vb, lq1, lk1, lq2, lk2, gsub, lam_init, tq, tk):
    b, t, _ = q.shape
    lam_spec = _full((1, DH_QK))
    return pl.pallas_call(
        functools.partial(_diff_prompt_kernel, tq=tq, tk=tk, lam_init=lam_init),
        grid=(b, H_DIFF, t // tq),
        in_specs=[
            pl.BlockSpec((1, tq, LANE), lambda i, h, j: (i, j, h)),
            pl.BlockSpec((1, t, LANE), lambda i, h, j: (i, 0, h)),
            pl.BlockSpec((1, t, LANE), lambda i, h, j: (i, 0, h)),
            lam_spec, lam_spec, lam_spec, lam_spec,
            _full((1, DV_DIFF)),
        ],
        out_specs=pl.BlockSpec((1, tq, LANE), lambda i, h, j: (i, j, h)),
        out_shape=jax.ShapeDtypeStruct((b, t, D_DIFF), BF16),
        compiler_params=_params("parallel", "parallel", "arbitrary"),
        name="diff_prompt",
    )(q, kb, vb, lq1, lk1, lq2, lk2, gsub)


def _diff_decode_kernel(q_ref, kp_ref, vp_ref, kn_ref, vn_ref, lq1_ref, lk1_ref, lq2_ref, lk2_ref,
                        gsub_ref, o_ref, qs_ref, m_ref, l_ref, acc_ref, *, t, tk, nk, past_len,
                        lam_init):
    j = pl.program_id(1)
    nsub = 2 * H_DIFF
    rows = nsub * t

    @pl.when(j == 0)
    def _():
        q = q_ref[0]
        lane = lax.broadcasted_iota(jnp.int32, (t, Q_DIFF), 1)
        for s in range(nsub):
            qs_ref[s * t:(s + 1) * t, :] = jnp.where(_lane_band(lane, s, DH_QK), q, jnp.zeros_like(q))
        m_ref[...] = jnp.full(m_ref.shape, NEG_INF, F32)
        l_ref[...] = jnp.zeros(l_ref.shape, F32)
        acc_ref[...] = jnp.zeros(acc_ref.shape, F32)

    def update(kb, vb, k_start, width):
        s = _dot_nt(qs_ref[...], kb)
        q_pos = past_len + _row_query_index(nsub, t, width)
        k_pos = k_start + lax.broadcasted_iota(jnp.int32, (rows, width), 1)
        s = jnp.where(_chunk_of(k_pos) <= _chunk_of(q_pos), s, NEG_INF)
        m = m_ref[...]
        m_new = jnp.maximum(m, jnp.max(s, axis=1, keepdims=True))
        alpha = jnp.exp(m - m_new)
        p = jnp.exp(s - m_new)
        l_ref[...] = alpha * l_ref[...] + jnp.sum(p, axis=1, keepdims=True)
        m_ref[...] = m_new
        pb = p.astype(BF16)
        for h in range(H_DIFF):
            rs = slice(2 * h * t, (2 * h + 2) * t)
            acc_ref[rs, :] = alpha[rs] * acc_ref[rs, :] + _dot(pb[rs], vb[:, h * LANE:(h + 1) * LANE])

    update(kp_ref[0].astype(BF16), vp_ref[0].astype(BF16), j * tk, tk)

    @pl.when(j == nk - 1)
    def _():
        update(kn_ref[0], vn_ref[0], past_len, t)
        lam = _lambda_full(lq1_ref, lk1_ref, lq2_ref, lk2_ref, lam_init)
        on = acc_ref[...] / l_ref[...]
        for h in range(H_DIFF):
            o = on[2 * h * t:(2 * h + 1) * t] - lam * on[(2 * h + 1) * t:(2 * h + 2) * t]
            o_ref[0, :, h * LANE:(h + 1) * LANE] = _subln(o, gsub_ref, lam_init).astype(BF16)


def _diff_decode(q, k_past, v_past, kn, vn, lq1, lk1, lq2, lk2, gsub, lam_init, tk):
    b, t, _ = q.shape
    past_len = k_past.shape[1]
    nk = past_len // tk
    rows = 2 * H_DIFF * t
    lam_spec = _full((1, DH_QK))
    tokb = lambda w: pl.BlockSpec((1, t, w), lambda i, j: (i, 0, 0))
    return pl.pallas_call(
        functools.partial(_diff_decode_kernel, t=t, tk=tk, nk=nk, past_len=past_len,
                          lam_init=lam_init),
        grid=(b, nk),
        in_specs=[
            tokb(Q_DIFF),
            pl.BlockSpec((1, tk, Q_DIFF), lambda i, j: (i, j, 0)),
            pl.BlockSpec((1, tk, D_DIFF), lambda i, j: (i, j, 0)),
            tokb(Q_DIFF), tokb(D_DIFF),
            lam_spec, lam_spec, lam_spec, lam_spec,
            _full((1, DV_DIFF)),
        ],
        out_specs=tokb(D_DIFF),
        out_shape=jax.ShapeDtypeStruct((b, t, D_DIFF), BF16),
        scratch_shapes=[pltpu.VMEM((rows, Q_DIFF), BF16), pltpu.VMEM((rows, 1), F32),
                        pltpu.VMEM((rows, 1), F32), pltpu.VMEM((rows, DV_DIFF), F32)],
        compiler_params=_params("parallel", "arbitrary"),
        name="diff_decode",
    )(q, k_past, v_past, kn, vn, lq1, lk1, lq2, lk2, gsub)


def _softplus(z):
    return jnp.maximum(z, 0.0) + jnp.log(1.0 + jnp.exp(-jnp.abs(z)))


def _rev_cumsum(sp, tri):
    hi = sp.astype(BF16)
    lo = (sp - hi.astype(F32)).astype(BF16)
    return _dot(hi, tri) + _dot(lo, tri)


def _tri(n):
    r = lax.broadcasted_iota(jnp.int32, (n, n), 0)
    c = lax.broadcasted_iota(jnp.int32, (n, n), 1)
    return jnp.where(r >= c, 1.0, 0.0).astype(BF16)


def _head_ones(width):
    r = lax.broadcasted_iota(jnp.int32, (LANE, LANE), 0)
    c = lax.broadcasted_iota(jnp.int32, (LANE, LANE), 1)
    return jnp.where((r < width) == (c < width), 1.0, 0.0).astype(BF16)


def _head_sq_norms(x_bf16, ones_blk):
    xf = x_bf16.astype(F32)
    return _dot((xf * xf).astype(BF16), ones_blk)


def _sb_prompt_kernel(q_ref, k_ref, v_ref, o_ref, kmax_ref, *, tq, t):
    qi = pl.program_id(2)
    q = q_ref[0]
    lane = lax.broadcasted_iota(jnp.int32, (tq, LANE), 1)
    zero = jnp.zeros_like(q)
    qm = (jnp.where(lane < DH_SB, q, zero), jnp.where(lane >= DH_SB, q, zero))
    tri = _tri(tq)
    ones_blk = _head_ones(DH_SB)

    @pl.when(qi == 0)
    def _():
        def body(i, mx):
            kc = k_ref[0, pl.ds(pl.multiple_of(i * tq, tq), tq), :]
            return jnp.maximum(mx, _head_sq_norms(kc, ones_blk))
        mx = lax.fori_loop(0, t // tq, body, jnp.zeros((tq, LANE), F32))
        kmax_ref[...] = jnp.max(mx, axis=0, keepdims=True)

    bound = jnp.sqrt(_head_sq_norms(q, ones_blk) * kmax_ref[...]) * SB_BOUND_SLACK

    qs = jnp.concatenate(qm, axis=0)
    bz = jnp.concatenate([bound[:, 0:1], bound[:, DH_SB:DH_SB + 1]], axis=0)

    def alive(ccar):
        return (jnp.max(bz - ccar) > -SB_DEAD).astype(jnp.int32)

    def blk(ref, j):
        return ref[0, pl.ds(pl.multiple_of(j * tq, tq), tq), :]

    has_prev = qi > 0
    jp = jnp.maximum(qi - 1, 0)
    z_r = _dot_nt(qs, blk(k_ref, qi))
    z_l = _dot_nt(qs, blk(k_ref, jp))
    vis = lax.broadcasted_iota(jnp.int32, (2 * tq, tq), 1) < _row_query_index(2, tq, tq)
    sp_r = jnp.where(vis, _softplus(z_r), 0.0)
    sp_l = jnp.where(has_prev, _softplus(z_l), 0.0)
    cl_r = _rev_cumsum(sp_r, tri)
    cl_l = _rev_cumsum(sp_l, tri) + cl_r[:, 0:1]
    w_r = jnp.where(vis, jnp.exp(z_r - cl_r), 0.0).astype(BF16)
    w_l = jnp.where(has_prev, jnp.exp(z_l - cl_l), 0.0).astype(BF16)
    acc = _dot(jnp.concatenate([w_l, w_r], axis=1),
               jnp.concatenate([blk(v_ref, jp), blk(v_ref, qi)], axis=0))
    ccar = cl_l[:, 0:1]

    def body(c):
        j, _, ccar, acc = c
        z = _dot_nt(qs, blk(k_ref, j))
        cl = _rev_cumsum(_softplus(z), tri)
        w = jnp.exp(z - ccar - cl).astype(BF16)
        ccar = ccar + cl[:, 0:1]
        return j - 1, alive(ccar), ccar, acc + _dot(w, blk(v_ref, j))

    out = lax.while_loop(lambda c: jnp.logical_and(c[0] >= 0, c[1] > 0), body,
                         (qi - 2, alive(ccar), ccar, acc))
    acc = out[3]
    o_ref[0] = jnp.where(lane < DH_SB, acc[:tq], acc[tq:]).astype(BF16)


def _sb_prompt(q, kb, vb, tq):
    b, t, _ = q.shape
    return pl.pallas_call(
        functools.partial(_sb_prompt_kernel, tq=tq, t=t),
        scratch_shapes=[pltpu.VMEM((1, LANE), F32)],
        grid=(b, D_SB // LANE, t // tq),
        in_specs=[
            pl.BlockSpec((1, tq, LANE), lambda i, h, j: (i, j, h)),
            pl.BlockSpec((1, t, LANE), lambda i, h, j: (i, 0, h)),
            pl.BlockSpec((1, t, LANE), lambda i, h, j: (i, 0, h)),
        ],
        out_specs=pl.BlockSpec((1, tq, LANE), lambda i, h, j: (i, j, h)),
        out_shape=jax.ShapeDtypeStruct((b, t, D_SB), BF16),
        compiler_params=_params("parallel", "parallel", "arbitrary"),
        name="sb_prompt",
    )(q, kb, vb)


def _sb_decode_kernel(q_ref, kp_ref, vp_ref, kn_ref, vn_ref, o_ref, qs_ref, c_ref, acc_ref,
                      *, t, tk, nk, hg):
    j = pl.program_id(2)
    rows = hg * t
    width_all = hg * DH_SB

    @pl.when(j == 0)
    def _():
        q = q_ref[0]
        lane = lax.broadcasted_iota(jnp.int32, (t, width_all), 1)
        for s in range(hg):
            qs_ref[s * t:(s + 1) * t, :] = jnp.where(_lane_band(lane, s, DH_SB), q, jnp.zeros_like(q))
        kn = kn_ref[0]
        z = _dot_nt(qs_ref[...], kn)
        r = _row_query_index(hg, t, t)
        c = lax.broadcasted_iota(jnp.int32, (rows, t), 1)
        vis = c < r
        sp = jnp.where(vis, _softplus(z), 0.0)
        cl = _rev_cumsum(sp, _tri(t))
        w = jnp.where(vis, jnp.exp(z - cl), 0.0).astype(BF16)
        vn = vn_ref[0]
        for s in range(hg):
            rs = slice(s * t, (s + 1) * t)
            acc_ref[rs, :] = _dot(w[rs], vn[:, s * DH_SB:(s + 1) * DH_SB])
        c_ref[...] = cl[:, 0:1]

    kb = kp_ref[0].astype(BF16)
    vb = vp_ref[0].astype(BF16)
    z = _dot_nt(qs_ref[...], kb)
    cl = _rev_cumsum(_softplus(z), _tri(tk))
    ccar = c_ref[...]
    w = jnp.exp(z - ccar - cl).astype(BF16)
    for s in range(hg):
        rs = slice(s * t, (s + 1) * t)
        acc_ref[rs, :] += _dot(w[rs], vb[:, s * DH_SB:(s + 1) * DH_SB])
    c_ref[...] = ccar + cl[:, 0:1]

    @pl.when(j == nk - 1)
    def _():
        for s in range(hg):
            o_ref[0, :, s * DH_SB:(s + 1) * DH_SB] = acc_ref[s * t:(s + 1) * t, :].astype(BF16)


def _sb_decode(q, k_past, v_past, kn, vn, tk, hg):
    b, t, _ = q.shape
    past_len = k_past.shape[1]
    nk = past_len // tk
    ng = H_SB // hg
    wg = hg * DH_SB
    rows = hg * t
    tokb = lambda: pl.BlockSpec((1, t, wg), lambda i, g, j: (i, 0, g))
    past = lambda: pl.BlockSpec((1, tk, wg), lambda i, g, j: (i, nk - 1 - j, g))
    return pl.pallas_call(
        functools.partial(_sb_decode_kernel, t=t, tk=tk, nk=nk, hg=hg),
        grid=(b, ng, nk),
        in_specs=[tokb(), past(), past(), tokb(), tokb()],
        out_specs=tokb(),
        out_shape=jax.ShapeDtypeStruct((b, t, D_SB), BF16),
        scratch_shapes=[pltpu.VMEM((rows, wg), BF16), pltpu.VMEM((rows, 1), F32),
                        pltpu.VMEM((rows, DH_SB), F32)],
        compiler_params=_params("parallel", "parallel", "arbitrary"),
        name="sb_decode",
    )(q, k_past, v_past, kn, vn)


def _post_mix_kernel(*refs, n_in):
    x_ref = refs[0]
    a_refs = refs[1:1 + n_in]
    w_ref, g_ref, wq_ref, mk_ref, mv_ref, wo_ref, o_ref = refs[1 + n_in:]
    a = a_refs[0][0] if n_in == 1 else jnp.concatenate([r[0] for r in a_refs], axis=1)
    x = x_ref[0] + _dot(a, w_ref[...])
    hq = _rms(x, g_ref[...], EPS).astype(BF16)
    q = (_dot(hq, wq_ref[...]) * (DH_MEM ** -0.5)).astype(BF16)
    y = x
    for h in range(H_MEM):
        hs = slice(h * DH_MEM, (h + 1) * DH_MEM)
        s = _dot_nt(q[:, hs], mk_ref[0, :, hs])
        p = jnp.exp(s - jnp.max(s, axis=1, keepdims=True))
        l = jnp.sum(p, axis=1, keepdims=True)
        oh = _dot(p.astype(BF16), mv_ref[0, :, hs]) / l
        y = y + _dot(oh.astype(BF16), wo_ref[hs, :])
    o_ref[0] = y


def _post_mix(x, acts, w, g, wq, mk, mv, wo, tm):
    b, t, d = x.shape
    n_in = len(acts)
    tok = lambda w: pl.BlockSpec((1, tm, w), lambda i, j: (i, j, 0))
    mem = pl.BlockSpec((1, N_MEM, D_MODEL), lambda i, j: (i, 0, 0))
    return pl.pallas_call(
        functools.partial(_post_mix_kernel, n_in=n_in),
        grid=(b, t // tm),
        in_specs=([tok(d)] + [tok(a.shape[-1]) for a in acts]
                  + [_full(w.shape), _full((1, d)), _full((d, d)), mem, mem, _full((d, d))]),
        out_specs=tok(d),
        out_shape=jax.ShapeDtypeStruct((b, t, d), F32),
        compiler_params=_params("parallel", "parallel"),
        name="post_mix",
    )(x, *acts, w, g, wq, mk, mv, wo)


def _ffn_kernel(x_ref, g_ref, wu_ref, wd_ref, gf_ref, o_ref, *, fc, final):
    x = x_ref[...]
    h = _rms(x, g_ref[...], EPS).astype(BF16)
    y = x
    for c in range(D_FF // fc):
        u = jnp.maximum(_dot(h, wu_ref[:, c * fc:(c + 1) * fc]), 0.0)
        y = y + _dot((u * u).astype(BF16), wd_ref[c * fc:(c + 1) * fc, :])
    if final:
        y = _rms(y, gf_ref[...], EPS)
    o_ref[...] = y


def _ffn(x2d, g, wu, wd, gf, final, tm, fc):
    m, d = x2d.shape
    return pl.pallas_call(
        functools.partial(_ffn_kernel, fc=fc, final=final),
        grid=(m // tm,),
        in_specs=[pl.BlockSpec((tm, d), lambda i: (i, 0)), _full((1, d)), _full((d, D_FF)),
                  _full((D_FF, d)), _full((1, d))],
        out_specs=pl.BlockSpec((tm, d), lambda i: (i, 0)),
        out_shape=jax.ShapeDtypeStruct((m, d), F32),
        compiler_params=_params("parallel"),
        name="ffn",
    )(x2d, g, wu, wd, gf)


def _rope_tables(pos):
    half = DH_QK // 2
    inv = jnp.power(ROPE_THETA, -jnp.arange(half, dtype=F32) * (2.0 / DH_QK))
    ang = pos.astype(F32)[:, None] * inv[None, :]
    cos, sin = jnp.cos(ang), jnp.sin(ang)
    reps = LANE // DH_QK
    return (jnp.tile(jnp.concatenate([cos, cos], axis=1), (1, reps)),
            jnp.tile(jnp.concatenate([-sin, sin], axis=1), (1, reps)))


def _row(v):
    return v.reshape(1, -1)


def _run(x, mem_kb, mem_vb, past, p, tm, tq):
    b, t, d = x.shape
    past_len = 0 if past is None else past[0].shape[2]
    pos = jnp.arange(past_len, past_len + t)
    cos, sin = _rope_tables(pos)
    outs = {}
    for i in range(DEPTH):
        if i % 2 == 0:
            e = i // 2
            lam_init = 0.8 - 0.6 * math.exp(-0.3 * i)
            state = jnp.zeros((b, CONV_W - 1, D_CONV), F32) if past is None else past[2][e]
            q, k, kb, v, vb, gc, nc = _even_in(x, _row(p['norm_mix'][i]), p['w_in_even_b'][e], cos, sin,
                                               p['conv_w'][e], state, tm)
            lam_args = (_row(p['lambda_q1'][e]), _row(p['lambda_k1'][e]), _row(p['lambda_q2'][e]),
                        _row(p['lambda_k2'][e]), _row(p['subln_gain'][e]))
            if past is None:
                o = _diff_prompt(q, kb, vb, *lam_args, lam_init, min(4 * tq, t), min(1024, t))
            else:
                o = _diff_decode(q, past[0][e].reshape(b, past_len, Q_DIFF),
                                 past[1][e].reshape(b, past_len, D_DIFF), kb, vb, *lam_args, lam_init,
                                 min(512, past_len))
            outs['dk'] = k.reshape(1, b, t, 2 * H_DIFF, DH_QK)
            outs['dv'] = v.reshape(1, b, t, H_DIFF, DV_DIFF)
            outs['conv'] = nc[None]
            acts = [o, gc]
            w_out = p['w_out_even_b'][e]
        else:
            o_idx = i // 2
            segs = ((0, D_SB, False, True, DH_SB ** -0.5), (D_SB, D_SB, True, True, 1.0),
                    (2 * D_SB, D_SB, True, True, 1.0))
            q, k, kb, v, vb = _norm_proj(x, p['norm_mix'][i].reshape(1, 1, d),
                                         p['w_in_odd_b'][o_idx][None], segs, tm)
            if past is None:
                o = _sb_prompt(q, kb, vb, tq)
            else:
                o = _sb_decode(q, past[3][o_idx].reshape(b, past_len, D_SB),
                               past[4][o_idx].reshape(b, past_len, D_SB), kb, vb,
                               min(512, past_len), 4)
            outs['sk'] = k.reshape(1, b, t, H_SB, DH_SB)
            outs['sv'] = v.reshape(1, b, t, H_SB, DH_SB)
            acts = [o]
            w_out = p['w_out_odd_b'][o_idx]
        x = _post_mix(x, acts, w_out, _row(p['norm_cross'][i]), p['w_q_mem_b'][i], mem_kb[i], mem_vb[i],
                      p['w_o_mem_b'][i], tm)
        m = b * t
        x = _ffn(x.reshape(m, d), _row(p['norm_ffn'][i]), p['w_ffn_up_b'][i], p['w_ffn_down_b'][i],
                 _row(p['norm_final']), i == DEPTH - 1, min(512, m), 1024).reshape(b, t, d)
    return x, outs


def kernel(x_prompt, x_sample, cache_diff_k, cache_diff_v, state_conv, cache_sb_k, cache_sb_v, cache_mem_k, cache_mem_v, mem_prompt, w_in_even, w_out_even, lambda_q1, lambda_k1, lambda_q2, lambda_k2, subln_gain, conv_w, w_in_odd, w_out_odd, norm_mix, norm_mem, norm_cross, w_q_mem, w_k_mem, w_v_mem, w_o_mem, norm_ffn, w_ffn_up, w_ffn_down, norm_final):
    p = dict(norm_mix=norm_mix, norm_cross=norm_cross, norm_ffn=norm_ffn, norm_final=norm_final,
             lambda_q1=lambda_q1, lambda_k1=lambda_k1, lambda_q2=lambda_q2, lambda_k2=lambda_k2,
             subln_gain=subln_gain, conv_w=conv_w)
    for name, w in (('w_in_even', w_in_even), ('w_out_even', w_out_even), ('w_in_odd', w_in_odd),
                    ('w_out_odd', w_out_odd), ('w_q_mem', w_q_mem), ('w_o_mem', w_o_mem),
                    ('w_ffn_up', w_ffn_up), ('w_ffn_down', w_ffn_down)):
        p[name + '_b'] = w.astype(BF16)

    bp = mem_prompt.shape[0]
    w_kv = jnp.concatenate([w_k_mem, w_v_mem], axis=-1).astype(BF16)
    segs = ((0, D_MODEL, True, True, 1.0), (D_MODEL, D_MODEL, True, True, 1.0))
    mk, mkb, mv, mvb = _norm_proj(mem_prompt, norm_mem.reshape(DEPTH, 1, D_MODEL), w_kv, segs, N_MEM)
    p_mem_k = mk.reshape(DEPTH, bp, N_MEM, H_MEM, DH_MEM)
    p_mem_v = mv.reshape(DEPTH, bp, N_MEM, H_MEM, DH_MEM)
    mkb = mkb.reshape(DEPTH, bp, N_MEM, D_MODEL)
    mvb = mvb.reshape(DEPTH, bp, N_MEM, D_MODEL)

    t_p = x_prompt.shape[1]
    y_prompt, po = _run(x_prompt, mkb, mvb, None, p, min(512, t_p), min(256, t_p))

    bs = x_sample.shape[0]
    t_s = x_sample.shape[1]
    cmk = cache_mem_k.reshape(DEPTH, bs, N_MEM, D_MODEL).astype(BF16)
    cmv = cache_mem_v.reshape(DEPTH, bs, N_MEM, D_MODEL).astype(BF16)
    y_sample, so = _run(x_sample, cmk, cmv,
                        (cache_diff_k, cache_diff_v, state_conv, cache_sb_k, cache_sb_v), p, t_s, t_s)

    return (y_prompt, y_sample, po['dk'], po['dv'], po['conv'], po['sk'], po['sv'], p_mem_k, p_mem_v,
            so['dk'], so['dv'], so['conv'], so['sk'], so['sv'])
```

```python
import functools
import math

import jax
import jax.numpy as jnp
from jax import lax
from jax.experimental import pallas as pl
from jax.experimental.pallas import tpu as pltpu

F32 = jnp.float32
BF16 = jnp.bfloat16

D_MODEL = 1024
DEPTH = 2
CHUNK = 64
H_DIFF = 4
DH_QK = 64
DV_DIFF = 2 * DH_QK
D_DIFF = H_DIFF * DV_DIFF
Q_DIFF = 2 * H_DIFF * DH_QK
D_CONV = D_MODEL - D_DIFF
CONV_W = 3
EVEN_IN = 2 * Q_DIFF + D_DIFF + 3 * D_CONV
H_SB = 16
DH_SB = 64
D_SB = H_SB * DH_SB
N_MEM = 256
H_MEM = 4
DH_MEM = D_MODEL // H_MEM
D_FF = 4 * D_MODEL
ROPE_THETA = 10000.0
EPS = 1e-6
SUBLN_EPS = 1e-5
NEG_INF = -1e30
SB_DEAD = 105.0
SB_BOUND_SLACK = 1.01

LANE = 128
SUBLANE = 8
VMEM_LIMIT = 52 * 1024 * 1024


def _params(*sem):
    return pltpu.CompilerParams(dimension_semantics=sem, vmem_limit_bytes=VMEM_LIMIT)


def _rms(x, g, eps):
    ms = jnp.mean(x * x, axis=-1, keepdims=True)
    return x * lax.rsqrt(ms + eps) * g


def _dot(a, b):
    return jnp.dot(a, b, preferred_element_type=F32)


def _dot_nt(a, b):
    return lax.dot_general(a, b, (((1,), (1,)), ((), ())), preferred_element_type=F32)


def _full(shape):
    n = len(shape)
    return pl.BlockSpec(shape, lambda *_: (0,) * n)


def _chunk_of(pos):
    shift = CHUNK.bit_length() - 1
    assert 1 << shift == CHUNK
    return lax.shift_right_logical(pos, shift)


def _row_query_index(groups, t, width):
    r = lax.broadcasted_iota(jnp.int32, (t, width), 0)
    return jnp.concatenate([r] * groups, axis=0)


def _lane_band(lane, s, width):
    return (lane >= s * width) & (lane < (s + 1) * width)


def _even_in_kernel(x_ref, g_ref, w_ref, cos_ref, sin_ref, cw_ref, st_ref,
                    q_ref, k_ref, kb_ref, v_ref, vb_ref, gc_ref, nc_ref, ext_ref, *, tm, nt):
    t = pl.program_id(1)
    h = _rms(x_ref[0], g_ref[...], EPS).astype(BF16)

    def proj(lo, width):
        return _dot(h, w_ref[:, lo:lo + width])

    cos = cos_ref[...]
    sin = sin_ref[...]
    lane = lax.broadcasted_iota(jnp.int32, (tm, LANE), 1)
    first_half = (lane & (DH_QK // 2)) == 0

    def rope_group(yj):
        sw = jnp.where(first_half, pltpu.roll(yj, LANE - DH_QK // 2, 1), pltpu.roll(yj, DH_QK // 2, 1))
        return yj * cos + sw * sin

    yq = proj(0, Q_DIFF)
    yk = proj(Q_DIFF, Q_DIFF)
    for j in range(Q_DIFF // LANE):
        sl = slice(j * LANE, (j + 1) * LANE)
        q_ref[0, :, sl] = (rope_group(yq[:, sl]) * (DH_QK ** -0.5)).astype(BF16)
        kj = rope_group(yk[:, sl])
        k_ref[0, :, sl] = kj
        kb_ref[0, :, sl] = kj.astype(BF16)
    yv = proj(2 * Q_DIFF, D_DIFF)
    v_ref[0] = yv
    vb_ref[0] = yv.astype(BF16)

    base = 2 * Q_DIFF + D_DIFF
    gate_b = proj(base, D_CONV)
    cu = proj(base + D_CONV, D_CONV) * proj(base + 2 * D_CONV, D_CONV)

    @pl.when(t == 0)
    def _():
        ext_ref[SUBLANE - 2:SUBLANE, :] = st_ref[0]

    ext_ref[SUBLANE:SUBLANE + tm, :] = cu
    cw = cw_ref[...]
    conv = (ext_ref[SUBLANE - 2:SUBLANE - 2 + tm, :] * cw[0:1, :]
            + ext_ref[SUBLANE - 1:SUBLANE - 1 + tm, :] * cw[1:2, :]
            + cu * cw[2:3, :])
    gc_ref[0] = (gate_b * conv).astype(BF16)
    ext_ref[0:SUBLANE, :] = ext_ref[tm:tm + SUBLANE, :]

    @pl.when(t == nt - 1)
    def _():
        nc_ref[0] = ext_ref[tm + SUBLANE - 2:tm + SUBLANE, :]


def _even_in(x, g, wb, cos, sin, cw, state, tm):
    b, t, _ = x.shape
    nt = t // tm
    tok = lambda w: pl.BlockSpec((1, tm, w), lambda i, j: (i, j, 0))
    outs = [
        jax.ShapeDtypeStruct((b, t, Q_DIFF), BF16),
        jax.ShapeDtypeStruct((b, t, Q_DIFF), F32),
        jax.ShapeDtypeStruct((b, t, Q_DIFF), BF16),
        jax.ShapeDtypeStruct((b, t, D_DIFF), F32),
        jax.ShapeDtypeStruct((b, t, D_DIFF), BF16),
        jax.ShapeDtypeStruct((b, t, D_CONV), BF16),
        jax.ShapeDtypeStruct((b, CONV_W - 1, D_CONV), F32),
    ]
    return pl.pallas_call(
        functools.partial(_even_in_kernel, tm=tm, nt=nt),
        grid=(b, nt),
        in_specs=[
            tok(D_MODEL),
            _full((1, D_MODEL)),
            _full((D_MODEL, EVEN_IN)),
            pl.BlockSpec((tm, LANE), lambda i, j: (j, 0)),
            pl.BlockSpec((tm, LANE), lambda i, j: (j, 0)),
            _full((CONV_W, D_CONV)),
            pl.BlockSpec((1, CONV_W - 1, D_CONV), lambda i, j: (i, 0, 0)),
        ],
        out_specs=[tok(Q_DIFF), tok(Q_DIFF), tok(Q_DIFF), tok(D_DIFF), tok(D_DIFF), tok(D_CONV),
                   pl.BlockSpec((1, CONV_W - 1, D_CONV), lambda i, j: (i, 0, 0))],
        out_shape=outs,
        scratch_shapes=[pltpu.VMEM((tm + 2 * SUBLANE, D_CONV), F32)],
        compiler_params=_params("parallel", "arbitrary"),
        name="even_in",
    )(x, g, wb, cos, sin, cw, state)


def _norm_proj_kernel(x_ref, g_ref, w_ref, *out_refs, segs):
    h = _rms(x_ref[0], g_ref[0], EPS).astype(BF16)
    i = 0
    for lo, width, want_f32, want_bf16, scale in segs:
        y = _dot(h, w_ref[0, :, lo:lo + width])
        if scale != 1.0:
            y = y * scale
        if want_f32:
            out_refs[i][0] = y
            i += 1
        if want_bf16:
            out_refs[i][0] = y.astype(BF16)
            i += 1


def _norm_proj(x, g, wb, segs, tm):
    b, t, d = x.shape
    n = wb.shape[-1]
    ng = wb.shape[0]
    nt = t // tm
    outs, specs = [], []
    for lo, width, want_f32, want_bf16, _ in segs:
        for want, dt in ((want_f32, F32), (want_bf16, BF16)):
            if want:
                outs.append(jax.ShapeDtypeStruct((ng * b, t, width), dt))
                specs.append(pl.BlockSpec((1, tm, width), lambda i, j: (i, j, 0)))
    return pl.pallas_call(
        functools.partial(_norm_proj_kernel, segs=segs),
        grid=(ng * b, nt),
        in_specs=[
            pl.BlockSpec((1, tm, d), lambda i, j: (i % b, j, 0)),
            pl.BlockSpec((1, 1, d), lambda i, j: (i // b, 0, 0)),
            pl.BlockSpec((1, d, n), lambda i, j: (i // b, 0, 0)),
        ],
        out_specs=specs,
        out_shape=outs,
        compiler_params=_params("parallel", "parallel"),
        name="norm_proj",
    )(x, g, wb)


def _lambda_full(lq1_ref, lk1_ref, lq2_ref, lk2_ref, lam_init):
    s1 = jnp.sum(lq1_ref[...] * lk1_ref[...], axis=1, keepdims=True)
    s2 = jnp.sum(lq2_ref[...] * lk2_ref[...], axis=1, keepdims=True)
    return jnp.exp(s1) - jnp.exp(s2) + lam_init


def _subln(o, gsub_ref, lam_init):
    return _rms(o, gsub_ref[...], SUBLN_EPS) * (1.0 - lam_init)


def _diff_prompt_kernel(q_ref, k_ref, v_ref, lq1_ref, lk1_ref, lq2_ref, lk2_ref, gsub_ref,
                        o_ref, *, tq, tk, lam_init):
    qi = pl.program_id(2)
    ratio = tk // tq
    q = q_ref[0]
    lane = lax.broadcasted_iota(jnp.int32, (tq, LANE), 1)
    zero = jnp.zeros_like(q)
    qm = (jnp.where(lane < DH_QK, q, zero), jnp.where(lane >= DH_QK, q, zero))
    ones = jnp.ones((tk, LANE), BF16)

    def step(j, carry, masked):
        start = pl.multiple_of(j * tk, tk)
        kb = k_ref[0, pl.ds(start, tk), :]
        vb = jnp.concatenate([v_ref[0, pl.ds(start, tk), :], ones], axis=1)
        if masked:
            r = _chunk_of(lax.broadcasted_iota(jnp.int32, (tq, tk), 0)) + lax.rem(qi, ratio) * (tq // CHUNK)
            c = _chunk_of(lax.broadcasted_iota(jnp.int32, (tq, tk), 1))
            vis = c <= r
        ss = [_dot_nt(qm[mi], kb) for mi in range(2)]
        if masked:
            ss = [jnp.where(vis, s, NEG_INF) for s in ss]
        ms = [jnp.maximum(carry[mi][0], jnp.max(ss[mi], axis=1, keepdims=True)) for mi in range(2)]
        ps = [jnp.exp(ss[mi] - ms[mi]).astype(BF16) for mi in range(2)]
        pv = [_dot(ps[mi], vb) for mi in range(2)]
        return tuple((ms[mi], jnp.exp(carry[mi][0] - ms[mi]) * carry[mi][1] + pv[mi]) for mi in range(2))

    init = tuple((jnp.full((tq, 1), NEG_INF, F32), jnp.zeros((tq, 2 * LANE), F32)) for _ in range(2))
    n_full = lax.div(qi, ratio)
    carry = lax.fori_loop(0, n_full, lambda j, c: step(j, c, False), init)
    carry = step(n_full, carry, True)
    lam = _lambda_full(lq1_ref, lk1_ref, lq2_ref, lk2_ref, lam_init)
    (_, a1), (_, a2) = carry
    o = a1[:, :LANE] / a1[:, LANE:] - lam * (a2[:, :LANE] / a2[:, LANE:])
    o_ref[0] = _subln(o, gsub_ref, lam_init).astype(BF16)


def _diff_prompt(q, kb, vb, lq1, lk1, lq2, lk2, gsub, lam_init, tq, tk):
    b, t, _ = q.shape
    lam_spec = _full((1, DH_QK))
    return pl.pallas_call(
        functools.partial(_diff_prompt_kernel, tq=tq, tk=tk, lam_init=lam_init),
        grid=(b, H_DIFF, t // tq),
        in_specs=[
            pl.BlockSpec((1, tq, LANE), lambda i, h, j: (i, j, h)),
            pl.BlockSpec((1, t, LANE), lambda i, h, j: (i, 0, h)),
            pl.BlockSpec((1, t, LANE), lambda i, h, j: (i, 0, h)),
            lam_spec, lam_spec, lam_spec, lam_spec,
            _full((1, DV_DIFF)),
        ],
        out_specs=pl.BlockSpec((1, tq, LANE), lambda i, h, j: (i, j, h)),
        out_shape=jax.ShapeDtypeStruct((b, t, D_DIFF), BF16),
        compiler_params=_params("parallel", "parallel", "arbitrary"),
        name="diff_prompt",
    )(q, kb, vb, lq1, lk1, lq2, lk2, gsub)


def _diff_decode_kernel(q_ref, kp_ref, vp_ref, kn_ref, vn_ref, lq1_ref, lk1_ref, lq2_ref, lk2_ref,
                        gsub_ref, o_ref, qs_ref, m_ref, l_ref, acc_ref, *, t, tk, nk, past_len,
                        lam_init):
    j = pl.program_id(1)
    nsub = 2 * H_DIFF
    rows = nsub * t

    @pl.when(j == 0)
    def _():
        q = q_ref[0]
        lane = lax.broadcasted_iota(jnp.int32, (t, Q_DIFF), 1)
        for s in range(nsub):
            qs_ref[s * t:(s + 1) * t, :] = jnp.where(_lane_band(lane, s, DH_QK), q, jnp.zeros_like(q))
        m_ref[...] = jnp.full(m_ref.shape, NEG_INF, F32)
        l_ref[...] = jnp.zeros(l_ref.shape, F32)
        acc_ref[...] = jnp.zeros(acc_ref.shape, F32)

    def update(kb, vb, k_start, width):
        s = _dot_nt(qs_ref[...], kb)
        q_pos = past_len + _row_query_index(nsub, t, width)
        k_pos = k_start + lax.broadcasted_iota(jnp.int32, (rows, width), 1)
        s = jnp.where(_chunk_of(k_pos) <= _chunk_of(q_pos), s, NEG_INF)
        m = m_ref[...]
        m_new = jnp.maximum(m, jnp.max(s, axis=1, keepdims=True))
        alpha = jnp.exp(m - m_new)
        p = jnp.exp(s - m_new)
        l_ref[...] = alpha * l_ref[...] + jnp.sum(p, axis=1, keepdims=True)
        m_ref[...] = m_new
        pb = p.astype(BF16)
        for h in range(H_DIFF):
            rs = slice(2 * h * t, (2 * h + 2) * t)
            acc_ref[rs, :] = alpha[rs] * acc_ref[rs, :] + _dot(pb[rs], vb[:, h * LANE:(h + 1) * LANE])

    update(kp_ref[0].astype(BF16), vp_ref[0].astype(BF16), j * tk, tk)

    @pl.when(j == nk - 1)
    def _():
        update(kn_ref[0], vn_ref[0], past_len, t)
        lam = _lambda_full(lq1_ref, lk1_ref, lq2_ref, lk2_ref, lam_init)
        on = acc_ref[...] / l_ref[...]
        for h in range(H_DIFF):
            o = on[2 * h * t:(2 * h + 1) * t] - lam * on[(2 * h + 1) * t:(2 * h + 2) * t]
            o_ref[0, :, h * LANE:(h + 1) * LANE] = _subln(o, gsub_ref, lam_init).astype(BF16)


def _diff_decode(q, k_past, v_past, kn, vn, lq1, lk1, lq2, lk2, gsub, lam_init, tk):
    b, t, _ = q.shape
    past_len = k_past.shape[1]
    nk = past_len // tk
    rows = 2 * H_DIFF * t
    lam_spec = _full((1, DH_QK))
    tokb = lambda w: pl.BlockSpec((1, t, w), lambda i, j: (i, 0, 0))
    return pl.pallas_call(
        functools.partial(_diff_decode_kernel, t=t, tk=tk, nk=nk, past_len=past_len,
                          lam_init=lam_init),
        grid=(b, nk),
        in_specs=[
            tokb(Q_DIFF),
            pl.BlockSpec((1, tk, Q_DIFF), lambda i, j: (i, j, 0)),
            pl.BlockSpec((1, tk, D_DIFF), lambda i, j: (i, j, 0)),
            tokb(Q_DIFF), tokb(D_DIFF),
            lam_spec, lam_spec, lam_spec, lam_spec,
            _full((1, DV_DIFF)),
        ],
        out_specs=tokb(D_DIFF),
        out_shape=jax.ShapeDtypeStruct((b, t, D_DIFF), BF16),
        scratch_shapes=[pltpu.VMEM((rows, Q_DIFF), BF16), pltpu.VMEM((rows, 1), F32),
                        pltpu.VMEM((rows, 1), F32), pltpu.VMEM((rows, DV_DIFF), F32)],
        compiler_params=_params("parallel", "arbitrary"),
        name="diff_decode",
    )(q, k_past, v_past, kn, vn, lq1, lk1, lq2, lk2, gsub)


def _softplus(z):
    return jnp.maximum(z, 0.0) + jnp.log(1.0 + jnp.exp(-jnp.abs(z)))


def _rev_cumsum(sp, tri):
    hi = sp.astype(BF16)
    lo = (sp - hi.astype(F32)).astype(BF16)
    return _dot(hi, tri) + _dot(lo, tri)


def _tri(n):
    r = lax.broadcasted_iota(jnp.int32, (n, n), 0)
    c = lax.broadcasted_iota(jnp.int32, (n, n), 1)
    return jnp.where(r >= c, 1.0, 0.0).astype(BF16)


def _head_ones(width):
    r = lax.broadcasted_iota(jnp.int32, (LANE, LANE), 0)
    c = lax.broadcasted_iota(jnp.int32, (LANE, LANE), 1)
    return jnp.where((r < width) == (c < width), 1.0, 0.0).astype(BF16)


def _head_sq_norms(x_bf16, ones_blk):
    xf = x_bf16.astype(F32)
    return _dot((xf * xf).astype(BF16), ones_blk)


def _sb_prompt_kernel(q_ref, k_ref, v_ref, o_ref, kmax_ref, *, tq, t):
    qi = pl.program_id(2)
    q = q_ref[0]
    lane = lax.broadcasted_iota(jnp.int32, (tq, LANE), 1)
    zero = jnp.zeros_like(q)
    qm = (jnp.where(lane < DH_SB, q, zero), jnp.where(lane >= DH_SB, q, zero))
    tri = _tri(tq)
    ones_blk = _head_ones(DH_SB)

    @pl.when(qi == 0)
    def _():
        def body(i, mx):
            kc = k_ref[0, pl.ds(pl.multiple_of(i * tq, tq), tq), :]
            return jnp.maximum(mx, _head_sq_norms(kc, ones_blk))
        mx = lax.fori_loop(0, t // tq, body, jnp.zeros((tq, LANE), F32))
        kmax_ref[...] = jnp.max(mx, axis=0, keepdims=True)

    bound = jnp.sqrt(_head_sq_norms(q, ones_blk) * kmax_ref[...]) * SB_BOUND_SLACK

    qs = jnp.concatenate(qm, axis=0)
    bz = jnp.concatenate([bound[:, 0:1], bound[:, DH_SB:DH_SB + 1]], axis=0)

    def alive(ccar):
        return (jnp.max(bz - ccar) > -SB_DEAD).astype(jnp.int32)

    def blk(ref, j):
        return ref[0, pl.ds(pl.multiple_of(j * tq, tq), tq), :]

    has_prev = qi > 0
    jp = jnp.maximum(qi - 1, 0)
    z_r = _dot_nt(qs, blk(k_ref, qi))
    z_l = _dot_nt(qs, blk(k_ref, jp))
    vis = lax.broadcasted_iota(jnp.int32, (2 * tq, tq), 1) < _row_query_index(2, tq, tq)
    sp_r = jnp.where(vis, _softplus(z_r), 0.0)
    sp_l = jnp.where(has_prev, _softplus(z_l), 0.0)
    cl_r = _rev_cumsum(sp_r, tri)
    cl_l = _rev_cumsum(sp_l, tri) + cl_r[:, 0:1]
    w_r = jnp.where(vis, jnp.exp(z_r - cl_r), 0.0).astype(BF16)
    w_l = jnp.where(has_prev, jnp.exp(z_l - cl_l), 0.0).astype(BF16)
    acc = _dot(jnp.concatenate([w_l, w_r], axis=1),
               jnp.concatenate([blk(v_ref, jp), blk(v_ref, qi)], axis=0))
    ccar = cl_l[:, 0:1]

    def body(c):
        j, _, ccar, acc = c
        z = _dot_nt(qs, blk(k_ref, j))
        cl = _rev_cumsum(_softplus(z), tri)
        w = jnp.exp(z - ccar - cl).astype(BF16)
        ccar = ccar + cl[:, 0:1]
        return j - 1, alive(ccar), ccar, acc + _dot(w, blk(v_ref, j))

    out = lax.while_loop(lambda c: jnp.logical_and(c[0] >= 0, c[1] > 0), body,
                         (qi - 2, alive(ccar), ccar, acc))
    acc = out[3]
    o_ref[0] = jnp.where(lane < DH_SB, acc[:tq], acc[tq:]).astype(BF16)


def _sb_prompt(q, kb, vb, tq):
    b, t, _ = q.shape
    return pl.pallas_call(
        functools.partial(_sb_prompt_kernel, tq=tq, t=t),
        scratch_shapes=[pltpu.VMEM((1, LANE), F32)],
        grid=(b, D_SB // LANE, t // tq),
        in_specs=[
            pl.BlockSpec((1, tq, LANE), lambda i, h, j: (i, j, h)),
            pl.BlockSpec((1, t, LANE), lambda i, h, j: (i, 0, h)),
            pl.BlockSpec((1, t, LANE), lambda i, h, j: (i, 0, h)),
        ],
        out_specs=pl.BlockSpec((1, tq, LANE), lambda i, h, j: (i, j, h)),
        out_shape=jax.ShapeDtypeStruct((b, t, D_SB), BF16),
        compiler_params=_params("parallel", "parallel", "arbitrary"),
        name="sb_prompt",
    )(q, kb, vb)


def _sb_kmax_kernel(k_ref, o_ref, *, tk):
    j = pl.program_id(1)
    x = k_ref[0]
    ones = jnp.ones((DH_SB, LANE), BF16)
    n2 = _dot((x * x).astype(BF16), ones).reshape(tk, H_SB, LANE)
    blk = jnp.max(n2, axis=0)

    @pl.when(j == 0)
    def _():
        o_ref[0] = blk

    @pl.when(j > 0)
    def _():
        o_ref[0] = jnp.maximum(o_ref[0], blk)


def _sb_kmax(k_rows, tk):
    b, rows, _ = k_rows.shape
    return pl.pallas_call(
        functools.partial(_sb_kmax_kernel, tk=tk),
        grid=(b, rows // (tk * H_SB)),
        in_specs=[pl.BlockSpec((1, tk * H_SB, DH_SB), lambda i, j: (i, j, 0))],
        out_specs=pl.BlockSpec((1, H_SB, LANE), lambda i, j: (i, 0, 0)),
        out_shape=jax.ShapeDtypeStruct((b, H_SB, LANE), F32),
        compiler_params=_params("parallel", "arbitrary"),
        name="sb_kmax",
    )(k_rows)


def _sb_decode_kernel(q_ref, kn_ref, vn_ref, kmax_ref, kp_hbm, vp_hbm, o_ref, kbuf, vbuf, sem,
                      *, t, tk, nk):
    b = pl.program_id(0)
    rows = H_SB * t
    q = q_ref[0]
    qh = [q[:, h * DH_SB:(h + 1) * DH_SB] for h in range(H_SB)]

    def attend(kh, vh, width, ccar, acc, vis):
        z = jnp.concatenate([_dot_nt(qh[h], kh[h]) for h in range(H_SB)], axis=0)
        sp = _softplus(z)
        if vis is not None:
            sp = jnp.where(vis, sp, 0.0)
        cl = _rev_cumsum(sp, _tri(width))
        w = jnp.exp(z - ccar - cl)
        if vis is not None:
            w = jnp.where(vis, w, 0.0)
        w = w.astype(BF16)
        pv = jnp.concatenate([_dot(w[h * t:(h + 1) * t], vh[h]) for h in range(H_SB)], axis=0)
        return ccar + cl[:, 0:1], acc + pv

    kn = kn_ref[0]
    vn = vn_ref[0]
    vis = lax.broadcasted_iota(jnp.int32, (rows, t), 1) < _row_query_index(H_SB, t, t)
    ccar, acc = attend([kn[:, h * DH_SB:(h + 1) * DH_SB] for h in range(H_SB)],
                       [vn[:, h * DH_SB:(h + 1) * DH_SB] for h in range(H_SB)],
                       t, jnp.zeros((rows, 1), F32), jnp.zeros((rows, DH_SB), F32), vis)

    kmax2 = kmax_ref[0]
    bz = jnp.concatenate(
        [jnp.sqrt(jnp.sum(jnp.square(qh[h].astype(F32)), axis=1, keepdims=True) * kmax2[h:h + 1, 0:1])
         for h in range(H_SB)], axis=0) * SB_BOUND_SLACK

    def alive(ccar):
        return (jnp.max(bz - ccar) > -SB_DEAD).astype(jnp.int32)

    def copies(j):
        start = pl.multiple_of(j * (tk * H_SB), tk * H_SB)
        return (pltpu.make_async_copy(kp_hbm.at[b, pl.ds(start, tk * H_SB), :], kbuf, sem.at[0]),
                pltpu.make_async_copy(vp_hbm.at[b, pl.ds(start, tk * H_SB), :], vbuf, sem.at[1]))

    def body(c):
        j, _, ccar, acc = c
        ck, cv = copies(j)
        ck.start()
        cv.start()
        ck.wait()
        cv.wait()
        kh = [kbuf[pl.ds(h, tk, stride=H_SB), :].astype(BF16) for h in range(H_SB)]
        vh = [vbuf[pl.ds(h, tk, stride=H_SB), :].astype(BF16) for h in range(H_SB)]
        ccar, acc = attend(kh, vh, tk, ccar, acc, None)
        return j - 1, alive(ccar), ccar, acc

    out = lax.while_loop(lambda c: jnp.logical_and(c[0] >= 0, c[1] > 0), body,
                         (jnp.int32(nk - 1), alive(ccar), ccar, acc))
    acc = out[3]
    for h in range(H_SB):
        o_ref[0, :, h * DH_SB:(h + 1) * DH_SB] = acc[h * t:(h + 1) * t, :].astype(BF16)


def _sb_decode(q, k_rows, v_rows, kn, vn, tk):
    b, t, _ = q.shape
    past_len = k_rows.shape[1] // H_SB
    kmax2 = _sb_kmax(k_rows, min(512, past_len))
    tokb = lambda: pl.BlockSpec((1, t, D_SB), lambda i: (i, 0, 0))
    return pl.pallas_call(
        functools.partial(_sb_decode_kernel, t=t, tk=tk, nk=past_len // tk),
        grid=(b,),
        in_specs=[tokb(), tokb(), tokb(), pl.BlockSpec((1, H_SB, LANE), lambda i: (i, 0, 0)),
                  pl.BlockSpec(memory_space=pl.ANY), pl.BlockSpec(memory_space=pl.ANY)],
        out_specs=tokb(),
        out_shape=jax.ShapeDtypeStruct((b, t, D_SB), BF16),
        scratch_shapes=[pltpu.VMEM((tk * H_SB, DH_SB), F32), pltpu.VMEM((tk * H_SB, DH_SB), F32),
                        pltpu.SemaphoreType.DMA((2,))],
        compiler_params=_params("arbitrary"),
        name="sb_decode",
    )(q, kn, vn, kmax2, k_rows, v_rows)


def _post_mix_kernel(*refs, n_in):
    x_ref = refs[0]
    a_refs = refs[1:1 + n_in]
    w_ref, g_ref, wq_ref, mk_ref, mv_ref, wo_ref, o_ref = refs[1 + n_in:]
    a = a_refs[0][0] if n_in == 1 else jnp.concatenate([r[0] for r in a_refs], axis=1)
    x = x_ref[0] + _dot(a, w_ref[...])
    hq = _rms(x, g_ref[...], EPS).astype(BF16)
    q = (_dot(hq, wq_ref[...]) * (DH_MEM ** -0.5)).astype(BF16)
    y = x
    for h in range(H_MEM):
        hs = slice(h * DH_MEM, (h + 1) * DH_MEM)
        s = _dot_nt(q[:, hs], mk_ref[0, :, hs])
        p = jnp.exp(s - jnp.max(s, axis=1, keepdims=True))
        l = jnp.sum(p, axis=1, keepdims=True)
        oh = _dot(p.astype(BF16), mv_ref[0, :, hs]) / l
        y = y + _dot(oh.astype(BF16), wo_ref[hs, :])
    o_ref[0] = y


def _post_mix(x, acts, w, g, wq, mk, mv, wo, tm):
    b, t, d = x.shape
    n_in = len(acts)
    tok = lambda w: pl.BlockSpec((1, tm, w), lambda i, j: (i, j, 0))
    mem = pl.BlockSpec((1, N_MEM, D_MODEL), lambda i, j: (i, 0, 0))
    return pl.pallas_call(
        functools.partial(_post_mix_kernel, n_in=n_in),
        grid=(b, t // tm),
        in_specs=([tok(d)] + [tok(a.shape[-1]) for a in acts]
                  + [_full(w.shape), _full((1, d)), _full((d, d)), mem, mem, _full((d, d))]),
        out_specs=tok(d),
        out_shape=jax.ShapeDtypeStruct((b, t, d), F32),
        compiler_params=_params("parallel", "parallel"),
        name="post_mix",
    )(x, *acts, w, g, wq, mk, mv, wo)


def _ffn_kernel(x_ref, g_ref, wu_ref, wd_ref, gf_ref, o_ref, *, fc, final):
    x = x_ref[...]
    h = _rms(x, g_ref[...], EPS).astype(BF16)
    y = x
    for c in range(D_FF // fc):
        u = jnp.maximum(_dot(h, wu_ref[:, c * fc:(c + 1) * fc]), 0.0)
        y = y + _dot((u * u).astype(BF16), wd_ref[c * fc:(c + 1) * fc, :])
    if final:
        y = _rms(y, gf_ref[...], EPS)
    o_ref[...] = y


def _ffn(x2d, g, wu, wd, gf, final, tm, fc):
    m, d = x2d.shape
    return pl.pallas_call(
        functools.partial(_ffn_kernel, fc=fc, final=final),
        grid=(m // tm,),
        in_specs=[pl.BlockSpec((tm, d), lambda i: (i, 0)), _full((1, d)), _full((d, D_FF)),
                  _full((D_FF, d)), _full((1, d))],
        out_specs=pl.BlockSpec((tm, d), lambda i: (i, 0)),
        out_shape=jax.ShapeDtypeStruct((m, d), F32),
        compiler_params=_params("parallel"),
        name="ffn",
    )(x2d, g, wu, wd, gf)


def _rope_tables(pos):
    half = DH_QK // 2
    inv = jnp.power(ROPE_THETA, -jnp.arange(half, dtype=F32) * (2.0 / DH_QK))
    ang = pos.astype(F32)[:, None] * inv[None, :]
    cos, sin = jnp.cos(ang), jnp.sin(ang)
    reps = LANE // DH_QK
    return (jnp.tile(jnp.concatenate([cos, cos], axis=1), (1, reps)),
            jnp.tile(jnp.concatenate([-sin, sin], axis=1), (1, reps)))


def _row(v):
    return v.reshape(1, -1)


def _run(x, mem_kb, mem_vb, past, p, tm, tq):
    b, t, d = x.shape
    past_len = 0 if past is None else past[0].shape[2]
    pos = jnp.arange(past_len, past_len + t)
    cos, sin = _rope_tables(pos)
    outs = {}
    for i in range(DEPTH):
        if i % 2 == 0:
            e = i // 2
            lam_init = 0.8 - 0.6 * math.exp(-0.3 * i)
            state = jnp.zeros((b, CONV_W - 1, D_CONV), F32) if past is None else past[2][e]
            q, k, kb, v, vb, gc, nc = _even_in(x, _row(p['norm_mix'][i]), p['w_in_even_b'][e], cos, sin,
                                               p['conv_w'][e], state, tm)
            lam_args = (_row(p['lambda_q1'][e]), _row(p['lambda_k1'][e]), _row(p['lambda_q2'][e]),
                        _row(p['lambda_k2'][e]), _row(p['subln_gain'][e]))
            if past is None:
                o = _diff_prompt(q, kb, vb, *lam_args, lam_init, min(4 * tq, t), min(1024, t))
            else:
                o = _diff_decode(q, past[0][e].reshape(b, past_len, Q_DIFF),
                                 past[1][e].reshape(b, past_len, D_DIFF), kb, vb, *lam_args, lam_init,
                                 min(512, past_len))
            outs['dk'] = k.reshape(1, b, t, 2 * H_DIFF, DH_QK)
            outs['dv'] = v.reshape(1, b, t, H_DIFF, DV_DIFF)
            outs['conv'] = nc[None]
            acts = [o, gc]
            w_out = p['w_out_even_b'][e]
        else:
            o_idx = i // 2
            segs = ((0, D_SB, False, True, DH_SB ** -0.5), (D_SB, D_SB, True, True, 1.0),
                    (2 * D_SB, D_SB, True, True, 1.0))
            q, k, kb, v, vb = _norm_proj(x, p['norm_mix'][i].reshape(1, 1, d),
                                         p['w_in_odd_b'][o_idx][None], segs, tm)
            if past is None:
                o = _sb_prompt(q, kb, vb, tq)
            else:
                o = _sb_decode(q, past[3][o_idx].reshape(b, past_len * H_SB, DH_SB),
                               past[4][o_idx].reshape(b, past_len * H_SB, DH_SB), kb, vb,
                               min(256, past_len))
            outs['sk'] = k.reshape(1, b, t, H_SB, DH_SB)
            outs['sv'] = v.reshape(1, b, t, H_SB, DH_SB)
            acts = [o]
            w_out = p['w_out_odd_b'][o_idx]
        x = _post_mix(x, acts, w_out, _row(p['norm_cross'][i]), p['w_q_mem_b'][i], mem_kb[i], mem_vb[i],
                      p['w_o_mem_b'][i], tm)
        m = b * t
        x = _ffn(x.reshape(m, d), _row(p['norm_ffn'][i]), p['w_ffn_up_b'][i], p['w_ffn_down_b'][i],
                 _row(p['norm_final']), i == DEPTH - 1, min(512, m), 1024).reshape(b, t, d)
    return x, outs


def kernel(x_prompt, x_sample, cache_diff_k, cache_diff_v, state_conv, cache_sb_k, cache_sb_v, cache_mem_k, cache_mem_v, mem_prompt, w_in_even, w_out_even, lambda_q1, lambda_k1, lambda_q2, lambda_k2, subln_gain, conv_w, w_in_odd, w_out_odd, norm_mix, norm_mem, norm_cross, w_q_mem, w_k_mem, w_v_mem, w_o_mem, norm_ffn, w_ffn_up, w_ffn_down, norm_final):
    p = dict(norm_mix=norm_mix, norm_cross=norm_cross, norm_ffn=norm_ffn, norm_final=norm_final,
             lambda_q1=lambda_q1, lambda_k1=lambda_k1, lambda_q2=lambda_q2, lambda_k2=lambda_k2,
             subln_gain=subln_gain, conv_w=conv_w)
    for name, w in (('w_in_even', w_in_even), ('w_out_even', w_out_even), ('w_in_odd', w_in_odd),
                    ('w_out_odd', w_out_odd), ('w_q_mem', w_q_mem), ('w_o_mem', w_o_mem),
                    ('w_ffn_up', w_ffn_up), ('w_ffn_down', w_ffn_down)):
        p[name + '_b'] = w.astype(BF16)

    bp = mem_prompt.shape[0]
    w_kv = jnp.concatenate([w_k_mem, w_v_mem], axis=-1).astype(BF16)
    segs = ((0, D_MODEL, True, True, 1.0), (D_MODEL, D_MODEL, True, True, 1.0))
    mk, mkb, mv, mvb = _norm_proj(mem_prompt, norm_mem.reshape(DEPTH, 1, D_MODEL), w_kv, segs, N_MEM)
    p_mem_k = mk.reshape(DEPTH, bp, N_MEM, H_MEM, DH_MEM)
    p_mem_v = mv.reshape(DEPTH, bp, N_MEM, H_MEM, DH_MEM)
    mkb = mkb.reshape(DEPTH, bp, N_MEM, D_MODEL)
    mvb = mvb.reshape(DEPTH, bp, N_MEM, D_MODEL)

    t_p = x_prompt.shape[1]
    y_prompt, po = _run(x_prompt, mkb, mvb, None, p, min(512, t_p), min(256, t_p))

    bs = x_sample.shape[0]
    t_s = x_sample.shape[1]
    cmk = cache_mem_k.reshape(DEPTH, bs, N_MEM, D_MODEL).astype(BF16)
    cmv = cache_mem_v.reshape(DEPTH, bs, N_MEM, D_MODEL).astype(BF16)
    y_sample, so = _run(x_sample, cmk, cmv,
                        (cache_diff_k, cache_diff_v, state_conv, cache_sb_k, cache_sb_v), p, t_s, t_s)

    return (y_prompt, y_sample, po['dk'], po['dv'], po['conv'], po['sk'], po['sv'], p_mem_k, p_mem_v,
            so['dk'], so['dv'], so['conv'], so['sk'], so['sv'])
```

```python
import functools
import math

import jax
import jax.numpy as jnp
from jax import lax
from jax.experimental import pallas as pl
from jax.experimental.pallas import tpu as pltpu

F32 = jnp.float32
BF16 = jnp.bfloat16

D_MODEL = 1024
DEPTH = 2
CHUNK = 64
H_DIFF = 4
DH_QK = 64
DV_DIFF = 2 * DH_QK
D_DIFF = H_DIFF * DV_DIFF
Q_DIFF = 2 * H_DIFF * DH_QK
D_CONV = D_MODEL - D_DIFF
CONV_W = 3
EVEN_IN = 2 * Q_DIFF + D_DIFF + 3 * D_CONV
H_SB = 16
DH_SB = 64
D_SB = H_SB * DH_SB
N_MEM = 256
H_MEM = 4
DH_MEM = D_MODEL // H_MEM
D_FF = 4 * D_MODEL
ROPE_THETA = 10000.0
EPS = 1e-6
SUBLN_EPS = 1e-5
NEG_INF = -1e30
LOG2E = math.log2(math.e)
SB_DEAD = 105.0 * LOG2E
SB_BOUND_SLACK = 1.01

LANE = 128
SUBLANE = 8
VMEM_LIMIT = 52 * 1024 * 1024


def _params(*sem):
    return pltpu.CompilerParams(dimension_semantics=sem, vmem_limit_bytes=VMEM_LIMIT)


def _rms(x, g, eps):
    ms = jnp.mean(x * x, axis=-1, keepdims=True)
    return x * lax.rsqrt(ms + eps) * g


def _dot(a, b):
    return jnp.dot(a, b, preferred_element_type=F32)


def _dot_nt(a, b):
    return lax.dot_general(a, b, (((1,), (1,)), ((), ())), preferred_element_type=F32)


def _full(shape):
    n = len(shape)
    return pl.BlockSpec(shape, lambda *_: (0,) * n)


def _chunk_of(pos):
    shift = CHUNK.bit_length() - 1
    assert 1 << shift == CHUNK
    return lax.shift_right_logical(pos, shift)


def _row_query_index(groups, t, width):
    r = lax.broadcasted_iota(jnp.int32, (t, width), 0)
    return jnp.concatenate([r] * groups, axis=0)


def _lane_band(lane, s, width):
    return (lane >= s * width) & (lane < (s + 1) * width)


def _even_in_kernel(x_ref, g_ref, w_ref, cos_ref, sin_ref, cw_ref, st_ref,
                    q_ref, k_ref, kb_ref, v_ref, vb_ref, gc_ref, nc_ref, ext_ref, *, tm, nt):
    t = pl.program_id(1)
    h = _rms(x_ref[0], g_ref[...], EPS).astype(BF16)

    def proj(lo, width):
        return _dot(h, w_ref[:, lo:lo + width])

    cos = cos_ref[...]
    sin = sin_ref[...]
    lane = lax.broadcasted_iota(jnp.int32, (tm, LANE), 1)
    first_half = (lane & (DH_QK // 2)) == 0

    def rope_group(yj):
        sw = jnp.where(first_half, pltpu.roll(yj, LANE - DH_QK // 2, 1), pltpu.roll(yj, DH_QK // 2, 1))
        return yj * cos + sw * sin

    yq = proj(0, Q_DIFF)
    yk = proj(Q_DIFF, Q_DIFF)
    for j in range(Q_DIFF // LANE):
        sl = slice(j * LANE, (j + 1) * LANE)
        q_ref[0, :, sl] = (rope_group(yq[:, sl]) * (DH_QK ** -0.5)).astype(BF16)
        kj = rope_group(yk[:, sl])
        k_ref[0, :, sl] = kj
        kb_ref[0, :, sl] = kj.astype(BF16)
    yv = proj(2 * Q_DIFF, D_DIFF)
    v_ref[0] = yv
    vb_ref[0] = yv.astype(BF16)

    base = 2 * Q_DIFF + D_DIFF
    gate_b = proj(base, D_CONV)
    cu = proj(base + D_CONV, D_CONV) * proj(base + 2 * D_CONV, D_CONV)

    @pl.when(t == 0)
    def _():
        ext_ref[SUBLANE - 2:SUBLANE, :] = st_ref[0]

    ext_ref[SUBLANE:SUBLANE + tm, :] = cu
    cw = cw_ref[...]
    conv = (ext_ref[SUBLANE - 2:SUBLANE - 2 + tm, :] * cw[0:1, :]
            + ext_ref[SUBLANE - 1:SUBLANE - 1 + tm, :] * cw[1:2, :]
            + cu * cw[2:3, :])
    gc_ref[0] = (gate_b * conv).astype(BF16)
    ext_ref[0:SUBLANE, :] = ext_ref[tm:tm + SUBLANE, :]

    @pl.when(t == nt - 1)
    def _():
        nc_ref[0] = ext_ref[tm + SUBLANE - 2:tm + SUBLANE, :]


def _even_in(x, g, wb, cos, sin, cw, state, tm):
    b, t, _ = x.shape
    nt = t // tm
    tok = lambda w: pl.BlockSpec((1, tm, w), lambda i, j: (i, j, 0))
    outs = [
        jax.ShapeDtypeStruct((b, t, Q_DIFF), BF16),
        jax.ShapeDtypeStruct((b, t, Q_DIFF), F32),
        jax.ShapeDtypeStruct((b, t, Q_DIFF), BF16),
        jax.ShapeDtypeStruct((b, t, D_DIFF), F32),
        jax.ShapeDtypeStruct((b, t, D_DIFF), BF16),
        jax.ShapeDtypeStruct((b, t, D_CONV), BF16),
        jax.ShapeDtypeStruct((b, CONV_W - 1, D_CONV), F32),
    ]
    return pl.pallas_call(
        functools.partial(_even_in_kernel, tm=tm, nt=nt),
        grid=(b, nt),
        in_specs=[
            tok(D_MODEL),
            _full((1, D_MODEL)),
            _full((D_MODEL, EVEN_IN)),
            pl.BlockSpec((tm, LANE), lambda i, j: (j, 0)),
            pl.BlockSpec((tm, LANE), lambda i, j: (j, 0)),
            _full((CONV_W, D_CONV)),
            pl.BlockSpec((1, CONV_W - 1, D_CONV), lambda i, j: (i, 0, 0)),
        ],
        out_specs=[tok(Q_DIFF), tok(Q_DIFF), tok(Q_DIFF), tok(D_DIFF), tok(D_DIFF), tok(D_CONV),
                   pl.BlockSpec((1, CONV_W - 1, D_CONV), lambda i, j: (i, 0, 0))],
        out_shape=outs,
        scratch_shapes=[pltpu.VMEM((tm + 2 * SUBLANE, D_CONV), F32)],
        compiler_params=_params("parallel", "arbitrary"),
        name="even_in",
    )(x, g, wb, cos, sin, cw, state)


def _norm_proj_kernel(x_ref, g_ref, w_ref, *out_refs, segs):
    h = _rms(x_ref[0], g_ref[0], EPS).astype(BF16)
    i = 0
    for lo, width, want_f32, want_bf16, scale in segs:
        y = _dot(h, w_ref[0, :, lo:lo + width])
        if scale != 1.0:
            y = y * scale
        if want_f32:
            out_refs[i][0] = y
            i += 1
        if want_bf16:
            out_refs[i][0] = y.astype(BF16)
            i += 1


def _norm_proj(x, g, wb, segs, tm):
    b, t, d = x.shape
    n = wb.shape[-1]
    ng = wb.shape[0]
    nt = t // tm
    outs, specs = [], []
    for lo, width, want_f32, want_bf16, _ in segs:
        for want, dt in ((want_f32, F32), (want_bf16, BF16)):
            if want:
                outs.append(jax.ShapeDtypeStruct((ng * b, t, width), dt))
                specs.append(pl.BlockSpec((1, tm, width), lambda i, j: (i, j, 0)))
    return pl.pallas_call(
        functools.partial(_norm_proj_kernel, segs=segs),
        grid=(ng * b, nt),
        in_specs=[
            pl.BlockSpec((1, tm, d), lambda i, j: (i % b, j, 0)),
            pl.BlockSpec((1, 1, d), lambda i, j: (i // b, 0, 0)),
            pl.BlockSpec((1, d, n), lambda i, j: (i // b, 0, 0)),
        ],
        out_specs=specs,
        out_shape=outs,
        compiler_params=_params("parallel", "parallel"),
        name="norm_proj",
    )(x, g, wb)


def _lambda_full(lq1_ref, lk1_ref, lq2_ref, lk2_ref, lam_init):
    s1 = jnp.sum(lq1_ref[...] * lk1_ref[...], axis=1, keepdims=True)
    s2 = jnp.sum(lq2_ref[...] * lk2_ref[...], axis=1, keepdims=True)
    return jnp.exp(s1) - jnp.exp(s2) + lam_init


def _subln(o, gsub_ref, lam_init):
    return _rms(o, gsub_ref[...], SUBLN_EPS) * (1.0 - lam_init)


def _diff_prompt_kernel(q_ref, k_ref, v_ref, lq1_ref, lk1_ref, lq2_ref, lk2_ref, gsub_ref,
                        o_ref, *, tq, tk, lam_init):
    qi = pl.program_id(2)
    ratio = tk // tq
    q = q_ref[0]
    lane = lax.broadcasted_iota(jnp.int32, (tq, LANE), 1)
    zero = jnp.zeros_like(q)
    qm = (jnp.where(lane < DH_QK, q, zero), jnp.where(lane >= DH_QK, q, zero))
    ones = jnp.ones((tk, LANE), BF16)

    def step(j, carry, masked):
        start = pl.multiple_of(j * tk, tk)
        kb = k_ref[0, pl.ds(start, tk), :]
        vb = jnp.concatenate([v_ref[0, pl.ds(start, tk), :], ones], axis=1)
        if masked:
            r = _chunk_of(lax.broadcasted_iota(jnp.int32, (tq, tk), 0)) + lax.rem(qi, ratio) * (tq // CHUNK)
            c = _chunk_of(lax.broadcasted_iota(jnp.int32, (tq, tk), 1))
            vis = c <= r
        ss = [_dot_nt(qm[mi], kb) for mi in range(2)]
        if masked:
            ss = [jnp.where(vis, s, NEG_INF) for s in ss]
        ms = [jnp.maximum(carry[mi][0], jnp.max(ss[mi], axis=1, keepdims=True)) for mi in range(2)]
        ps = [jnp.exp(ss[mi] - ms[mi]).astype(BF16) for mi in range(2)]
        pv = [_dot(ps[mi], vb) for mi in range(2)]
        return tuple((ms[mi], jnp.exp(carry[mi][0] - ms[mi]) * carry[mi][1] + pv[mi]) for mi in range(2))

    init = tuple((jnp.full((tq, 1), NEG_INF, F32), jnp.zeros((tq, 2 * LANE), F32)) for _ in range(2))
    n_full = lax.div(qi, ratio)
    carry = lax.fori_loop(0, n_full, lambda j, c: step(j, c, False), init)
    carry = step(n_full, carry, True)
    lam = _lambda_full(lq1_ref, lk1_ref, lq2_ref, lk2_ref, lam_init)
    (_, a1), (_, a2) = carry
    o = a1[:, :LANE] / a1[:, LANE:] - lam * (a2[:, :LANE] / a2[:, LANE:])
    o_ref[0] = _subln(o, gsub_ref, lam_init).astype(BF16)


def _diff_prompt(q, kb, vb, lq1, lk1, lq2, lk2, gsub, lam_init, tq, tk):
    b, t, _ = q.shape
    lam_spec = _full((1, DH_QK))
    return pl.pallas_call(
        functools.partial(_diff_prompt_kernel, tq=tq, tk=tk, lam_init=lam_init),
        grid=(b, H_DIFF, t // tq),
        in_specs=[
            pl.BlockSpec((1, tq, LANE), lambda i, h, j: (i, j, h)),
            pl.BlockSpec((1, t, LANE), lambda i, h, j: (i, 0, h)),
            pl.BlockSpec((1, t, LANE), lambda i, h, j: (i, 0, h)),
            lam_spec, lam_spec, lam_spec, lam_spec,
            _full((1, DV_DIFF)),
        ],
        out_specs=pl.BlockSpec((1, tq, LANE), lambda i, h, j: (i, j, h)),
        out_shape=jax.ShapeDtypeStruct((b, t, D_DIFF), BF16),
        compiler_params=_params("parallel", "parallel", "arbitrary"),
        name="diff_prompt",
    )(q, kb, vb, lq1, lk1, lq2, lk2, gsub)


def _diff_decode_kernel(q_ref, kp_ref, vp_ref, kn_ref, vn_ref, lq1_ref, lk1_ref, lq2_ref, lk2_ref,
                        gsub_ref, o_ref, qs_ref, m_ref, l_ref, acc_ref, *, t, tk, nk, past_len,
                        lam_init):
    j = pl.program_id(1)
    nsub = 2 * H_DIFF
    rows = nsub * t

    @pl.when(j == 0)
    def _():
        q = q_ref[0]
        lane = lax.broadcasted_iota(jnp.int32, (t, Q_DIFF), 1)
        for s in range(nsub):
            qs_ref[s * t:(s + 1) * t, :] = jnp.where(_lane_band(lane, s, DH_QK), q, jnp.zeros_like(q))
        m_ref[...] = jnp.full(m_ref.shape, NEG_INF, F32)
        l_ref[...] = jnp.zeros(l_ref.shape, F32)
        acc_ref[...] = jnp.zeros(acc_ref.shape, F32)

    def update(kb, vb, k_start, width):
        s = _dot_nt(qs_ref[...], kb)
        q_pos = past_len + _row_query_index(nsub, t, width)
        k_pos = k_start + lax.broadcasted_iota(jnp.int32, (rows, width), 1)
        s = jnp.where(_chunk_of(k_pos) <= _chunk_of(q_pos), s, NEG_INF)
        m = m_ref[...]
        m_new = jnp.maximum(m, jnp.max(s, axis=1, keepdims=True))
        alpha = jnp.exp(m - m_new)
        p = jnp.exp(s - m_new)
        l_ref[...] = alpha * l_ref[...] + jnp.sum(p, axis=1, keepdims=True)
        m_ref[...] = m_new
        pb = p.astype(BF16)
        for h in range(H_DIFF):
            rs = slice(2 * h * t, (2 * h + 2) * t)
            acc_ref[rs, :] = alpha[rs] * acc_ref[rs, :] + _dot(pb[rs], vb[:, h * LANE:(h + 1) * LANE])

    update(kp_ref[0].astype(BF16), vp_ref[0].astype(BF16), j * tk, tk)

    @pl.when(j == nk - 1)
    def _():
        update(kn_ref[0], vn_ref[0], past_len, t)
        lam = _lambda_full(lq1_ref, lk1_ref, lq2_ref, lk2_ref, lam_init)
        on = acc_ref[...] / l_ref[...]
        for h in range(H_DIFF):
            o = on[2 * h * t:(2 * h + 1) * t] - lam * on[(2 * h + 1) * t:(2 * h + 2) * t]
            o_ref[0, :, h * LANE:(h + 1) * LANE] = _subln(o, gsub_ref, lam_init).astype(BF16)


def _diff_decode(q, k_past, v_past, kn, vn, lq1, lk1, lq2, lk2, gsub, lam_init, tk):
    b, t, _ = q.shape
    past_len = k_past.shape[1]
    nk = past_len // tk
    rows = 2 * H_DIFF * t
    lam_spec = _full((1, DH_QK))
    tokb = lambda w: pl.BlockSpec((1, t, w), lambda i, j: (i, 0, 0))
    return pl.pallas_call(
        functools.partial(_diff_decode_kernel, t=t, tk=tk, nk=nk, past_len=past_len,
                          lam_init=lam_init),
        grid=(b, nk),
        in_specs=[
            tokb(Q_DIFF),
            pl.BlockSpec((1, tk, Q_DIFF), lambda i, j: (i, j, 0)),
            pl.BlockSpec((1, tk, D_DIFF), lambda i, j: (i, j, 0)),
            tokb(Q_DIFF), tokb(D_DIFF),
            lam_spec, lam_spec, lam_spec, lam_spec,
            _full((1, DV_DIFF)),
        ],
        out_specs=tokb(D_DIFF),
        out_shape=jax.ShapeDtypeStruct((b, t, D_DIFF), BF16),
        scratch_shapes=[pltpu.VMEM((rows, Q_DIFF), BF16), pltpu.VMEM((rows, 1), F32),
                        pltpu.VMEM((rows, 1), F32), pltpu.VMEM((rows, DV_DIFF), F32)],
        compiler_params=_params("parallel", "arbitrary"),
        name="diff_decode",
    )(q, k_past, v_past, kn, vn, lq1, lk1, lq2, lk2, gsub)


def _softplus(z):
    return jnp.maximum(z, 0.0) + jnp.log(1.0 + jnp.exp2(-jnp.abs(z))) * LOG2E


def _rev_cumsum(sp, tri):
    hi = sp.astype(BF16)
    lo = (sp - hi.astype(F32)).astype(BF16)
    return _dot(hi, tri) + _dot(lo, tri)


def _tri(n):
    r = lax.broadcasted_iota(jnp.int32, (n, n), 0)
    c = lax.broadcasted_iota(jnp.int32, (n, n), 1)
    return jnp.where(r >= c, 1.0, 0.0).astype(BF16)


def _head_ones(width):
    r = lax.broadcasted_iota(jnp.int32, (LANE, LANE), 0)
    c = lax.broadcasted_iota(jnp.int32, (LANE, LANE), 1)
    return jnp.where((r < width) == (c < width), 1.0, 0.0).astype(BF16)


def _head_sq_norms(x_bf16, ones_blk):
    xf = x_bf16.astype(F32)
    return _dot((xf * xf).astype(BF16), ones_blk)


def _sb_prompt_kernel(q_ref, k_ref, v_ref, o_ref, kmax_ref, *, tq, t):
    qi = pl.program_id(2)
    q = q_ref[0]
    lane = lax.broadcasted_iota(jnp.int32, (tq, LANE), 1)
    zero = jnp.zeros_like(q)
    qm = (jnp.where(lane < DH_SB, q, zero), jnp.where(lane >= DH_SB, q, zero))
    tri = _tri(tq)
    ones_blk = _head_ones(DH_SB)

    @pl.when(qi == 0)
    def _():
        def body(i, mx):
            kc = k_ref[0, pl.ds(pl.multiple_of(i * tq, tq), tq), :]
            return jnp.maximum(mx, _head_sq_norms(kc, ones_blk))
        mx = lax.fori_loop(0, t // tq, body, jnp.zeros((tq, LANE), F32))
        kmax_ref[...] = jnp.max(mx, axis=0, keepdims=True)

    bound = jnp.sqrt(_head_sq_norms(q, ones_blk) * kmax_ref[...]) * SB_BOUND_SLACK

    qs = jnp.concatenate(qm, axis=0)
    bz = jnp.concatenate([bound[:, 0:1], bound[:, DH_SB:DH_SB + 1]], axis=0)

    def alive(ccar):
        return (jnp.max(bz - ccar) > -SB_DEAD).astype(jnp.int32)

    def blk(ref, j):
        return ref[0, pl.ds(pl.multiple_of(j * tq, tq), tq), :]

    has_prev = qi > 0
    jp = jnp.maximum(qi - 1, 0)
    z_r = _dot_nt(qs, blk(k_ref, qi))
    z_l = _dot_nt(qs, blk(k_ref, jp))
    vis = lax.broadcasted_iota(jnp.int32, (2 * tq, tq), 1) < _row_query_index(2, tq, tq)
    sp_r = jnp.where(vis, _softplus(z_r), 0.0)
    cl_r = _rev_cumsum(sp_r, tri)
    total_r = cl_r[:, 0:1]
    cl_l = _rev_cumsum(_softplus(z_l), tri) + jnp.where(has_prev, total_r, -NEG_INF)
    w_r = jnp.where(vis, jnp.exp2(z_r - cl_r), 0.0).astype(BF16)
    w_l = jnp.exp2(z_l - cl_l).astype(BF16)
    acc = _dot(jnp.concatenate([w_l, w_r], axis=1),
               jnp.concatenate([blk(v_ref, jp), blk(v_ref, qi)], axis=0))
    ccar = jnp.where(has_prev, cl_l[:, 0:1], total_r)

    def body(c):
        j, _, ccar, acc = c
        z = _dot_nt(qs, blk(k_ref, j))
        cl = _rev_cumsum(_softplus(z), tri)
        w = jnp.exp2(z - ccar - cl).astype(BF16)
        ccar = ccar + cl[:, 0:1]
        return j - 1, alive(ccar), ccar, acc + _dot(w, blk(v_ref, j))

    out = lax.while_loop(lambda c: jnp.logical_and(c[0] >= 0, c[1] > 0), body,
                         (qi - 2, alive(ccar), ccar, acc))
    acc = out[3]
    o_ref[0] = jnp.where(lane < DH_SB, acc[:tq], acc[tq:]).astype(BF16)


def _sb_prompt(q, kb, vb, tq):
    b, t, _ = q.shape
    return pl.pallas_call(
        functools.partial(_sb_prompt_kernel, tq=tq, t=t),
        scratch_shapes=[pltpu.VMEM((1, LANE), F32)],
        grid=(b, D_SB // LANE, t // tq),
        in_specs=[
            pl.BlockSpec((1, tq, LANE), lambda i, h, j: (i, j, h)),
            pl.BlockSpec((1, t, LANE), lambda i, h, j: (i, 0, h)),
            pl.BlockSpec((1, t, LANE), lambda i, h, j: (i, 0, h)),
        ],
        out_specs=pl.BlockSpec((1, tq, LANE), lambda i, h, j: (i, j, h)),
        out_shape=jax.ShapeDtypeStruct((b, t, D_SB), BF16),
        compiler_params=_params("parallel", "parallel", "arbitrary"),
        name="sb_prompt",
    )(q, kb, vb)


def _sb_kmax_kernel(k_ref, o_ref):
    j = pl.program_id(1)
    x = k_ref[0]
    ones_blk = _head_ones(DH_SB)
    blk = jnp.concatenate(
        [jnp.max(_dot(jnp.square(x[:, g * LANE:(g + 1) * LANE]).astype(BF16), ones_blk), axis=0, keepdims=True)
         for g in range(D_SB // LANE)], axis=1)

    @pl.when(j == 0)
    def _():
        o_ref[0] = blk

    @pl.when(j > 0)
    def _():
        o_ref[0] = jnp.maximum(o_ref[0], blk)


def _sb_kmax(k_past, tk):
    b, past_len, _ = k_past.shape
    return pl.pallas_call(
        _sb_kmax_kernel,
        grid=(b, past_len // tk),
        in_specs=[pl.BlockSpec((1, tk, D_SB), lambda i, j: (i, j, 0))],
        out_specs=pl.BlockSpec((1, 1, D_SB), lambda i, j: (i, 0, 0)),
        out_shape=jax.ShapeDtypeStruct((b, 1, D_SB), F32),
        compiler_params=_params("parallel", "arbitrary"),
        name="sb_kmax",
    )(k_past)


def _heads(x):
    return [x[:, h * DH_SB:(h + 1) * DH_SB] for h in range(H_SB)]


def _sb_rows_attend(qh, kh, vh, t, width, ccar, acc, vis):
    z = jnp.concatenate([_dot_nt(qh[h], kh[h]) for h in range(H_SB)], axis=0)
    sp = _softplus(z)
    if vis is not None:
        sp = jnp.where(vis, sp, 0.0)
    cl = _rev_cumsum(sp, _tri(width))
    w = jnp.exp2(z - ccar - cl)
    if vis is not None:
        w = jnp.where(vis, w, 0.0)
    w = w.astype(BF16)
    pv = jnp.concatenate([_dot(w[h * t:(h + 1) * t], vh[h]) for h in range(H_SB)], axis=0)
    return ccar + cl[:, 0:1], acc + pv


def _sb_rows_bound(qh, kmax2):
    return jnp.concatenate(
        [jnp.sqrt(jnp.sum(jnp.square(qh[h].astype(F32)), axis=1, keepdims=True)
                  * kmax2[:, h * DH_SB:h * DH_SB + 1]) for h in range(H_SB)], axis=0) * SB_BOUND_SLACK


def _sb_rows_alive(bz, ccar):
    return (jnp.max(bz - ccar) > -SB_DEAD).astype(jnp.int32)


def _sb_rows_store(o_ref, acc, t):
    for h in range(H_SB):
        o_ref[0, :, h * DH_SB:(h + 1) * DH_SB] = acc[h * t:(h + 1) * t, :].astype(BF16)


def _sb_decode_head_kernel(q_ref, kn_ref, vn_ref, kt_ref, vt_ref, kmax_ref,
                           o_ref, c_ref, acc_ref, alive_ref, *, t, tk):
    rows = H_SB * t
    qh = _heads(q_ref[0])
    vis = lax.broadcasted_iota(jnp.int32, (rows, t), 1) < _row_query_index(H_SB, t, t)
    ccar, acc = _sb_rows_attend(qh, _heads(kn_ref[0]), _heads(vn_ref[0]), t, t,
                                jnp.zeros((rows, 1), F32), jnp.zeros((rows, DH_SB), F32), vis)
    ccar, acc = _sb_rows_attend(qh, _heads(kt_ref[0].astype(BF16)), _heads(vt_ref[0].astype(BF16)),
                                t, tk, ccar, acc, None)
    _sb_rows_store(o_ref, acc, t)
    c_ref[0] = ccar
    acc_ref[0] = acc
    alive_ref[0] = jnp.zeros((1, LANE), jnp.int32) + _sb_rows_alive(_sb_rows_bound(qh, kmax_ref[0]), ccar)


def _sb_decode_rest_kernel(q_ref, kmax_ref, c_ref, acc_ref, kp_hbm, vp_hbm, o_ref, kbuf, vbuf, sem,
                           *, t, tk, nk):
    b = pl.program_id(0)
    qh = _heads(q_ref[0])
    bz = _sb_rows_bound(qh, kmax_ref[0])

    def body(c):
        j, _, ccar, acc = c
        start = pl.multiple_of(j * tk, tk)
        ck = pltpu.make_async_copy(kp_hbm.at[b, pl.ds(start, tk), :], kbuf, sem.at[0])
        cv = pltpu.make_async_copy(vp_hbm.at[b, pl.ds(start, tk), :], vbuf, sem.at[1])
        ck.start()
        cv.start()
        ck.wait()
        cv.wait()
        ccar, acc = _sb_rows_attend(qh, _heads(kbuf[...].astype(BF16)), _heads(vbuf[...].astype(BF16)),
                                    t, tk, ccar, acc, None)
        return j - 1, _sb_rows_alive(bz, ccar), ccar, acc

    ccar = c_ref[0]
    out = lax.while_loop(lambda c: jnp.logical_and(c[0] >= 0, c[1] > 0), body,
                         (jnp.int32(nk - 2), _sb_rows_alive(bz, ccar), ccar, acc_ref[0]))
    _sb_rows_store(o_ref, out[3], t)


def _sb_decode(q, k_past, v_past, kn, vn, tk):
    b, t, _ = q.shape
    past_len = k_past.shape[1]
    nk = past_len // tk
    rows = H_SB * t
    k_flat = k_past.reshape(b, past_len, D_SB)
    kmax2 = _sb_kmax(k_flat, min(512, past_len))
    k_tail = k_past[:, past_len - tk:].reshape(b, tk, D_SB)
    v_tail = v_past[:, past_len - tk:].reshape(b, tk, D_SB)
    tokb = lambda: pl.BlockSpec((1, t, D_SB), lambda i: (i, 0, 0))
    tail = lambda: pl.BlockSpec((1, tk, D_SB), lambda i: (i, 0, 0))
    kmx = lambda: pl.BlockSpec((1, 1, D_SB), lambda i: (i, 0, 0))
    car = lambda: pl.BlockSpec((1, rows, 1), lambda i: (i, 0, 0))
    accs = lambda: pl.BlockSpec((1, rows, DH_SB), lambda i: (i, 0, 0))
    o, ccar, acc, alive = pl.pallas_call(
        functools.partial(_sb_decode_head_kernel, t=t, tk=tk),
        grid=(b,),
        in_specs=[tokb(), tokb(), tokb(), tail(), tail(), kmx()],
        out_specs=[tokb(), car(), accs(), pl.BlockSpec((1, 1, LANE), lambda i: (i, 0, 0))],
        out_shape=[jax.ShapeDtypeStruct((b, t, D_SB), BF16), jax.ShapeDtypeStruct((b, rows, 1), F32),
                   jax.ShapeDtypeStruct((b, rows, DH_SB), F32),
                   jax.ShapeDtypeStruct((b, 1, LANE), jnp.int32)],
        compiler_params=_params("parallel"),
        name="sb_decode_head",
    )(q, kn, vn, k_tail, v_tail, kmax2)
    if nk < 2:
        return o

    def rest(args):
        q_, kmax2_, ccar_, acc_, k_flat_, v_past_, _ = args
        return pl.pallas_call(
            functools.partial(_sb_decode_rest_kernel, t=t, tk=tk, nk=nk),
            grid=(b,),
            in_specs=[tokb(), kmx(), car(), accs(), pl.BlockSpec(memory_space=pl.ANY),
                      pl.BlockSpec(memory_space=pl.ANY)],
            out_specs=tokb(),
            out_shape=jax.ShapeDtypeStruct((b, t, D_SB), BF16),
            scratch_shapes=[pltpu.VMEM((tk, D_SB), F32), pltpu.VMEM((tk, D_SB), F32),
                            pltpu.SemaphoreType.DMA((2,))],
            compiler_params=_params("arbitrary"),
            name="sb_decode_rest",
        )(q_, kmax2_, ccar_, acc_, k_flat_, v_past_.reshape(b, past_len, D_SB))

    return lax.cond(jnp.max(alive) > 0, rest, lambda args: args[-1],
                    (q, kmax2, ccar, acc, k_flat, v_past, o))


def _post_mix_kernel(*refs, n_in):
    x_ref = refs[0]
    a_refs = refs[1:1 + n_in]
    w_ref, g_ref, wq_ref, mk_ref, mv_ref, wo_ref, o_ref = refs[1 + n_in:]
    a = a_refs[0][0] if n_in == 1 else jnp.concatenate([r[0] for r in a_refs], axis=1)
    x = x_ref[0] + _dot(a, w_ref[...])
    hq = _rms(x, g_ref[...], EPS).astype(BF16)
    q = (_dot(hq, wq_ref[...]) * (DH_MEM ** -0.5)).astype(BF16)
    y = x
    for h in range(H_MEM):
        hs = slice(h * DH_MEM, (h + 1) * DH_MEM)
        s = _dot_nt(q[:, hs], mk_ref[0, :, hs])
        p = jnp.exp(s - jnp.max(s, axis=1, keepdims=True))
        l = jnp.sum(p, axis=1, keepdims=True)
        oh = _dot(p.astype(BF16), mv_ref[0, :, hs]) / l
        y = y + _dot(oh.astype(BF16), wo_ref[hs, :])
    o_ref[0] = y


def _post_mix(x, acts, w, g, wq, mk, mv, wo, tm):
    b, t, d = x.shape
    n_in = len(acts)
    tok = lambda w: pl.BlockSpec((1, tm, w), lambda i, j: (i, j, 0))
    mem = pl.BlockSpec((1, N_MEM, D_MODEL), lambda i, j: (i, 0, 0))
    return pl.pallas_call(
        functools.partial(_post_mix_kernel, n_in=n_in),
        grid=(b, t // tm),
        in_specs=([tok(d)] + [tok(a.shape[-1]) for a in acts]
                  + [_full(w.shape), _full((1, d)), _full((d, d)), mem, mem, _full((d, d))]),
        out_specs=tok(d),
        out_shape=jax.ShapeDtypeStruct((b, t, d), F32),
        compiler_params=_params("parallel", "parallel"),
        name="post_mix",
    )(x, *acts, w, g, wq, mk, mv, wo)


def _ffn_kernel(x_ref, g_ref, wu_ref, wd_ref, gf_ref, o_ref, *, fc, final):
    x = x_ref[...]
    h = _rms(x, g_ref[...], EPS).astype(BF16)
    y = x
    for c in range(D_FF // fc):
        u = jnp.maximum(_dot(h, wu_ref[:, c * fc:(c + 1) * fc]), 0.0)
        y = y + _dot((u * u).astype(BF16), wd_ref[c * fc:(c + 1) * fc, :])
    if final:
        y = _rms(y, gf_ref[...], EPS)
    o_ref[...] = y


def _ffn(x2d, g, wu, wd, gf, final, tm, fc):
    m, d = x2d.shape
    return pl.pallas_call(
        functools.partial(_ffn_kernel, fc=fc, final=final),
        grid=(m // tm,),
        in_specs=[pl.BlockSpec((tm, d), lambda i: (i, 0)), _full((1, d)), _full((d, D_FF)),
                  _full((D_FF, d)), _full((1, d))],
        out_specs=pl.BlockSpec((tm, d), lambda i: (i, 0)),
        out_shape=jax.ShapeDtypeStruct((m, d), F32),
        compiler_params=_params("parallel"),
        name="ffn",
    )(x2d, g, wu, wd, gf)


def _rope_tables(pos):
    half = DH_QK // 2
    inv = jnp.power(ROPE_THETA, -jnp.arange(half, dtype=F32) * (2.0 / DH_QK))
    ang = pos.astype(F32)[:, None] * inv[None, :]
    cos, sin = jnp.cos(ang), jnp.sin(ang)
    reps = LANE // DH_QK
    return (jnp.tile(jnp.concatenate([cos, cos], axis=1), (1, reps)),
            jnp.tile(jnp.concatenate([-sin, sin], axis=1), (1, reps)))


def _row(v):
    return v.reshape(1, -1)


def _run(x, mem_kb, mem_vb, past, p, tm, tq):
    b, t, d = x.shape
    past_len = 0 if past is None else past[0].shape[2]
    pos = jnp.arange(past_len, past_len + t)
    cos, sin = _rope_tables(pos)
    outs = {}
    for i in range(DEPTH):
        if i % 2 == 0:
            e = i // 2
            lam_init = 0.8 - 0.6 * math.exp(-0.3 * i)
            state = jnp.zeros((b, CONV_W - 1, D_CONV), F32) if past is None else past[2][e]
            q, k, kb, v, vb, gc, nc = _even_in(x, _row(p['norm_mix'][i]), p['w_in_even_b'][e], cos, sin,
                                               p['conv_w'][e], state, tm)
            lam_args = (_row(p['lambda_q1'][e]), _row(p['lambda_k1'][e]), _row(p['lambda_q2'][e]),
                        _row(p['lambda_k2'][e]), _row(p['subln_gain'][e]))
            if past is None:
                o = _diff_prompt(q, kb, vb, *lam_args, lam_init, min(4 * tq, t), min(1024, t))
            else:
                o = _diff_decode(q, past[0][e].reshape(b, past_len, Q_DIFF),
                                 past[1][e].reshape(b, past_len, D_DIFF), kb, vb, *lam_args, lam_init,
                                 min(512, past_len))
            outs['dk'] = k.reshape(1, b, t, 2 * H_DIFF, DH_QK)
            outs['dv'] = v.reshape(1, b, t, H_DIFF, DV_DIFF)
            outs['conv'] = nc[None]
            acts = [o, gc]
            w_out = p['w_out_even_b'][e]
        else:
            o_idx = i // 2
            segs = ((0, D_SB, False, True, DH_SB ** -0.5 * LOG2E), (D_SB, D_SB, True, True, 1.0),
                    (2 * D_SB, D_SB, True, True, 1.0))
            q, k, kb, v, vb = _norm_proj(x, p['norm_mix'][i].reshape(1, 1, d),
                                         p['w_in_odd_b'][o_idx][None], segs, tm)
            if past is None:
                o = _sb_prompt(q, kb, vb, tq)
            else:
                o = _sb_decode(q, past[3][o_idx], past[4][o_idx], kb, vb, min(256, past_len))
            outs['sk'] = k.reshape(1, b, t, H_SB, DH_SB)
            outs['sv'] = v.reshape(1, b, t, H_SB, DH_SB)
            acts = [o]
            w_out = p['w_out_odd_b'][o_idx]
        x = _post_mix(x, acts, w_out, _row(p['norm_cross'][i]), p['w_q_mem_b'][i], mem_kb[i], mem_vb[i],
                      p['w_o_mem_b'][i], tm)
        m = b * t
        x = _ffn(x.reshape(m, d), _row(p['norm_ffn'][i]), p['w_ffn_up_b'][i], p['w_ffn_down_b'][i],
                 _row(p['norm_final']), i == DEPTH - 1, min(512, m), 1024).reshape(b, t, d)
    return x, outs


def kernel(x_prompt, x_sample, cache_diff_k, cache_diff_v, state_conv, cache_sb_k, cache_sb_v, cache_mem_k, cache_mem_v, mem_prompt, w_in_even, w_out_even, lambda_q1, lambda_k1, lambda_q2, lambda_k2, subln_gain, conv_w, w_in_odd, w_out_odd, norm_mix, norm_mem, norm_cross, w_q_mem, w_k_mem, w_v_mem, w_o_mem, norm_ffn, w_ffn_up, w_ffn_down, norm_final):
    p = dict(norm_mix=norm_mix, norm_cross=norm_cross, norm_ffn=norm_ffn, norm_final=norm_final,
             lambda_q1=lambda_q1, lambda_k1=lambda_k1, lambda_q2=lambda_q2, lambda_k2=lambda_k2,
             subln_gain=subln_gain, conv_w=conv_w)
    for name, w in (('w_in_even', w_in_even), ('w_out_even', w_out_even), ('w_in_odd', w_in_odd),
                    ('w_out_odd', w_out_odd), ('w_q_mem', w_q_mem), ('w_o_mem', w_o_mem),
                    ('w_ffn_up', w_ffn_up), ('w_ffn_down', w_ffn_down)):
        p[name + '_b'] = w.astype(BF16)

    bp = mem_prompt.shape[0]
    w_kv = jnp.concatenate([w_k_mem, w_v_mem], axis=-1).astype(BF16)
    segs = ((0, D_MODEL, True, True, 1.0), (D_MODEL, D_MODEL, True, True, 1.0))
    mk, mkb, mv, mvb = _norm_proj(mem_prompt, norm_mem.reshape(DEPTH, 1, D_MODEL), w_kv, segs, N_MEM)
    p_mem_k = mk.reshape(DEPTH, bp, N_MEM, H_MEM, DH_MEM)
    p_mem_v = mv.reshape(DEPTH, bp, N_MEM, H_MEM, DH_MEM)
    mkb = mkb.reshape(DEPTH, bp, N_MEM, D_MODEL)
    mvb = mvb.reshape(DEPTH, bp, N_MEM, D_MODEL)

    t_p = x_prompt.shape[1]
    y_prompt, po = _run(x_prompt, mkb, mvb, None, p, min(512, t_p), min(256, t_p))

    bs = x_sample.shape[0]
    t_s = x_sample.shape[1]
    cmk = cache_mem_k.reshape(DEPTH, bs, N_MEM, D_MODEL).astype(BF16)
    cmv = cache_mem_v.reshape(DEPTH, bs, N_MEM, D_MODEL).astype(BF16)
    y_sample, so = _run(x_sample, cmk, cmv,
                        (cache_diff_k, cache_diff_v, state_conv, cache_sb_k, cache_sb_v), p, t_s, t_s)

    return (y_prompt, y_sample, po['dk'], po['dv'], po['conv'], po['sk'], po['sv'], p_mem_k, p_mem_v,
            so['dk'], so['dv'], so['conv'], so['sk'], so['sv'])
```

```python
import functools
import math

import jax
import jax.numpy as jnp
from jax import lax
from jax.experimental import pallas as pl
from jax.experimental.pallas import tpu as pltpu

F32 = jnp.float32
BF16 = jnp.bfloat16

D_MODEL = 1024
DEPTH = 2
CHUNK = 64
H_DIFF = 4
DH_QK = 64
DV_DIFF = 2 * DH_QK
D_DIFF = H_DIFF * DV_DIFF
Q_DIFF = 2 * H_DIFF * DH_QK
D_CONV = D_MODEL - D_DIFF
CONV_W = 3
EVEN_IN = 2 * Q_DIFF + D_DIFF + 3 * D_CONV
H_SB = 16
DH_SB = 64
D_SB = H_SB * DH_SB
N_MEM = 256
H_MEM = 4
DH_MEM = D_MODEL // H_MEM
D_FF = 4 * D_MODEL
ROPE_THETA = 10000.0
EPS = 1e-6
SUBLN_EPS = 1e-5
NEG_INF = -1e30
LOG2E = math.log2(math.e)
SB_DEAD = 105.0 * LOG2E
SB_BOUND_SLACK = 1.01

LANE = 128
SUBLANE = 8
VMEM_LIMIT = 52 * 1024 * 1024


def _params(*sem):
    return pltpu.CompilerParams(dimension_semantics=sem, vmem_limit_bytes=VMEM_LIMIT)


def _rms(x, g, eps):
    ms = jnp.mean(x * x, axis=-1, keepdims=True)
    return x * lax.rsqrt(ms + eps) * g


def _dot(a, b):
    return jnp.dot(a, b, preferred_element_type=F32)


def _dot_nt(a, b):
    return lax.dot_general(a, b, (((1,), (1,)), ((), ())), preferred_element_type=F32)


def _full(shape):
    n = len(shape)
    return pl.BlockSpec(shape, lambda *_: (0,) * n)


def _chunk_of(pos):
    shift = CHUNK.bit_length() - 1
    assert 1 << shift == CHUNK
    return lax.shift_right_logical(pos, shift)


def _row_query_index(groups, t, width):
    r = lax.broadcasted_iota(jnp.int32, (t, width), 0)
    return jnp.concatenate([r] * groups, axis=0)


def _lane_band(lane, s, width):
    return (lane >= s * width) & (lane < (s + 1) * width)


def _even_in_kernel(x_ref, g_ref, w_ref, cos_ref, sin_ref, cw_ref, st_ref,
                    q_ref, k_ref, kb_ref, v_ref, vb_ref, gc_ref, nc_ref, ext_ref, *, tm, nt):
    t = pl.program_id(1)
    h = _rms(x_ref[0], g_ref[...], EPS).astype(BF16)

    def proj(lo, width):
        return _dot(h, w_ref[:, lo:lo + width])

    cos = cos_ref[...]
    sin = sin_ref[...]
    lane = lax.broadcasted_iota(jnp.int32, (tm, LANE), 1)
    first_half = (lane & (DH_QK // 2)) == 0

    def rope_group(yj):
        sw = jnp.where(first_half, pltpu.roll(yj, LANE - DH_QK // 2, 1), pltpu.roll(yj, DH_QK // 2, 1))
        return yj * cos + sw * sin

    yq = proj(0, Q_DIFF)
    yk = proj(Q_DIFF, Q_DIFF)
    for j in range(Q_DIFF // LANE):
        sl = slice(j * LANE, (j + 1) * LANE)
        q_ref[0, :, sl] = (rope_group(yq[:, sl]) * (DH_QK ** -0.5)).astype(BF16)
        kj = rope_group(yk[:, sl])
        k_ref[0, :, sl] = kj
        kb_ref[0, :, sl] = kj.astype(BF16)
    yv = proj(2 * Q_DIFF, D_DIFF)
    v_ref[0] = yv
    vb_ref[0] = yv.astype(BF16)

    base = 2 * Q_DIFF + D_DIFF
    gate_b = proj(base, D_CONV)
    cu = proj(base + D_CONV, D_CONV) * proj(base + 2 * D_CONV, D_CONV)

    @pl.when(t == 0)
    def _():
        ext_ref[SUBLANE - 2:SUBLANE, :] = st_ref[0]

    ext_ref[SUBLANE:SUBLANE + tm, :] = cu
    cw = cw_ref[...]
    conv = (ext_ref[SUBLANE - 2:SUBLANE - 2 + tm, :] * cw[0:1, :]
            + ext_ref[SUBLANE - 1:SUBLANE - 1 + tm, :] * cw[1:2, :]
            + cu * cw[2:3, :])
    gc_ref[0] = (gate_b * conv).astype(BF16)
    ext_ref[0:SUBLANE, :] = ext_ref[tm:tm + SUBLANE, :]

    @pl.when(t == nt - 1)
    def _():
        nc_ref[0] = ext_ref[tm + SUBLANE - 2:tm + SUBLANE, :]


def _even_in(x, g, wb, cos, sin, cw, state, tm):
    b, t, _ = x.shape
    nt = t // tm
    tok = lambda w: pl.BlockSpec((1, tm, w), lambda i, j: (i, j, 0))
    outs = [
        jax.ShapeDtypeStruct((b, t, Q_DIFF), BF16),
        jax.ShapeDtypeStruct((b, t, Q_DIFF), F32),
        jax.ShapeDtypeStruct((b, t, Q_DIFF), BF16),
        jax.ShapeDtypeStruct((b, t, D_DIFF), F32),
        jax.ShapeDtypeStruct((b, t, D_DIFF), BF16),
        jax.ShapeDtypeStruct((b, t, D_CONV), BF16),
        jax.ShapeDtypeStruct((b, CONV_W - 1, D_CONV), F32),
    ]
    return pl.pallas_call(
        functools.partial(_even_in_kernel, tm=tm, nt=nt),
        grid=(b, nt),
        in_specs=[
            tok(D_MODEL),
            _full((1, D_MODEL)),
            _full((D_MODEL, EVEN_IN)),
            pl.BlockSpec((tm, LANE), lambda i, j: (j, 0)),
            pl.BlockSpec((tm, LANE), lambda i, j: (j, 0)),
            _full((CONV_W, D_CONV)),
            pl.BlockSpec((1, CONV_W - 1, D_CONV), lambda i, j: (i, 0, 0)),
        ],
        out_specs=[tok(Q_DIFF), tok(Q_DIFF), tok(Q_DIFF), tok(D_DIFF), tok(D_DIFF), tok(D_CONV),
                   pl.BlockSpec((1, CONV_W - 1, D_CONV), lambda i, j: (i, 0, 0))],
        out_shape=outs,
        scratch_shapes=[pltpu.VMEM((tm + 2 * SUBLANE, D_CONV), F32)],
        compiler_params=_params("parallel", "arbitrary"),
        name="even_in",
    )(x, g, wb, cos, sin, cw, state)


def _norm_proj_kernel(x_ref, g_ref, w_ref, *out_refs, segs):
    h = _rms(x_ref[0], g_ref[0], EPS).astype(BF16)
    i = 0
    for lo, width, want_f32, want_bf16, scale in segs:
        y = _dot(h, w_ref[0, :, lo:lo + width])
        if scale != 1.0:
            y = y * scale
        if want_f32:
            out_refs[i][0] = y
            i += 1
        if want_bf16:
            out_refs[i][0] = y.astype(BF16)
            i += 1


def _norm_proj(x, g, wb, segs, tm):
    b, t, d = x.shape
    n = wb.shape[-1]
    ng = wb.shape[0]
    nt = t // tm
    outs, specs = [], []
    for lo, width, want_f32, want_bf16, _ in segs:
        for want, dt in ((want_f32, F32), (want_bf16, BF16)):
            if want:
                outs.append(jax.ShapeDtypeStruct((ng * b, t, width), dt))
                specs.append(pl.BlockSpec((1, tm, width), lambda i, j: (i, j, 0)))
    return pl.pallas_call(
        functools.partial(_norm_proj_kernel, segs=segs),
        grid=(ng * b, nt),
        in_specs=[
            pl.BlockSpec((1, tm, d), lambda i, j: (i % b, j, 0)),
            pl.BlockSpec((1, 1, d), lambda i, j: (i // b, 0, 0)),
            pl.BlockSpec((1, d, n), lambda i, j: (i // b, 0, 0)),
        ],
        out_specs=specs,
        out_shape=outs,
        compiler_params=_params("parallel", "parallel"),
        name="norm_proj",
    )(x, g, wb)


def _lambda_full(lq1_ref, lk1_ref, lq2_ref, lk2_ref, lam_init):
    s1 = jnp.sum(lq1_ref[...] * lk1_ref[...], axis=1, keepdims=True)
    s2 = jnp.sum(lq2_ref[...] * lk2_ref[...], axis=1, keepdims=True)
    return jnp.exp(s1) - jnp.exp(s2) + lam_init


def _subln(o, gsub_ref, lam_init):
    return _rms(o, gsub_ref[...], SUBLN_EPS) * (1.0 - lam_init)


def _diff_prompt_kernel(q_ref, k_ref, v_ref, lq1_ref, lk1_ref, lq2_ref, lk2_ref, gsub_ref,
                        o_ref, *, tq, tk, lam_init):
    qi = pl.program_id(2)
    ratio = tk // tq
    q = q_ref[0]
    lane = lax.broadcasted_iota(jnp.int32, (tq, LANE), 1)
    zero = jnp.zeros_like(q)
    qm = (jnp.where(lane < DH_QK, q, zero), jnp.where(lane >= DH_QK, q, zero))
    ones = jnp.ones((tk, LANE), BF16)

    def step(j, carry, masked):
        start = pl.multiple_of(j * tk, tk)
        kb = k_ref[0, pl.ds(start, tk), :]
        vb = jnp.concatenate([v_ref[0, pl.ds(start, tk), :], ones], axis=1)
        if masked:
            r = _chunk_of(lax.broadcasted_iota(jnp.int32, (tq, tk), 0)) + lax.rem(qi, ratio) * (tq // CHUNK)
            c = _chunk_of(lax.broadcasted_iota(jnp.int32, (tq, tk), 1))
            vis = c <= r
        ss = [_dot_nt(qm[mi], kb) for mi in range(2)]
        if masked:
            ss = [jnp.where(vis, s, NEG_INF) for s in ss]
        ms = [jnp.maximum(carry[mi][0], jnp.max(ss[mi], axis=1, keepdims=True)) for mi in range(2)]
        ps = [jnp.exp(ss[mi] - ms[mi]).astype(BF16) for mi in range(2)]
        pv = [_dot(ps[mi], vb) for mi in range(2)]
        return tuple((ms[mi], jnp.exp(carry[mi][0] - ms[mi]) * carry[mi][1] + pv[mi]) for mi in range(2))

    init = tuple((jnp.full((tq, 1), NEG_INF, F32), jnp.zeros((tq, 2 * LANE), F32)) for _ in range(2))
    n_full = lax.div(qi, ratio)
    carry = lax.fori_loop(0, n_full, lambda j, c: step(j, c, False), init)
    carry = step(n_full, carry, True)
    lam = _lambda_full(lq1_ref, lk1_ref, lq2_ref, lk2_ref, lam_init)
    (_, a1), (_, a2) = carry
    o = a1[:, :LANE] / a1[:, LANE:] - lam * (a2[:, :LANE] / a2[:, LANE:])
    o_ref[0] = _subln(o, gsub_ref, lam_init).astype(BF16)


def _diff_prompt(q, kb, vb, lq1, lk1, lq2, lk2, gsub, lam_init, tq, tk):
    b, t, _ = q.shape
    lam_spec = _full((1, DH_QK))
    return pl.pallas_call(
        functools.partial(_diff_prompt_kernel, tq=tq, tk=tk, lam_init=lam_init),
        grid=(b, H_DIFF, t // tq),
        in_specs=[
            pl.BlockSpec((1, tq, LANE), lambda i, h, j: (i, j, h)),
            pl.BlockSpec((1, t, LANE), lambda i, h, j: (i, 0, h)),
            pl.BlockSpec((1, t, LANE), lambda i, h, j: (i, 0, h)),
            lam_spec, lam_spec, lam_spec, lam_spec,
            _full((1, DV_DIFF)),
        ],
        out_specs=pl.BlockSpec((1, tq, LANE), lambda i, h, j: (i, j, h)),
        out_shape=jax.ShapeDtypeStruct((b, t, D_DIFF), BF16),
        compiler_params=_params("parallel", "parallel", "arbitrary"),
        name="diff_prompt",
    )(q, kb, vb, lq1, lk1, lq2, lk2, gsub)


def _diff_decode_kernel(q_ref, kp_ref, vp_ref, kn_ref, vn_ref, lq1_ref, lk1_ref, lq2_ref, lk2_ref,
                        gsub_ref, o_ref, qs_ref, m_ref, l_ref, acc_ref, *, t, tk, nk, past_len,
                        lam_init):
    j = pl.program_id(1)
    nsub = 2 * H_DIFF
    rows = nsub * t

    @pl.when(j == 0)
    def _():
        q = q_ref[0]
        lane = lax.broadcasted_iota(jnp.int32, (t, Q_DIFF), 1)
        for s in range(nsub):
            qs_ref[s * t:(s + 1) * t, :] = jnp.where(_lane_band(lane, s, DH_QK), q, jnp.zeros_like(q))
        m_ref[...] = jnp.full(m_ref.shape, NEG_INF, F32)
        l_ref[...] = jnp.zeros(l_ref.shape, F32)
        acc_ref[...] = jnp.zeros(acc_ref.shape, F32)

    def update(kb, vb, k_start, width):
        s = _dot_nt(qs_ref[...], kb)
        q_pos = past_len + _row_query_index(nsub, t, width)
        k_pos = k_start + lax.broadcasted_iota(jnp.int32, (rows, width), 1)
        s = jnp.where(_chunk_of(k_pos) <= _chunk_of(q_pos), s, NEG_INF)
        m = m_ref[...]
        m_new = jnp.maximum(m, jnp.max(s, axis=1, keepdims=True))
        alpha = jnp.exp(m - m_new)
        p = jnp.exp(s - m_new)
        l_ref[...] = alpha * l_ref[...] + jnp.sum(p, axis=1, keepdims=True)
        m_ref[...] = m_new
        pb = p.astype(BF16)
        for h in range(H_DIFF):
            rs = slice(2 * h * t, (2 * h + 2) * t)
            acc_ref[rs, :] = alpha[rs] * acc_ref[rs, :] + _dot(pb[rs], vb[:, h * LANE:(h + 1) * LANE])

    update(kp_ref[0].astype(BF16), vp_ref[0].astype(BF16), j * tk, tk)

    @pl.when(j == nk - 1)
    def _():
        update(kn_ref[0], vn_ref[0], past_len, t)
        lam = _lambda_full(lq1_ref, lk1_ref, lq2_ref, lk2_ref, lam_init)
        on = acc_ref[...] / l_ref[...]
        for h in range(H_DIFF):
            o = on[2 * h * t:(2 * h + 1) * t] - lam * on[(2 * h + 1) * t:(2 * h + 2) * t]
            o_ref[0, :, h * LANE:(h + 1) * LANE] = _subln(o, gsub_ref, lam_init).astype(BF16)


def _diff_decode(q, k_past, v_past, kn, vn, lq1, lk1, lq2, lk2, gsub, lam_init, tk):
    b, t, _ = q.shape
    past_len = k_past.shape[1]
    nk = past_len // tk
    rows = 2 * H_DIFF * t
    lam_spec = _full((1, DH_QK))
    tokb = lambda w: pl.BlockSpec((1, t, w), lambda i, j: (i, 0, 0))
    return pl.pallas_call(
        functools.partial(_diff_decode_kernel, t=t, tk=tk, nk=nk, past_len=past_len,
                          lam_init=lam_init),
        grid=(b, nk),
        in_specs=[
            tokb(Q_DIFF),
            pl.BlockSpec((1, tk, Q_DIFF), lambda i, j: (i, j, 0)),
            pl.BlockSpec((1, tk, D_DIFF), lambda i, j: (i, j, 0)),
            tokb(Q_DIFF), tokb(D_DIFF),
            lam_spec, lam_spec, lam_spec, lam_spec,
            _full((1, DV_DIFF)),
        ],
        out_specs=tokb(D_DIFF),
        out_shape=jax.ShapeDtypeStruct((b, t, D_DIFF), BF16),
        scratch_shapes=[pltpu.VMEM((rows, Q_DIFF), BF16), pltpu.VMEM((rows, 1), F32),
                        pltpu.VMEM((rows, 1), F32), pltpu.VMEM((rows, DV_DIFF), F32)],
        compiler_params=_params("parallel", "arbitrary"),
        name="diff_decode",
    )(q, k_past, v_past, kn, vn, lq1, lk1, lq2, lk2, gsub)


def _softplus(z):
    return jnp.maximum(z, 0.0) + jnp.log(1.0 + jnp.exp2(-jnp.abs(z))) * LOG2E


def _rev_cumsum(sp, tri):
    hi = sp.astype(BF16)
    lo = (sp - hi.astype(F32)).astype(BF16)
    return _dot(hi, tri) + _dot(lo, tri)


def _tri(n):
    r = lax.broadcasted_iota(jnp.int32, (n, n), 0)
    c = lax.broadcasted_iota(jnp.int32, (n, n), 1)
    return jnp.where(r >= c, 1.0, 0.0).astype(BF16)


def _head_ones(width):
    r = lax.broadcasted_iota(jnp.int32, (LANE, LANE), 0)
    c = lax.broadcasted_iota(jnp.int32, (LANE, LANE), 1)
    return jnp.where((r < width) == (c < width), 1.0, 0.0).astype(BF16)


def _head_sq_norms(x_bf16, ones_blk):
    xf = x_bf16.astype(F32)
    return _dot((xf * xf).astype(BF16), ones_blk)


def _sb_prompt_kernel(q_ref, k_ref, v_ref, o_ref, kmax_ref, *, tq, t):
    qi = pl.program_id(2)
    q = q_ref[0]
    lane = lax.broadcasted_iota(jnp.int32, (tq, LANE), 1)
    zero = jnp.zeros_like(q)
    qm = (jnp.where(lane < DH_SB, q, zero), jnp.where(lane >= DH_SB, q, zero))
    tri = _tri(tq)
    ones_blk = _head_ones(DH_SB)

    @pl.when(qi == 0)
    def _():
        rows = math.gcd(t, 4 * tq)

        def body(i, mx):
            kc = k_ref[0, pl.ds(pl.multiple_of(i * rows, rows), rows), :]
            return jnp.maximum(mx, _head_sq_norms(kc, ones_blk))
        mx = lax.fori_loop(0, t // rows, body, jnp.zeros((rows, LANE), F32))
        kmax_ref[...] = jnp.max(mx, axis=0, keepdims=True)

    bound = jnp.sqrt(_head_sq_norms(q, ones_blk) * kmax_ref[...]) * SB_BOUND_SLACK

    qs = jnp.concatenate(qm, axis=0)
    bz = jnp.concatenate([bound[:, 0:1], bound[:, DH_SB:DH_SB + 1]], axis=0)

    def alive(ccar):
        return (jnp.max(bz - ccar) > -SB_DEAD).astype(jnp.int32)

    def blk(ref, j):
        return ref[0, pl.ds(pl.multiple_of(j * tq, tq), tq), :]

    has_prev = qi > 0
    jp = jnp.maximum(qi - 1, 0)
    z_r = _dot_nt(qs, blk(k_ref, qi))
    z_l = _dot_nt(qs, blk(k_ref, jp))
    vis = lax.broadcasted_iota(jnp.int32, (2 * tq, tq), 1) < _row_query_index(2, tq, tq)
    sp_r = jnp.where(vis, _softplus(z_r), 0.0)
    cl_r = _rev_cumsum(sp_r, tri)
    total_r = cl_r[:, 0:1]
    cl_l = _rev_cumsum(_softplus(z_l), tri) + jnp.where(has_prev, total_r, -NEG_INF)
    w_r = jnp.where(vis, jnp.exp2(z_r - cl_r), 0.0).astype(BF16)
    w_l = jnp.exp2(z_l - cl_l).astype(BF16)
    acc = _dot(jnp.concatenate([w_l, w_r], axis=1),
               jnp.concatenate([blk(v_ref, jp), blk(v_ref, qi)], axis=0))
    ccar = jnp.where(has_prev, cl_l[:, 0:1], total_r)

    def body(c):
        j, _, ccar, acc = c
        z = _dot_nt(qs, blk(k_ref, j))
        cl = _rev_cumsum(_softplus(z), tri)
        w = jnp.exp2(z - ccar - cl).astype(BF16)
        ccar = ccar + cl[:, 0:1]
        return j - 1, alive(ccar), ccar, acc + _dot(w, blk(v_ref, j))

    out = lax.while_loop(lambda c: jnp.logical_and(c[0] >= 0, c[1] > 0), body,
                         (qi - 2, alive(ccar), ccar, acc))
    acc = out[3]
    o_ref[0] = jnp.where(lane < DH_SB, acc[:tq], acc[tq:]).astype(BF16)


def _sb_prompt(q, kb, vb, tq):
    b, t, _ = q.shape
    return pl.pallas_call(
        functools.partial(_sb_prompt_kernel, tq=tq, t=t),
        scratch_shapes=[pltpu.VMEM((1, LANE), F32)],
        grid=(b, D_SB // LANE, t // tq),
        in_specs=[
            pl.BlockSpec((1, tq, LANE), lambda i, h, j: (i, j, h)),
            pl.BlockSpec((1, t, LANE), lambda i, h, j: (i, 0, h)),
            pl.BlockSpec((1, t, LANE), lambda i, h, j: (i, 0, h)),
        ],
        out_specs=pl.BlockSpec((1, tq, LANE), lambda i, h, j: (i, j, h)),
        out_shape=jax.ShapeDtypeStruct((b, t, D_SB), BF16),
        compiler_params=_params("parallel", "parallel", "arbitrary"),
        name="sb_prompt",
    )(q, kb, vb)


def _sb_kmax_kernel(k_ref, o_ref):
    j = pl.program_id(1)
    x = k_ref[0]
    ones_blk = _head_ones(DH_SB)
    blk = jnp.concatenate(
        [jnp.max(_head_sq_norms(x[:, g * LANE:(g + 1) * LANE], ones_blk), axis=0, keepdims=True)
         for g in range(D_SB // LANE)], axis=1)

    @pl.when(j == 0)
    def _():
        o_ref[0] = blk

    @pl.when(j > 0)
    def _():
        o_ref[0] = jnp.maximum(o_ref[0], blk)


def _sb_kmax(k_past, tk):
    b, past_len, _ = k_past.shape
    return pl.pallas_call(
        _sb_kmax_kernel,
        grid=(b, past_len // tk),
        in_specs=[pl.BlockSpec((1, tk, D_SB), lambda i, j: (i, j, 0))],
        out_specs=pl.BlockSpec((1, 1, D_SB), lambda i, j: (i, 0, 0)),
        out_shape=jax.ShapeDtypeStruct((b, 1, D_SB), F32),
        compiler_params=_params("parallel", "arbitrary"),
        name="sb_kmax",
    )(k_past)


def _heads(x):
    return [x[:, h * DH_SB:(h + 1) * DH_SB] for h in range(H_SB)]


def _sb_rows_attend(qh, kh, vh, t, width, ccar, acc, vis):
    z = jnp.concatenate([_dot_nt(qh[h], kh[h]) for h in range(H_SB)], axis=0)
    sp = _softplus(z)
    if vis is not None:
        sp = jnp.where(vis, sp, 0.0)
    cl = _rev_cumsum(sp, _tri(width))
    w = jnp.exp2(z - ccar - cl)
    if vis is not None:
        w = jnp.where(vis, w, 0.0)
    w = w.astype(BF16)
    pv = jnp.concatenate([_dot(w[h * t:(h + 1) * t], vh[h]) for h in range(H_SB)], axis=0)
    return ccar + cl[:, 0:1], acc + pv


def _sb_rows_bound(qh, kmax2):
    return jnp.concatenate(
        [jnp.sqrt(jnp.sum(jnp.square(qh[h].astype(F32)), axis=1, keepdims=True)
                  * kmax2[:, h * DH_SB:h * DH_SB + 1]) for h in range(H_SB)], axis=0) * SB_BOUND_SLACK


def _sb_rows_alive(bz, ccar):
    return (jnp.max(bz - ccar) > -SB_DEAD).astype(jnp.int32)


def _sb_rows_store(o_ref, acc, t):
    for h in range(H_SB):
        o_ref[0, :, h * DH_SB:(h + 1) * DH_SB] = acc[h * t:(h + 1) * t, :].astype(BF16)


def _sb_decode_head_kernel(q_ref, kn_ref, vn_ref, kt_ref, vt_ref, kmax_ref,
                           o_ref, c_ref, acc_ref, alive_ref, *, t, tk):
    rows = H_SB * t
    qh = _heads(q_ref[0])
    vis = lax.broadcasted_iota(jnp.int32, (rows, t), 1) < _row_query_index(H_SB, t, t)
    ccar, acc = _sb_rows_attend(qh, _heads(kn_ref[0]), _heads(vn_ref[0]), t, t,
                                jnp.zeros((rows, 1), F32), jnp.zeros((rows, DH_SB), F32), vis)
    ccar, acc = _sb_rows_attend(qh, _heads(kt_ref[0].astype(BF16)), _heads(vt_ref[0].astype(BF16)),
                                t, tk, ccar, acc, None)
    _sb_rows_store(o_ref, acc, t)
    c_ref[0] = ccar
    acc_ref[0] = acc
    alive_ref[0] = jnp.zeros((1, LANE), jnp.int32) + _sb_rows_alive(_sb_rows_bound(qh, kmax_ref[0]), ccar)


def _sb_decode_rest_kernel(q_ref, kmax_ref, c_ref, acc_ref, kp_hbm, vp_hbm, o_ref, kbuf, vbuf, sem,
                           *, t, tk, nk):
    b = pl.program_id(0)
    qh = _heads(q_ref[0])
    bz = _sb_rows_bound(qh, kmax_ref[0])

    def body(c):
        j, _, ccar, acc = c
        start = pl.multiple_of(j * tk, tk)
        ck = pltpu.make_async_copy(kp_hbm.at[b, pl.ds(start, tk), :], kbuf, sem.at[0])
        cv = pltpu.make_async_copy(vp_hbm.at[b, pl.ds(start, tk), :], vbuf, sem.at[1])
        ck.start()
        cv.start()
        ck.wait()
        cv.wait()
        ccar, acc = _sb_rows_attend(qh, _heads(kbuf[...].astype(BF16)), _heads(vbuf[...].astype(BF16)),
                                    t, tk, ccar, acc, None)
        return j - 1, _sb_rows_alive(bz, ccar), ccar, acc

    ccar = c_ref[0]
    out = lax.while_loop(lambda c: jnp.logical_and(c[0] >= 0, c[1] > 0), body,
                         (jnp.int32(nk - 2), _sb_rows_alive(bz, ccar), ccar, acc_ref[0]))
    _sb_rows_store(o_ref, out[3], t)


def _sb_decode(q, k_past, v_past, kn, vn, tk):
    b, t, _ = q.shape
    past_len = k_past.shape[1]
    nk = past_len // tk
    rows = H_SB * t
    k_flat = k_past.reshape(b, past_len, D_SB).astype(BF16)
    kmax2 = _sb_kmax(k_flat, min(512, past_len))
    k_tail = k_past[:, past_len - tk:].reshape(b, tk, D_SB)
    v_tail = v_past[:, past_len - tk:].reshape(b, tk, D_SB)
    tokb = lambda: pl.BlockSpec((1, t, D_SB), lambda i: (i, 0, 0))
    tail = lambda: pl.BlockSpec((1, tk, D_SB), lambda i: (i, 0, 0))
    kmx = lambda: pl.BlockSpec((1, 1, D_SB), lambda i: (i, 0, 0))
    car = lambda: pl.BlockSpec((1, rows, 1), lambda i: (i, 0, 0))
    accs = lambda: pl.BlockSpec((1, rows, DH_SB), lambda i: (i, 0, 0))
    o, ccar, acc, alive = pl.pallas_call(
        functools.partial(_sb_decode_head_kernel, t=t, tk=tk),
        grid=(b,),
        in_specs=[tokb(), tokb(), tokb(), tail(), tail(), kmx()],
        out_specs=[tokb(), car(), accs(), pl.BlockSpec((1, 1, LANE), lambda i: (i, 0, 0))],
        out_shape=[jax.ShapeDtypeStruct((b, t, D_SB), BF16), jax.ShapeDtypeStruct((b, rows, 1), F32),
                   jax.ShapeDtypeStruct((b, rows, DH_SB), F32),
                   jax.ShapeDtypeStruct((b, 1, LANE), jnp.int32)],
        compiler_params=_params("parallel"),
        name="sb_decode_head",
    )(q, kn, vn, k_tail, v_tail, kmax2)
    if nk < 2:
        return o

    def rest(args):
        q_, kmax2_, ccar_, acc_, k_flat_, v_past_, _ = args
        return pl.pallas_call(
            functools.partial(_sb_decode_rest_kernel, t=t, tk=tk, nk=nk),
            grid=(b,),
            in_specs=[tokb(), kmx(), car(), accs(), pl.BlockSpec(memory_space=pl.ANY),
                      pl.BlockSpec(memory_space=pl.ANY)],
            out_specs=tokb(),
            out_shape=jax.ShapeDtypeStruct((b, t, D_SB), BF16),
            scratch_shapes=[pltpu.VMEM((tk, D_SB), BF16), pltpu.VMEM((tk, D_SB), F32),
                            pltpu.SemaphoreType.DMA((2,))],
            compiler_params=_params("arbitrary"),
            name="sb_decode_rest",
        )(q_, kmax2_, ccar_, acc_, k_flat_, v_past_.reshape(b, past_len, D_SB))

    return lax.cond(jnp.max(alive) > 0, rest, lambda args: args[-1],
                    (q, kmax2, ccar, acc, k_flat, v_past, o))


def _post_mix_kernel(*refs, n_in):
    x_ref = refs[0]
    a_refs = refs[1:1 + n_in]
    w_ref, g_ref, wq_ref, mk_ref, mv_ref, wo_ref, o_ref = refs[1 + n_in:]
    a = a_refs[0][0] if n_in == 1 else jnp.concatenate([r[0] for r in a_refs], axis=1)
    x = x_ref[0] + _dot(a, w_ref[...])
    hq = _rms(x, g_ref[...], EPS).astype(BF16)
    q = (_dot(hq, wq_ref[...]) * (DH_MEM ** -0.5)).astype(BF16)
    hs = [slice(h * DH_MEM, (h + 1) * DH_MEM) for h in range(H_MEM)]
    ss = [_dot_nt(q[:, hs[h]], mk_ref[0, :, hs[h]]) for h in range(H_MEM)]
    ps = [jnp.exp(s - jnp.max(s, axis=1, keepdims=True)) for s in ss]
    ls = [jnp.sum(p, axis=1, keepdims=True) for p in ps]
    oh = [(_dot(ps[h].astype(BF16), mv_ref[0, :, hs[h]]) / ls[h]).astype(BF16) for h in range(H_MEM)]
    o_ref[0] = x + _dot(jnp.concatenate(oh, axis=1), wo_ref[...])


def _post_mix(x, acts, w, g, wq, mk, mv, wo, tm):
    b, t, d = x.shape
    n_in = len(acts)
    tok = lambda w: pl.BlockSpec((1, tm, w), lambda i, j: (i, j, 0))
    mem = pl.BlockSpec((1, N_MEM, D_MODEL), lambda i, j: (i, 0, 0))
    return pl.pallas_call(
        functools.partial(_post_mix_kernel, n_in=n_in),
        grid=(b, t // tm),
        in_specs=([tok(d)] + [tok(a.shape[-1]) for a in acts]
                  + [_full(w.shape), _full((1, d)), _full((d, d)), mem, mem, _full((d, d))]),
        out_specs=tok(d),
        out_shape=jax.ShapeDtypeStruct((b, t, d), F32),
        compiler_params=_params("parallel", "parallel"),
        name="post_mix",
    )(x, *acts, w, g, wq, mk, mv, wo)


def _ffn_kernel(x_ref, g_ref, wu_ref, wd_ref, gf_ref, o_ref, *, fc, final):
    x = x_ref[...]
    h = _rms(x, g_ref[...], EPS).astype(BF16)
    y = x
    for c in range(D_FF // fc):
        u = jnp.maximum(_dot(h, wu_ref[:, c * fc:(c + 1) * fc]), 0.0)
        y = y + _dot((u * u).astype(BF16), wd_ref[c * fc:(c + 1) * fc, :])
    if final:
        y = _rms(y, gf_ref[...], EPS)
    o_ref[...] = y


def _ffn(x2d, g, wu, wd, gf, final, tm, fc):
    m, d = x2d.shape
    return pl.pallas_call(
        functools.partial(_ffn_kernel, fc=fc, final=final),
        grid=(m // tm,),
        in_specs=[pl.BlockSpec((tm, d), lambda i: (i, 0)), _full((1, d)), _full((d, D_FF)),
                  _full((D_FF, d)), _full((1, d))],
        out_specs=pl.BlockSpec((tm, d), lambda i: (i, 0)),
        out_shape=jax.ShapeDtypeStruct((m, d), F32),
        compiler_params=_params("parallel"),
        name="ffn",
    )(x2d, g, wu, wd, gf)


def _rope_tables(pos):
    half = DH_QK // 2
    inv = jnp.power(ROPE_THETA, -jnp.arange(half, dtype=F32) * (2.0 / DH_QK))
    ang = pos.astype(F32)[:, None] * inv[None, :]
    cos, sin = jnp.cos(ang), jnp.sin(ang)
    reps = LANE // DH_QK
    return (jnp.tile(jnp.concatenate([cos, cos], axis=1), (1, reps)),
            jnp.tile(jnp.concatenate([-sin, sin], axis=1), (1, reps)))


def _row(v):
    return v.reshape(1, -1)


def _run(x, mem_kb, mem_vb, past, p, tm, tq):
    b, t, d = x.shape
    past_len = 0 if past is None else past[0].shape[2]
    pos = jnp.arange(past_len, past_len + t)
    cos, sin = _rope_tables(pos)
    outs = {}
    for i in range(DEPTH):
        if i % 2 == 0:
            e = i // 2
            lam_init = 0.8 - 0.6 * math.exp(-0.3 * i)
            state = jnp.zeros((b, CONV_W - 1, D_CONV), F32) if past is None else past[2][e]
            q, k, kb, v, vb, gc, nc = _even_in(x, _row(p['norm_mix'][i]), p['w_in_even_b'][e], cos, sin,
                                               p['conv_w'][e], state, tm)
            lam_args = (_row(p['lambda_q1'][e]), _row(p['lambda_k1'][e]), _row(p['lambda_q2'][e]),
                        _row(p['lambda_k2'][e]), _row(p['subln_gain'][e]))
            if past is None:
                o = _diff_prompt(q, kb, vb, *lam_args, lam_init, min(4 * tq, t), min(1024, t))
            else:
                o = _diff_decode(q, past[0][e].reshape(b, past_len, Q_DIFF).astype(BF16),
                                 past[1][e].reshape(b, past_len, D_DIFF).astype(BF16), kb, vb,
                                 *lam_args, lam_init,
                                 min(512, past_len))
            outs['dk'] = k.reshape(1, b, t, 2 * H_DIFF, DH_QK)
            outs['dv'] = v.reshape(1, b, t, H_DIFF, DV_DIFF)
            outs['conv'] = nc[None]
            acts = [o, gc]
            w_out = p['w_out_even_b'][e]
        else:
            o_idx = i // 2
            segs = ((0, D_SB, False, True, DH_SB ** -0.5 * LOG2E), (D_SB, D_SB, True, True, 1.0),
                    (2 * D_SB, D_SB, True, True, 1.0))
            q, k, kb, v, vb = _norm_proj(x, p['norm_mix'][i].reshape(1, 1, d),
                                         p['w_in_odd_b'][o_idx][None], segs, tm)
            if past is None:
                o = _sb_prompt(q, kb, vb, tq)
            else:
                o = _sb_decode(q, past[3][o_idx], past[4][o_idx], kb, vb, min(256, past_len))
            outs['sk'] = k.reshape(1, b, t, H_SB, DH_SB)
            outs['sv'] = v.reshape(1, b, t, H_SB, DH_SB)
            acts = [o]
            w_out = p['w_out_odd_b'][o_idx]
        x = _post_mix(x, acts, w_out, _row(p['norm_cross'][i]), p['w_q_mem_b'][i], mem_kb[i], mem_vb[i],
                      p['w_o_mem_b'][i], tm)
        m = b * t
        x = _ffn(x.reshape(m, d), _row(p['norm_ffn'][i]), p['w_ffn_up_b'][i], p['w_ffn_down_b'][i],
                 _row(p['norm_final']), i == DEPTH - 1, min(512, m), 1024).reshape(b, t, d)
    return x, outs


def kernel(x_prompt, x_sample, cache_diff_k, cache_diff_v, state_conv, cache_sb_k, cache_sb_v, cache_mem_k, cache_mem_v, mem_prompt, w_in_even, w_out_even, lambda_q1, lambda_k1, lambda_q2, lambda_k2, subln_gain, conv_w, w_in_odd, w_out_odd, norm_mix, norm_mem, norm_cross, w_q_mem, w_k_mem, w_v_mem, w_o_mem, norm_ffn, w_ffn_up, w_ffn_down, norm_final):
    p = dict(norm_mix=norm_mix, norm_cross=norm_cross, norm_ffn=norm_ffn, norm_final=norm_final,
             lambda_q1=lambda_q1, lambda_k1=lambda_k1, lambda_q2=lambda_q2, lambda_k2=lambda_k2,
             subln_gain=subln_gain, conv_w=conv_w)
    for name, w in (('w_in_even', w_in_even), ('w_out_even', w_out_even), ('w_in_odd', w_in_odd),
                    ('w_out_odd', w_out_odd), ('w_q_mem', w_q_mem), ('w_o_mem', w_o_mem),
                    ('w_ffn_up', w_ffn_up), ('w_ffn_down', w_ffn_down)):
        p[name + '_b'] = w.astype(BF16)

    bp = mem_prompt.shape[0]
    w_kv = jnp.concatenate([w_k_mem, w_v_mem], axis=-1).astype(BF16)
    segs = ((0, D_MODEL, True, True, 1.0), (D_MODEL, D_MODEL, True, True, 1.0))
    mk, mkb, mv, mvb = _norm_proj(mem_prompt, norm_mem.reshape(DEPTH, 1, D_MODEL), w_kv, segs, N_MEM)
    p_mem_k = mk.reshape(DEPTH, bp, N_MEM, H_MEM, DH_MEM)
    p_mem_v = mv.reshape(DEPTH, bp, N_MEM, H_MEM, DH_MEM)
    mkb = mkb.reshape(DEPTH, bp, N_MEM, D_MODEL)
    mvb = mvb.reshape(DEPTH, bp, N_MEM, D_MODEL)

    t_p = x_prompt.shape[1]
    y_prompt, po = _run(x_prompt, mkb, mvb, None, p, min(512, t_p), min(256, t_p))

    bs = x_sample.shape[0]
    t_s = x_sample.shape[1]
    cmk = cache_mem_k.reshape(DEPTH, bs, N_MEM, D_MODEL).astype(BF16)
    cmv = cache_mem_v.reshape(DEPTH, bs, N_MEM, D_MODEL).astype(BF16)
    y_sample, so = _run(x_sample, cmk, cmv,
                        (cache_diff_k, cache_diff_v, state_conv, cache_sb_k, cache_sb_v), p, t_s, t_s)

    return (y_prompt, y_sample, po['dk'], po['dv'], po['conv'], po['sk'], po['sv'], p_mem_k, p_mem_v,
            so['dk'], so['dv'], so['conv'], so['sk'], so['sv'])
```

```python
import functools
import math

import jax
import jax.numpy as jnp
from jax import lax
from jax.experimental import pallas as pl
from jax.experimental.pallas import tpu as pltpu

F32 = jnp.float32
BF16 = jnp.bfloat16

D_MODEL = 1024
DEPTH = 2
CHUNK = 64
H_DIFF = 4
DH_QK = 64
DV_DIFF = 2 * DH_QK
D_DIFF = H_DIFF * DV_DIFF
Q_DIFF = 2 * H_DIFF * DH_QK
D_CONV = D_MODEL - D_DIFF
CONV_W = 3
EVEN_IN = 2 * Q_DIFF + D_DIFF + 3 * D_CONV
H_SB = 16
DH_SB = 64
D_SB = H_SB * DH_SB
N_MEM = 256
H_MEM = 4
DH_MEM = D_MODEL // H_MEM
D_FF = 4 * D_MODEL
ROPE_THETA = 10000.0
EPS = 1e-6
SUBLN_EPS = 1e-5
NEG_INF = -1e30
LOG2E = math.log2(math.e)
SB_DEAD = 105.0 * LOG2E
SB_BOUND_SLACK = 1.01

LANE = 128
SUBLANE = 8
VMEM_LIMIT = 52 * 1024 * 1024


def _params(*sem):
    return pltpu.CompilerParams(dimension_semantics=sem, vmem_limit_bytes=VMEM_LIMIT)


def _rms(x, g, eps):
    ms = jnp.mean(x * x, axis=-1, keepdims=True)
    return x * lax.rsqrt(ms + eps) * g


def _dot(a, b):
    return jnp.dot(a, b, preferred_element_type=F32)


def _dot_nt(a, b):
    return lax.dot_general(a, b, (((1,), (1,)), ((), ())), preferred_element_type=F32)


def _full(shape):
    n = len(shape)
    return pl.BlockSpec(shape, lambda *_: (0,) * n)


def _chunk_of(pos):
    shift = CHUNK.bit_length() - 1
    assert 1 << shift == CHUNK
    return lax.shift_right_logical(pos, shift)


def _row_query_index(groups, t, width):
    r = lax.broadcasted_iota(jnp.int32, (t, width), 0)
    return jnp.concatenate([r] * groups, axis=0)


def _lane_band(lane, s, width):
    return (lane >= s * width) & (lane < (s + 1) * width)


def _even_in_kernel(x_ref, g_ref, w_ref, cos_ref, sin_ref, cw_ref, st_ref,
                    q_ref, k_ref, kb_ref, v_ref, vb_ref, gc_ref, nc_ref, ext_ref, *, tm, nt):
    t = pl.program_id(1)
    h = _rms(x_ref[0], g_ref[...], EPS).astype(BF16)

    def proj(lo, width):
        return _dot(h, w_ref[:, lo:lo + width])

    cos = cos_ref[...]
    sin = sin_ref[...]
    lane = lax.broadcasted_iota(jnp.int32, (tm, LANE), 1)
    first_half = (lane & (DH_QK // 2)) == 0

    def rope_group(yj):
        sw = jnp.where(first_half, pltpu.roll(yj, LANE - DH_QK // 2, 1), pltpu.roll(yj, DH_QK // 2, 1))
        return yj * cos + sw * sin

    yq = proj(0, Q_DIFF)
    yk = proj(Q_DIFF, Q_DIFF)
    for j in range(Q_DIFF // LANE):
        sl = slice(j * LANE, (j + 1) * LANE)
        q_ref[0, :, sl] = (rope_group(yq[:, sl]) * (DH_QK ** -0.5 * LOG2E)).astype(BF16)
        kj = rope_group(yk[:, sl])
        k_ref[0, :, sl] = kj
        kb_ref[0, :, sl] = kj.astype(BF16)
    yv = proj(2 * Q_DIFF, D_DIFF)
    v_ref[0] = yv
    vb_ref[0] = yv.astype(BF16)

    base = 2 * Q_DIFF + D_DIFF
    gate_b = proj(base, D_CONV)
    cu = proj(base + D_CONV, D_CONV) * proj(base + 2 * D_CONV, D_CONV)

    @pl.when(t == 0)
    def _():
        ext_ref[SUBLANE - 2:SUBLANE, :] = st_ref[0]

    ext_ref[SUBLANE:SUBLANE + tm, :] = cu
    cw = cw_ref[...]
    conv = (ext_ref[SUBLANE - 2:SUBLANE - 2 + tm, :] * cw[0:1, :]
            + ext_ref[SUBLANE - 1:SUBLANE - 1 + tm, :] * cw[1:2, :]
            + cu * cw[2:3, :])
    gc_ref[0] = (gate_b * conv).astype(BF16)
    ext_ref[0:SUBLANE, :] = ext_ref[tm:tm + SUBLANE, :]

    @pl.when(t == nt - 1)
    def _():
        nc_ref[0] = ext_ref[tm + SUBLANE - 2:tm + SUBLANE, :]


def _even_in(x, g, wb, cos, sin, cw, state, tm):
    b, t, _ = x.shape
    nt = t // tm
    tok = lambda w: pl.BlockSpec((1, tm, w), lambda i, j: (i, j, 0))
    outs = [
        jax.ShapeDtypeStruct((b, t, Q_DIFF), BF16),
        jax.ShapeDtypeStruct((b, t, Q_DIFF), F32),
        jax.ShapeDtypeStruct((b, t, Q_DIFF), BF16),
        jax.ShapeDtypeStruct((b, t, D_DIFF), F32),
        jax.ShapeDtypeStruct((b, t, D_DIFF), BF16),
        jax.ShapeDtypeStruct((b, t, D_CONV), BF16),
        jax.ShapeDtypeStruct((b, CONV_W - 1, D_CONV), F32),
    ]
    return pl.pallas_call(
        functools.partial(_even_in_kernel, tm=tm, nt=nt),
        grid=(b, nt),
        in_specs=[
            tok(D_MODEL),
            _full((1, D_MODEL)),
            _full((D_MODEL, EVEN_IN)),
            pl.BlockSpec((tm, LANE), lambda i, j: (j, 0)),
            pl.BlockSpec((tm, LANE), lambda i, j: (j, 0)),
            _full((CONV_W, D_CONV)),
            pl.BlockSpec((1, CONV_W - 1, D_CONV), lambda i, j: (i, 0, 0)),
        ],
        out_specs=[tok(Q_DIFF), tok(Q_DIFF), tok(Q_DIFF), tok(D_DIFF), tok(D_DIFF), tok(D_CONV),
                   pl.BlockSpec((1, CONV_W - 1, D_CONV), lambda i, j: (i, 0, 0))],
        out_shape=outs,
        scratch_shapes=[pltpu.VMEM((tm + 2 * SUBLANE, D_CONV), F32)],
        compiler_params=_params("parallel", "arbitrary"),
        name="even_in",
    )(x, g, wb, cos, sin, cw, state)


def _norm_proj_kernel(x_ref, g_ref, w_ref, *out_refs, segs):
    h = _rms(x_ref[0], g_ref[0], EPS).astype(BF16)
    i = 0
    for lo, width, want_f32, want_bf16, scale in segs:
        y = _dot(h, w_ref[0, :, lo:lo + width])
        if scale != 1.0:
            y = y * scale
        if want_f32:
            out_refs[i][0] = y
            i += 1
        if want_bf16:
            out_refs[i][0] = y.astype(BF16)
            i += 1


def _norm_proj(x, g, wb, segs, tm):
    b, t, d = x.shape
    n = wb.shape[-1]
    ng = wb.shape[0]
    nt = t // tm
    outs, specs = [], []
    for lo, width, want_f32, want_bf16, _ in segs:
        for want, dt in ((want_f32, F32), (want_bf16, BF16)):
            if want:
                outs.append(jax.ShapeDtypeStruct((ng * b, t, width), dt))
                specs.append(pl.BlockSpec((1, tm, width), lambda i, j: (i, j, 0)))
    return pl.pallas_call(
        functools.partial(_norm_proj_kernel, segs=segs),
        grid=(ng * b, nt),
        in_specs=[
            pl.BlockSpec((1, tm, d), lambda i, j: (i % b, j, 0)),
            pl.BlockSpec((1, 1, d), lambda i, j: (i // b, 0, 0)),
            pl.BlockSpec((1, d, n), lambda i, j: (i // b, 0, 0)),
        ],
        out_specs=specs,
        out_shape=outs,
        compiler_params=_params("parallel", "parallel"),
        name="norm_proj",
    )(x, g, wb)


def _lambda_full(lq1_ref, lk1_ref, lq2_ref, lk2_ref, lam_init):
    s1 = jnp.sum(lq1_ref[...] * lk1_ref[...], axis=1, keepdims=True)
    s2 = jnp.sum(lq2_ref[...] * lk2_ref[...], axis=1, keepdims=True)
    return jnp.exp(s1) - jnp.exp(s2) + lam_init


def _subln(o, gsub_ref, lam_init):
    return _rms(o, gsub_ref[...], SUBLN_EPS) * (1.0 - lam_init)


def _diff_prompt_kernel(q_ref, k_ref, v_ref, lq1_ref, lk1_ref, lq2_ref, lk2_ref, gsub_ref,
                        o_ref, *, tq, tk, lam_init):
    qi = pl.program_id(2)
    ratio = tk // tq
    q = q_ref[0]
    lane = lax.broadcasted_iota(jnp.int32, (tq, LANE), 1)
    zero = jnp.zeros_like(q)
    qm = (jnp.where(lane < DH_QK, q, zero), jnp.where(lane >= DH_QK, q, zero))
    ones = jnp.ones((tk, LANE), BF16)

    def step(j, carry, masked):
        start = pl.multiple_of(j * tk, tk)
        kb = k_ref[0, pl.ds(start, tk), :]
        vb = jnp.concatenate([v_ref[0, pl.ds(start, tk), :], ones], axis=1)
        if masked:
            r = _chunk_of(lax.broadcasted_iota(jnp.int32, (tq, tk), 0)) + lax.rem(qi, ratio) * (tq // CHUNK)
            c = _chunk_of(lax.broadcasted_iota(jnp.int32, (tq, tk), 1))
            vis = c <= r
        ss = [_dot_nt(qm[mi], kb) for mi in range(2)]
        if masked:
            ss = [jnp.where(vis, s, NEG_INF) for s in ss]
        ms = [jnp.maximum(carry[mi][0], jnp.max(ss[mi], axis=1, keepdims=True)) for mi in range(2)]
        ps = [jnp.exp2(ss[mi] - ms[mi]).astype(BF16) for mi in range(2)]
        pv = [_dot(ps[mi], vb) for mi in range(2)]
        return tuple((ms[mi], jnp.exp2(carry[mi][0] - ms[mi]) * carry[mi][1] + pv[mi]) for mi in range(2))

    init = tuple((jnp.full((tq, 1), NEG_INF, F32), jnp.zeros((tq, 2 * LANE), F32)) for _ in range(2))
    n_full = lax.div(qi, ratio)
    carry = lax.fori_loop(0, n_full, lambda j, c: step(j, c, False), init)
    carry = step(n_full, carry, True)
    lam = _lambda_full(lq1_ref, lk1_ref, lq2_ref, lk2_ref, lam_init)
    (_, a1), (_, a2) = carry
    o = a1[:, :LANE] / a1[:, LANE:] - lam * (a2[:, :LANE] / a2[:, LANE:])
    o_ref[0] = _subln(o, gsub_ref, lam_init).astype(BF16)


def _diff_prompt(q, kb, vb, lq1, lk1, lq2, lk2, gsub, lam_init, tq, tk):
    b, t, _ = q.shape
    lam_spec = _full((1, DH_QK))
    return pl.pallas_call(
        functools.partial(_diff_prompt_kernel, tq=tq, tk=tk, lam_init=lam_init),
        grid=(b, H_DIFF, t // tq),
        in_specs=[
            pl.BlockSpec((1, tq, LANE), lambda i, h, j: (i, j, h)),
            pl.BlockSpec((1, t, LANE), lambda i, h, j: (i, 0, h)),
            pl.BlockSpec((1, t, LANE), lambda i, h, j: (i, 0, h)),
            lam_spec, lam_spec, lam_spec, lam_spec,
            _full((1, DV_DIFF)),
        ],
        out_specs=pl.BlockSpec((1, tq, LANE), lambda i, h, j: (i, j, h)),
        out_shape=jax.ShapeDtypeStruct((b, t, D_DIFF), BF16),
        compiler_params=_params("parallel", "parallel", "arbitrary"),
        name="diff_prompt",
    )(q, kb, vb, lq1, lk1, lq2, lk2, gsub)


def _diff_decode_kernel(q_ref, kp_ref, vp_ref, kn_ref, vn_ref, lq1_ref, lk1_ref, lq2_ref, lk2_ref,
                        gsub_ref, o_ref, qs_ref, m_ref, l_ref, acc_ref, *, t, tk, nk, past_len,
                        lam_init):
    j = pl.program_id(1)
    nsub = 2 * H_DIFF
    rows = nsub * t

    @pl.when(j == 0)
    def _():
        q = q_ref[0]
        lane = lax.broadcasted_iota(jnp.int32, (t, Q_DIFF), 1)
        for s in range(nsub):
            qs_ref[s * t:(s + 1) * t, :] = jnp.where(_lane_band(lane, s, DH_QK), q, jnp.zeros_like(q))
        m_ref[...] = jnp.full(m_ref.shape, NEG_INF, F32)
        l_ref[...] = jnp.zeros(l_ref.shape, F32)
        acc_ref[...] = jnp.zeros(acc_ref.shape, F32)

    def update(kb, vb, k_start, width):
        s = _dot_nt(qs_ref[...], kb)
        q_pos = past_len + _row_query_index(nsub, t, width)
        k_pos = k_start + lax.broadcasted_iota(jnp.int32, (rows, width), 1)
        s = jnp.where(_chunk_of(k_pos) <= _chunk_of(q_pos), s, NEG_INF)
        m = m_ref[...]
        m_new = jnp.maximum(m, jnp.max(s, axis=1, keepdims=True))
        alpha = jnp.exp2(m - m_new)
        p = jnp.exp2(s - m_new)
        l_ref[...] = alpha * l_ref[...] + jnp.sum(p, axis=1, keepdims=True)
        m_ref[...] = m_new
        pb = p.astype(BF16)
        for h in range(H_DIFF):
            rs = slice(2 * h * t, (2 * h + 2) * t)
            acc_ref[rs, :] = alpha[rs] * acc_ref[rs, :] + _dot(pb[rs], vb[:, h * LANE:(h + 1) * LANE])

    update(kp_ref[0].astype(BF16), vp_ref[0].astype(BF16), j * tk, tk)

    @pl.when(j == nk - 1)
    def _():
        update(kn_ref[0], vn_ref[0], past_len, t)
        lam = _lambda_full(lq1_ref, lk1_ref, lq2_ref, lk2_ref, lam_init)
        on = acc_ref[...] / l_ref[...]
        for h in range(H_DIFF):
            o = on[2 * h * t:(2 * h + 1) * t] - lam * on[(2 * h + 1) * t:(2 * h + 2) * t]
            o_ref[0, :, h * LANE:(h + 1) * LANE] = _subln(o, gsub_ref, lam_init).astype(BF16)


def _diff_decode(q, k_past, v_past, kn, vn, lq1, lk1, lq2, lk2, gsub, lam_init, tk):
    b, t, _ = q.shape
    past_len = k_past.shape[1]
    nk = past_len // tk
    rows = 2 * H_DIFF * t
    lam_spec = _full((1, DH_QK))
    tokb = lambda w: pl.BlockSpec((1, t, w), lambda i, j: (i, 0, 0))
    return pl.pallas_call(
        functools.partial(_diff_decode_kernel, t=t, tk=tk, nk=nk, past_len=past_len,
                          lam_init=lam_init),
        grid=(b, nk),
        in_specs=[
            tokb(Q_DIFF),
            pl.BlockSpec((1, tk, Q_DIFF), lambda i, j: (i, j, 0)),
            pl.BlockSpec((1, tk, D_DIFF), lambda i, j: (i, j, 0)),
            tokb(Q_DIFF), tokb(D_DIFF),
            lam_spec, lam_spec, lam_spec, lam_spec,
            _full((1, DV_DIFF)),
        ],
        out_specs=tokb(D_DIFF),
        out_shape=jax.ShapeDtypeStruct((b, t, D_DIFF), BF16),
        scratch_shapes=[pltpu.VMEM((rows, Q_DIFF), BF16), pltpu.VMEM((rows, 1), F32),
                        pltpu.VMEM((rows, 1), F32), pltpu.VMEM((rows, DV_DIFF), F32)],
        compiler_params=_params("parallel", "arbitrary"),
        name="diff_decode",
    )(q, k_past, v_past, kn, vn, lq1, lk1, lq2, lk2, gsub)


def _softplus(z):
    return jnp.maximum(z, 0.0) + jnp.log(1.0 + jnp.exp2(-jnp.abs(z))) * LOG2E


def _rev_cumsum(sp, tri):
    hi = sp.astype(BF16)
    lo = (sp - hi.astype(F32)).astype(BF16)
    return _dot(hi, tri) + _dot(lo, tri)


def _tri(n):
    r = lax.broadcasted_iota(jnp.int32, (n, n), 0)
    c = lax.broadcasted_iota(jnp.int32, (n, n), 1)
    return jnp.where(r >= c, 1.0, 0.0).astype(BF16)


def _head_ones(width):
    r = lax.broadcasted_iota(jnp.int32, (LANE, LANE), 0)
    c = lax.broadcasted_iota(jnp.int32, (LANE, LANE), 1)
    return jnp.where((r < width) == (c < width), 1.0, 0.0).astype(BF16)


def _head_sq_norms(x_bf16, ones_blk):
    xf = x_bf16.astype(F32)
    return _dot((xf * xf).astype(BF16), ones_blk)


def _sb_prompt_kernel(q_ref, k_ref, v_ref, o_ref, kmax_ref, *, tq, t, ng):
    qi = pl.program_id(2)
    lane = lax.broadcasted_iota(jnp.int32, (tq, LANE), 1)
    tri = _tri(tq)
    ones_blk = _head_ones(DH_SB)
    gs = [slice(g * LANE, (g + 1) * LANE) for g in range(ng)]

    @pl.when(qi == 0)
    def _():
        rows = math.gcd(t, 4 * tq)
        for g in range(ng):
            def body(i, mx, g=g):
                kc = k_ref[0, pl.ds(pl.multiple_of(i * rows, rows), rows), gs[g]]
                return jnp.maximum(mx, _head_sq_norms(kc, ones_blk))
            mx = lax.fori_loop(0, t // rows, body, jnp.zeros((rows, LANE), F32))
            kmax_ref[:, gs[g]] = jnp.max(mx, axis=0, keepdims=True)

    qs, bzs = [], []
    for g in range(ng):
        q = q_ref[0, :, gs[g]]
        zero = jnp.zeros_like(q)
        qs.append(jnp.concatenate([jnp.where(lane < DH_SB, q, zero), jnp.where(lane >= DH_SB, q, zero)], axis=0))
        bound = jnp.sqrt(_head_sq_norms(q, ones_blk) * kmax_ref[:, gs[g]]) * SB_BOUND_SLACK
        bzs += [bound[:, 0:1], bound[:, DH_SB:DH_SB + 1]]
    bz = jnp.concatenate(bzs, axis=0)

    def alive(ccar):
        return (jnp.max(bz - ccar) > -SB_DEAD).astype(jnp.int32)

    def blk(ref, j, g):
        return ref[0, pl.ds(pl.multiple_of(j * tq, tq), tq), gs[g]]

    def scores(j):
        return jnp.concatenate([_dot_nt(qs[g], blk(k_ref, j, g)) for g in range(ng)], axis=0)

    def weighted(w, vals):
        return jnp.concatenate([_dot(w[2 * g * tq:2 * (g + 1) * tq], vals(g)) for g in range(ng)], axis=0)

    has_prev = qi > 0
    jp = jnp.maximum(qi - 1, 0)
    z_r = scores(qi)
    z_l = scores(jp)
    vis = lax.broadcasted_iota(jnp.int32, (2 * ng * tq, tq), 1) < _row_query_index(2 * ng, tq, tq)
    sp_r = jnp.where(vis, _softplus(z_r), 0.0)
    cl_r = _rev_cumsum(sp_r, tri)
    total_r = cl_r[:, 0:1]
    cl_l = _rev_cumsum(_softplus(z_l), tri) + jnp.where(has_prev, total_r, -NEG_INF)
    w_r = jnp.where(vis, jnp.exp2(z_r - cl_r), 0.0).astype(BF16)
    w_l = jnp.exp2(z_l - cl_l).astype(BF16)
    acc = weighted(jnp.concatenate([w_l, w_r], axis=1),
                   lambda g: jnp.concatenate([blk(v_ref, jp, g), blk(v_ref, qi, g)], axis=0))
    ccar = jnp.where(has_prev, cl_l[:, 0:1], total_r)

    def body(c):
        j, _, ccar, acc = c
        z = scores(j)
        cl = _rev_cumsum(_softplus(z), tri)
        w = jnp.exp2(z - ccar - cl).astype(BF16)
        ccar = ccar + cl[:, 0:1]
        return j - 1, alive(ccar), ccar, acc + weighted(w, lambda g: blk(v_ref, j, g))

    out = lax.while_loop(lambda c: jnp.logical_and(c[0] >= 0, c[1] > 0), body,
                         (qi - 2, alive(ccar), ccar, acc))
    acc = out[3]
    for g in range(ng):
        a = acc[2 * g * tq:2 * (g + 1) * tq]
        o_ref[0, :, gs[g]] = jnp.where(lane < DH_SB, a[:tq], a[tq:]).astype(BF16)


def _sb_prompt(q, kb, vb, tq, ng):
    b, t, _ = q.shape
    wg = ng * LANE
    return pl.pallas_call(
        functools.partial(_sb_prompt_kernel, tq=tq, t=t, ng=ng),
        scratch_shapes=[pltpu.VMEM((1, wg), F32)],
        grid=(b, D_SB // wg, t // tq),
        in_specs=[
            pl.BlockSpec((1, tq, wg), lambda i, h, j: (i, j, h)),
            pl.BlockSpec((1, t, wg), lambda i, h, j: (i, 0, h)),
            pl.BlockSpec((1, t, wg), lambda i, h, j: (i, 0, h)),
        ],
        out_specs=pl.BlockSpec((1, tq, wg), lambda i, h, j: (i, j, h)),
        out_shape=jax.ShapeDtypeStruct((b, t, D_SB), BF16),
        compiler_params=_params("parallel", "parallel", "arbitrary"),
        name="sb_prompt",
    )(q, kb, vb)


def _sb_kmax_kernel(k_ref, o_ref):
    j = pl.program_id(1)
    x = k_ref[0]
    ones_blk = _head_ones(DH_SB)
    blk = jnp.concatenate(
        [jnp.max(_head_sq_norms(x[:, g * LANE:(g + 1) * LANE], ones_blk), axis=0, keepdims=True)
         for g in range(D_SB // LANE)], axis=1)

    @pl.when(j == 0)
    def _():
        o_ref[0] = blk

    @pl.when(j > 0)
    def _():
        o_ref[0] = jnp.maximum(o_ref[0], blk)


def _sb_kmax(k_past, tk):
    b, past_len, _ = k_past.shape
    return pl.pallas_call(
        _sb_kmax_kernel,
        grid=(b, past_len // tk),
        in_specs=[pl.BlockSpec((1, tk, D_SB), lambda i, j: (i, j, 0))],
        out_specs=pl.BlockSpec((1, 1, D_SB), lambda i, j: (i, 0, 0)),
        out_shape=jax.ShapeDtypeStruct((b, 1, D_SB), F32),
        compiler_params=_params("parallel", "arbitrary"),
        name="sb_kmax",
    )(k_past)


def _heads(x):
    return [x[:, h * DH_SB:(h + 1) * DH_SB] for h in range(H_SB)]


def _sb_rows_attend(qh, kh, vh, t, width, ccar, acc, vis):
    z = jnp.concatenate([_dot_nt(qh[h], kh[h]) for h in range(H_SB)], axis=0)
    sp = _softplus(z)
    if vis is not None:
        sp = jnp.where(vis, sp, 0.0)
    cl = _rev_cumsum(sp, _tri(width))
    w = jnp.exp2(z - ccar - cl)
    if vis is not None:
        w = jnp.where(vis, w, 0.0)
    w = w.astype(BF16)
    pv = jnp.concatenate([_dot(w[h * t:(h + 1) * t], vh[h]) for h in range(H_SB)], axis=0)
    return ccar + cl[:, 0:1], acc + pv


def _sb_rows_bound(qh, kmax2):
    return jnp.concatenate(
        [jnp.sqrt(jnp.sum(jnp.square(qh[h].astype(F32)), axis=1, keepdims=True)
                  * kmax2[:, h * DH_SB:h * DH_SB + 1]) for h in range(H_SB)], axis=0) * SB_BOUND_SLACK


def _sb_rows_alive(bz, ccar):
    return (jnp.max(bz - ccar) > -SB_DEAD).astype(jnp.int32)


def _sb_rows_store(o_ref, acc, t):
    for h in range(H_SB):
        o_ref[0, :, h * DH_SB:(h + 1) * DH_SB] = acc[h * t:(h + 1) * t, :].astype(BF16)


def _sb_decode_head_kernel(q_ref, kn_ref, vn_ref, kt_ref, vt_ref, kmax_ref,
                           o_ref, c_ref, acc_ref, alive_ref, *, t, tk):
    rows = H_SB * t
    qh = _heads(q_ref[0])
    vis = lax.broadcasted_iota(jnp.int32, (rows, t), 1) < _row_query_index(H_SB, t, t)
    ccar, acc = _sb_rows_attend(qh, _heads(kn_ref[0]), _heads(vn_ref[0]), t, t,
                                jnp.zeros((rows, 1), F32), jnp.zeros((rows, DH_SB), F32), vis)
    ccar, acc = _sb_rows_attend(qh, _heads(kt_ref[0].astype(BF16)), _heads(vt_ref[0].astype(BF16)),
                                t, tk, ccar, acc, None)
    _sb_rows_store(o_ref, acc, t)
    c_ref[0] = ccar
    acc_ref[0] = acc
    alive_ref[0] = jnp.zeros((1, LANE), jnp.int32) + _sb_rows_alive(_sb_rows_bound(qh, kmax_ref[0]), ccar)


def _sb_decode_rest_kernel(q_ref, kmax_ref, c_ref, acc_ref, kp_hbm, vp_hbm, o_ref, kbuf, vbuf, sem,
                           *, t, tk, nk):
    b = pl.program_id(0)
    qh = _heads(q_ref[0])
    bz = _sb_rows_bound(qh, kmax_ref[0])

    def body(c):
        j, _, ccar, acc = c
        start = pl.multiple_of(j * tk, tk)
        ck = pltpu.make_async_copy(kp_hbm.at[b, pl.ds(start, tk), :], kbuf, sem.at[0])
        cv = pltpu.make_async_copy(vp_hbm.at[b, pl.ds(start, tk), :], vbuf, sem.at[1])
        ck.start()
        cv.start()
        ck.wait()
        cv.wait()
        ccar, acc = _sb_rows_attend(qh, _heads(kbuf[...].astype(BF16)), _heads(vbuf[...].astype(BF16)),
                                    t, tk, ccar, acc, None)
        return j - 1, _sb_rows_alive(bz, ccar), ccar, acc

    ccar = c_ref[0]
    out = lax.while_loop(lambda c: jnp.logical_and(c[0] >= 0, c[1] > 0), body,
                         (jnp.int32(nk - 2), _sb_rows_alive(bz, ccar), ccar, acc_ref[0]))
    _sb_rows_store(o_ref, out[3], t)


def _sb_decode(q, k_past, v_past, kn, vn, tk):
    b, t, _ = q.shape
    past_len = k_past.shape[1]
    nk = past_len // tk
    rows = H_SB * t
    k_flat = k_past.reshape(b, past_len, D_SB)
    kmax2 = _sb_kmax(k_flat, min(512, past_len))
    k_tail = k_past[:, past_len - tk:].reshape(b, tk, D_SB)
    v_tail = v_past[:, past_len - tk:].reshape(b, tk, D_SB)
    tokb = lambda: pl.BlockSpec((1, t, D_SB), lambda i: (i, 0, 0))
    tail = lambda: pl.BlockSpec((1, tk, D_SB), lambda i: (i, 0, 0))
    kmx = lambda: pl.BlockSpec((1, 1, D_SB), lambda i: (i, 0, 0))
    car = lambda: pl.BlockSpec((1, rows, 1), lambda i: (i, 0, 0))
    accs = lambda: pl.BlockSpec((1, rows, DH_SB), lambda i: (i, 0, 0))
    o, ccar, acc, alive = pl.pallas_call(
        functools.partial(_sb_decode_head_kernel, t=t, tk=tk),
        grid=(b,),
        in_specs=[tokb(), tokb(), tokb(), tail(), tail(), kmx()],
        out_specs=[tokb(), car(), accs(), pl.BlockSpec((1, 1, LANE), lambda i: (i, 0, 0))],
        out_shape=[jax.ShapeDtypeStruct((b, t, D_SB), BF16), jax.ShapeDtypeStruct((b, rows, 1), F32),
                   jax.ShapeDtypeStruct((b, rows, DH_SB), F32),
                   jax.ShapeDtypeStruct((b, 1, LANE), jnp.int32)],
        compiler_params=_params("parallel"),
        name="sb_decode_head",
    )(q, kn, vn, k_tail, v_tail, kmax2)
    if nk < 2:
        return o

    def rest(args):
        q_, kmax2_, ccar_, acc_, k_flat_, v_past_, _ = args
        return pl.pallas_call(
            functools.partial(_sb_decode_rest_kernel, t=t, tk=tk, nk=nk),
            grid=(b,),
            in_specs=[tokb(), kmx(), car(), accs(), pl.BlockSpec(memory_space=pl.ANY),
                      pl.BlockSpec(memory_space=pl.ANY)],
            out_specs=tokb(),
            out_shape=jax.ShapeDtypeStruct((b, t, D_SB), BF16),
            scratch_shapes=[pltpu.VMEM((tk, D_SB), F32), pltpu.VMEM((tk, D_SB), F32),
                            pltpu.SemaphoreType.DMA((2,))],
            compiler_params=_params("arbitrary"),
            name="sb_decode_rest",
        )(q_, kmax2_, ccar_, acc_, k_flat_, v_past_.reshape(b, past_len, D_SB))

    return lax.cond(jnp.max(alive) > 0, rest, lambda args: args[-1],
                    (q, kmax2, ccar, acc, k_flat, v_past, o))


def _post_mix_kernel(*refs, n_in):
    x_ref = refs[0]
    a_refs = refs[1:1 + n_in]
    w_ref, g_ref, wq_ref, mk_ref, mv_ref, wo_ref, o_ref = refs[1 + n_in:]
    a = a_refs[0][0] if n_in == 1 else jnp.concatenate([r[0] for r in a_refs], axis=1)
    x = x_ref[0] + _dot(a, w_ref[...])
    hq = _rms(x, g_ref[...], EPS).astype(BF16)
    q = (_dot(hq, wq_ref[...]) * (DH_MEM ** -0.5 * LOG2E)).astype(BF16)
    hs = [slice(h * DH_MEM, (h + 1) * DH_MEM) for h in range(H_MEM)]
    ss = [_dot_nt(q[:, hs[h]], mk_ref[0, :, hs[h]]) for h in range(H_MEM)]
    ps = [jnp.exp2(s - jnp.max(s, axis=1, keepdims=True)) for s in ss]
    ls = [jnp.sum(p, axis=1, keepdims=True) for p in ps]
    oh = [(_dot(ps[h].astype(BF16), mv_ref[0, :, hs[h]]) / ls[h]).astype(BF16) for h in range(H_MEM)]
    o_ref[0] = x + _dot(jnp.concatenate(oh, axis=1), wo_ref[...])


def _post_mix(x, acts, w, g, wq, mk, mv, wo, tm):
    b, t, d = x.shape
    n_in = len(acts)
    tok = lambda w: pl.BlockSpec((1, tm, w), lambda i, j: (i, j, 0))
    mem = pl.BlockSpec((1, N_MEM, D_MODEL), lambda i, j: (i, 0, 0))
    return pl.pallas_call(
        functools.partial(_post_mix_kernel, n_in=n_in),
        grid=(b, t // tm),
        in_specs=([tok(d)] + [tok(a.shape[-1]) for a in acts]
                  + [_full(w.shape), _full((1, d)), _full((d, d)), mem, mem, _full((d, d))]),
        out_specs=tok(d),
        out_shape=jax.ShapeDtypeStruct((b, t, d), F32),
        compiler_params=_params("parallel", "parallel"),
        name="post_mix",
    )(x, *acts, w, g, wq, mk, mv, wo)


def _ffn_kernel(x_ref, g_ref, wu_ref, wd_ref, gf_ref, o_ref, *, fc, final):
    x = x_ref[...]
    h = _rms(x, g_ref[...], EPS).astype(BF16)
    y = x
    for c in range(D_FF // fc):
        u = jnp.maximum(_dot(h, wu_ref[:, c * fc:(c + 1) * fc]), 0.0)
        y = y + _dot((u * u).astype(BF16), wd_ref[c * fc:(c + 1) * fc, :])
    if final:
        y = _rms(y, gf_ref[...], EPS)
    o_ref[...] = y


def _ffn(x2d, g, wu, wd, gf, final, tm, fc):
    m, d = x2d.shape
    return pl.pallas_call(
        functools.partial(_ffn_kernel, fc=fc, final=final),
        grid=(m // tm,),
        in_specs=[pl.BlockSpec((tm, d), lambda i: (i, 0)), _full((1, d)), _full((d, D_FF)),
                  _full((D_FF, d)), _full((1, d))],
        out_specs=pl.BlockSpec((tm, d), lambda i: (i, 0)),
        out_shape=jax.ShapeDtypeStruct((m, d), F32),
        compiler_params=_params("parallel"),
        name="ffn",
    )(x2d, g, wu, wd, gf)


def _rope_tables(pos):
    half = DH_QK // 2
    inv = jnp.power(ROPE_THETA, -jnp.arange(half, dtype=F32) * (2.0 / DH_QK))
    ang = pos.astype(F32)[:, None] * inv[None, :]
    cos, sin = jnp.cos(ang), jnp.sin(ang)
    reps = LANE // DH_QK
    return (jnp.tile(jnp.concatenate([cos, cos], axis=1), (1, reps)),
            jnp.tile(jnp.concatenate([-sin, sin], axis=1), (1, reps)))


def _row(v):
    return v.reshape(1, -1)


def _run(x, mem_kb, mem_vb, past, p, tm, tq):
    b, t, d = x.shape
    past_len = 0 if past is None else past[0].shape[2]
    pos = jnp.arange(past_len, past_len + t)
    cos, sin = _rope_tables(pos)
    outs = {}
    for i in range(DEPTH):
        if i % 2 == 0:
            e = i // 2
            lam_init = 0.8 - 0.6 * math.exp(-0.3 * i)
            state = jnp.zeros((b, CONV_W - 1, D_CONV), F32) if past is None else past[2][e]
            q, k, kb, v, vb, gc, nc = _even_in(x, _row(p['norm_mix'][i]), p['w_in_even_b'][e], cos, sin,
                                               p['conv_w'][e], state, tm)
            lam_args = (_row(p['lambda_q1'][e]), _row(p['lambda_k1'][e]), _row(p['lambda_q2'][e]),
                        _row(p['lambda_k2'][e]), _row(p['subln_gain'][e]))
            if past is None:
                o = _diff_prompt(q, kb, vb, *lam_args, lam_init, min(4 * tq, t), min(1024, t))
            else:
                o = _diff_decode(q, past[0][e].reshape(b, past_len, Q_DIFF),
                                 past[1][e].reshape(b, past_len, D_DIFF), kb, vb, *lam_args, lam_init,
                                 min(512, past_len))
            outs['dk'] = k.reshape(1, b, t, 2 * H_DIFF, DH_QK)
            outs['dv'] = v.reshape(1, b, t, H_DIFF, DV_DIFF)
            outs['conv'] = nc[None]
            acts = [o, gc]
            w_out = p['w_out_even_b'][e]
        else:
            o_idx = i // 2
            segs = ((0, D_SB, False, True, DH_SB ** -0.5 * LOG2E), (D_SB, D_SB, True, True, 1.0),
                    (2 * D_SB, D_SB, True, True, 1.0))
            q, k, kb, v, vb = _norm_proj(x, p['norm_mix'][i].reshape(1, 1, d),
                                         p['w_in_odd_b'][o_idx][None], segs, tm)
            if past is None:
                o = _sb_prompt(q, kb, vb, tq, 2)
            else:
                o = _sb_decode(q, past[3][o_idx], past[4][o_idx], kb, vb, min(256, past_len))
            outs['sk'] = k.reshape(1, b, t, H_SB, DH_SB)
            outs['sv'] = v.reshape(1, b, t, H_SB, DH_SB)
            acts = [o]
            w_out = p['w_out_odd_b'][o_idx]
        x = _post_mix(x, acts, w_out, _row(p['norm_cross'][i]), p['w_q_mem_b'][i], mem_kb[i], mem_vb[i],
                      p['w_o_mem_b'][i], tm)
        m = b * t
        x = _ffn(x.reshape(m, d), _row(p['norm_ffn'][i]), p['w_ffn_up_b'][i], p['w_ffn_down_b'][i],
                 _row(p['norm_final']), i == DEPTH - 1, min(512, m), 1024).reshape(b, t, d)
    return x, outs


def kernel(x_prompt, x_sample, cache_diff_k, cache_diff_v, state_conv, cache_sb_k, cache_sb_v, cache_mem_k, cache_mem_v, mem_prompt, w_in_even, w_out_even, lambda_q1, lambda_k1, lambda_q2, lambda_k2, subln_gain, conv_w, w_in_odd, w_out_odd, norm_mix, norm_mem, norm_cross, w_q_mem, w_k_mem, w_v_mem, w_o_mem, norm_ffn, w_ffn_up, w_ffn_down, norm_final):
    p = dict(norm_mix=norm_mix, norm_cross=norm_cross, norm_ffn=norm_ffn, norm_final=norm_final,
             lambda_q1=lambda_q1, lambda_k1=lambda_k1, lambda_q2=lambda_q2, lambda_k2=lambda_k2,
             subln_gain=subln_gain, conv_w=conv_w)
    for name, w in (('w_in_even', w_in_even), ('w_out_even', w_out_even), ('w_in_odd', w_in_odd),
                    ('w_out_odd', w_out_odd), ('w_q_mem', w_q_mem), ('w_o_mem', w_o_mem),
                    ('w_ffn_up', w_ffn_up), ('w_ffn_down', w_ffn_down)):
        p[name + '_b'] = w.astype(BF16)

    bp = mem_prompt.shape[0]
    w_kv = jnp.concatenate([w_k_mem, w_v_mem], axis=-1).astype(BF16)
    segs = ((0, D_MODEL, True, True, 1.0), (D_MODEL, D_MODEL, True, True, 1.0))
    mk, mkb, mv, mvb = _norm_proj(mem_prompt, norm_mem.reshape(DEPTH, 1, D_MODEL), w_kv, segs, N_MEM)
    p_mem_k = mk.reshape(DEPTH, bp, N_MEM, H_MEM, DH_MEM)
    p_mem_v = mv.reshape(DEPTH, bp, N_MEM, H_MEM, DH_MEM)
    mkb = mkb.reshape(DEPTH, bp, N_MEM, D_MODEL)
    mvb = mvb.reshape(DEPTH, bp, N_MEM, D_MODEL)

    t_p = x_prompt.shape[1]
    y_prompt, po = _run(x_prompt, mkb, mvb, None, p, min(512, t_p), min(256, t_p))

    bs = x_sample.shape[0]
    t_s = x_sample.shape[1]
    cmk = cache_mem_k.reshape(DEPTH, bs, N_MEM, D_MODEL).astype(BF16)
    cmv = cache_mem_v.reshape(DEPTH, bs, N_MEM, D_MODEL).astype(BF16)
    y_sample, so = _run(x_sample, cmk, cmv,
                        (cache_diff_k, cache_diff_v, state_conv, cache_sb_k, cache_sb_v), p, t_s, t_s)

    return (y_prompt, y_sample, po['dk'], po['dv'], po['conv'], po['sk'], po['sv'], p_mem_k, p_mem_v,
            so['dk'], so['dv'], so['conv'], so['sk'], so['sv'])
```

```python
import functools
import math

import jax
import jax.numpy as jnp
from jax import lax
from jax.experimental import pallas as pl
from jax.experimental.pallas import tpu as pltpu

F32 = jnp.float32
BF16 = jnp.bfloat16

D_MODEL = 1024
DEPTH = 2
CHUNK = 64
H_DIFF = 4
DH_QK = 64
DV_DIFF = 2 * DH_QK
D_DIFF = H_DIFF * DV_DIFF
Q_DIFF = 2 * H_DIFF * DH_QK
D_CONV = D_MODEL - D_DIFF
CONV_W = 3
EVEN_IN = 2 * Q_DIFF + D_DIFF + 3 * D_CONV
H_SB = 16
DH_SB = 64
D_SB = H_SB * DH_SB
N_MEM = 256
H_MEM = 4
DH_MEM = D_MODEL // H_MEM
D_FF = 4 * D_MODEL
ROPE_THETA = 10000.0
EPS = 1e-6
SUBLN_EPS = 1e-5
NEG_INF = -1e30
LOG2E = math.log2(math.e)
SB_DEAD = 105.0 * LOG2E
SB_BOUND_SLACK = 1.01

LANE = 128
SUBLANE = 8
VMEM_LIMIT = 52 * 1024 * 1024


def _params(*sem):
    return pltpu.CompilerParams(dimension_semantics=sem, vmem_limit_bytes=VMEM_LIMIT)


def _rms(x, g, eps):
    ms = jnp.mean(x * x, axis=-1, keepdims=True)
    return x * lax.rsqrt(ms + eps) * g


def _dot(a, b):
    return jnp.dot(a, b, preferred_element_type=F32)


def _dot_nt(a, b):
    return lax.dot_general(a, b, (((1,), (1,)), ((), ())), preferred_element_type=F32)


def _full(shape):
    n = len(shape)
    return pl.BlockSpec(shape, lambda *_: (0,) * n)


def _chunk_of(pos):
    shift = CHUNK.bit_length() - 1
    assert 1 << shift == CHUNK
    return lax.shift_right_logical(pos, shift)


def _row_query_index(groups, t, width):
    r = lax.broadcasted_iota(jnp.int32, (t, width), 0)
    return jnp.concatenate([r] * groups, axis=0)


def _lane_band(lane, s, width):
    return (lane >= s * width) & (lane < (s + 1) * width)


def _even_in_kernel(x_ref, g_ref, w_ref, cos_ref, sin_ref, cw_ref, st_ref,
                    q_ref, k_ref, kb_ref, v_ref, vb_ref, gc_ref, nc_ref, ext_ref, *, tm, nt):
    t = pl.program_id(1)
    h = _rms(x_ref[0], g_ref[...], EPS).astype(BF16)

    def proj(lo, width):
        return _dot(h, w_ref[:, lo:lo + width])

    cos = cos_ref[...]
    sin = sin_ref[...]
    lane = lax.broadcasted_iota(jnp.int32, (tm, LANE), 1)
    first_half = (lane & (DH_QK // 2)) == 0

    def rope_group(yj):
        sw = jnp.where(first_half, pltpu.roll(yj, LANE - DH_QK // 2, 1), pltpu.roll(yj, DH_QK // 2, 1))
        return yj * cos + sw * sin

    yq = proj(0, Q_DIFF)
    yk = proj(Q_DIFF, Q_DIFF)
    for j in range(Q_DIFF // LANE):
        sl = slice(j * LANE, (j + 1) * LANE)
        q_ref[0, :, sl] = (rope_group(yq[:, sl]) * (DH_QK ** -0.5 * LOG2E)).astype(BF16)
        kj = rope_group(yk[:, sl])
        k_ref[0, :, sl] = kj
        kb_ref[0, :, sl] = kj.astype(BF16)
    yv = proj(2 * Q_DIFF, D_DIFF)
    for hh in range(H_DIFF):
        v_ref[0, pl.ds(hh, tm, stride=H_DIFF), :] = yv[:, hh * DV_DIFF:(hh + 1) * DV_DIFF]
    vb_ref[0] = yv.astype(BF16)

    base = 2 * Q_DIFF + D_DIFF
    gate_b = proj(base, D_CONV)
    cu = proj(base + D_CONV, D_CONV) * proj(base + 2 * D_CONV, D_CONV)

    @pl.when(t == 0)
    def _():
        ext_ref[SUBLANE - 2:SUBLANE, :] = st_ref[0]

    ext_ref[SUBLANE:SUBLANE + tm, :] = cu
    cw = cw_ref[...]
    conv = (ext_ref[SUBLANE - 2:SUBLANE - 2 + tm, :] * cw[0:1, :]
            + ext_ref[SUBLANE - 1:SUBLANE - 1 + tm, :] * cw[1:2, :]
            + cu * cw[2:3, :])
    gc_ref[0] = (gate_b * conv).astype(BF16)
    ext_ref[0:SUBLANE, :] = ext_ref[tm:tm + SUBLANE, :]

    @pl.when(t == nt - 1)
    def _():
        nc_ref[0] = ext_ref[tm + SUBLANE - 2:tm + SUBLANE, :]


def _even_in(x, g, wb, cos, sin, cw, state, tm):
    b, t, _ = x.shape
    nt = t // tm
    tok = lambda w: pl.BlockSpec((1, tm, w), lambda i, j: (i, j, 0))
    outs = [
        jax.ShapeDtypeStruct((b, t, Q_DIFF), BF16),
        jax.ShapeDtypeStruct((b, t, Q_DIFF), F32),
        jax.ShapeDtypeStruct((b, t, Q_DIFF), BF16),
        jax.ShapeDtypeStruct((b, t * H_DIFF, DV_DIFF), F32),
        jax.ShapeDtypeStruct((b, t, D_DIFF), BF16),
        jax.ShapeDtypeStruct((b, t, D_CONV), BF16),
        jax.ShapeDtypeStruct((b, CONV_W - 1, D_CONV), F32),
    ]
    return pl.pallas_call(
        functools.partial(_even_in_kernel, tm=tm, nt=nt),
        grid=(b, nt),
        in_specs=[
            tok(D_MODEL),
            _full((1, D_MODEL)),
            _full((D_MODEL, EVEN_IN)),
            pl.BlockSpec((tm, LANE), lambda i, j: (j, 0)),
            pl.BlockSpec((tm, LANE), lambda i, j: (j, 0)),
            _full((CONV_W, D_CONV)),
            pl.BlockSpec((1, CONV_W - 1, D_CONV), lambda i, j: (i, 0, 0)),
        ],
        out_specs=[tok(Q_DIFF), tok(Q_DIFF), tok(Q_DIFF),
                   pl.BlockSpec((1, tm * H_DIFF, DV_DIFF), lambda i, j: (i, j, 0)), tok(D_DIFF), tok(D_CONV),
                   pl.BlockSpec((1, CONV_W - 1, D_CONV), lambda i, j: (i, 0, 0))],
        out_shape=outs,
        scratch_shapes=[pltpu.VMEM((tm + 2 * SUBLANE, D_CONV), F32)],
        compiler_params=_params("parallel", "arbitrary"),
        name="even_in",
    )(x, g, wb, cos, sin, cw, state)


def _norm_proj_kernel(x_ref, g_ref, w_ref, *out_refs, segs):
    h = _rms(x_ref[0], g_ref[0], EPS).astype(BF16)
    i = 0
    for lo, width, want_f32, want_bf16, scale in segs:
        y = _dot(h, w_ref[0, :, lo:lo + width])
        if scale != 1.0:
            y = y * scale
        if want_f32:
            out_refs[i][0] = y
            i += 1
        if want_bf16:
            out_refs[i][0] = y.astype(BF16)
            i += 1


def _norm_proj(x, g, wb, segs, tm):
    b, t, d = x.shape
    n = wb.shape[-1]
    ng = wb.shape[0]
    nt = t // tm
    outs, specs = [], []
    for lo, width, want_f32, want_bf16, _ in segs:
        for want, dt in ((want_f32, F32), (want_bf16, BF16)):
            if want:
                outs.append(jax.ShapeDtypeStruct((ng * b, t, width), dt))
                specs.append(pl.BlockSpec((1, tm, width), lambda i, j: (i, j, 0)))
    return pl.pallas_call(
        functools.partial(_norm_proj_kernel, segs=segs),
        grid=(ng * b, nt),
        in_specs=[
            pl.BlockSpec((1, tm, d), lambda i, j: (i % b, j, 0)),
            pl.BlockSpec((1, 1, d), lambda i, j: (i // b, 0, 0)),
            pl.BlockSpec((1, d, n), lambda i, j: (i // b, 0, 0)),
        ],
        out_specs=specs,
        out_shape=outs,
        compiler_params=_params("parallel", "parallel"),
        name="norm_proj",
    )(x, g, wb)


def _lambda_full(lq1_ref, lk1_ref, lq2_ref, lk2_ref, lam_init):
    s1 = jnp.sum(lq1_ref[...] * lk1_ref[...], axis=1, keepdims=True)
    s2 = jnp.sum(lq2_ref[...] * lk2_ref[...], axis=1, keepdims=True)
    return jnp.exp(s1) - jnp.exp(s2) + lam_init


def _subln(o, gsub_ref, lam_init):
    return _rms(o, gsub_ref[...], SUBLN_EPS) * (1.0 - lam_init)


def _diff_prompt_kernel(q_ref, k_ref, v_ref, lq1_ref, lk1_ref, lq2_ref, lk2_ref, gsub_ref,
                        o_ref, *, tq, tk, lam_init):
    qi = pl.program_id(2)
    ratio = tk // tq
    q = q_ref[0]
    lane = lax.broadcasted_iota(jnp.int32, (tq, LANE), 1)
    zero = jnp.zeros_like(q)
    qm = (jnp.where(lane < DH_QK, q, zero), jnp.where(lane >= DH_QK, q, zero))
    ones = jnp.ones((tk, LANE), BF16)

    def step(j, carry, masked):
        start = pl.multiple_of(j * tk, tk)
        kb = k_ref[0, pl.ds(start, tk), :]
        vb = jnp.concatenate([v_ref[0, pl.ds(start, tk), :], ones], axis=1)
        if masked:
            r = _chunk_of(lax.broadcasted_iota(jnp.int32, (tq, tk), 0)) + lax.rem(qi, ratio) * (tq // CHUNK)
            c = _chunk_of(lax.broadcasted_iota(jnp.int32, (tq, tk), 1))
            vis = c <= r
        ss = [_dot_nt(qm[mi], kb) for mi in range(2)]
        if masked:
            ss = [jnp.where(vis, s, NEG_INF) for s in ss]
        ms = [jnp.maximum(carry[mi][0], jnp.max(ss[mi], axis=1, keepdims=True)) for mi in range(2)]
        ps = [jnp.exp2(ss[mi] - ms[mi]).astype(BF16) for mi in range(2)]
        pv = [_dot(ps[mi], vb) for mi in range(2)]
        return tuple((ms[mi], jnp.exp2(carry[mi][0] - ms[mi]) * carry[mi][1] + pv[mi]) for mi in range(2))

    init = tuple((jnp.full((tq, 1), NEG_INF, F32), jnp.zeros((tq, 2 * LANE), F32)) for _ in range(2))
    n_full = lax.div(qi, ratio)
    carry = lax.fori_loop(0, n_full, lambda j, c: step(j, c, False), init)
    carry = step(n_full, carry, True)
    lam = _lambda_full(lq1_ref, lk1_ref, lq2_ref, lk2_ref, lam_init)
    (_, a1), (_, a2) = carry
    o = a1[:, :LANE] / a1[:, LANE:] - lam * (a2[:, :LANE] / a2[:, LANE:])
    o_ref[0] = _subln(o, gsub_ref, lam_init).astype(BF16)


def _diff_prompt(q, kb, vb, lq1, lk1, lq2, lk2, gsub, lam_init, tq, tk):
    b, t, _ = q.shape
    lam_spec = _full((1, DH_QK))
    return pl.pallas_call(
        functools.partial(_diff_prompt_kernel, tq=tq, tk=tk, lam_init=lam_init),
        grid=(b, H_DIFF, t // tq),
        in_specs=[
            pl.BlockSpec((1, tq, LANE), lambda i, h, j: (i, j, h)),
            pl.BlockSpec((1, t, LANE), lambda i, h, j: (i, 0, h)),
            pl.BlockSpec((1, t, LANE), lambda i, h, j: (i, 0, h)),
            lam_spec, lam_spec, lam_spec, lam_spec,
            _full((1, DV_DIFF)),
        ],
        out_specs=pl.BlockSpec((1, tq, LANE), lambda i, h, j: (i, j, h)),
        out_shape=jax.ShapeDtypeStruct((b, t, D_DIFF), BF16),
        compiler_params=_params("parallel", "parallel", "arbitrary"),
        name="diff_prompt",
    )(q, kb, vb, lq1, lk1, lq2, lk2, gsub)


def _diff_decode_kernel(q_ref, kp_ref, vp_ref, kn_ref, vn_ref, lq1_ref, lk1_ref, lq2_ref, lk2_ref,
                        gsub_ref, o_ref, qs_ref, m_ref, l_ref, acc_ref, *, t, tk, nk, past_len,
                        lam_init):
    j = pl.program_id(1)
    nsub = 2 * H_DIFF
    rows = nsub * t

    @pl.when(j == 0)
    def _():
        q = q_ref[0]
        lane = lax.broadcasted_iota(jnp.int32, (t, Q_DIFF), 1)
        for s in range(nsub):
            qs_ref[s * t:(s + 1) * t, :] = jnp.where(_lane_band(lane, s, DH_QK), q, jnp.zeros_like(q))
        m_ref[...] = jnp.full(m_ref.shape, NEG_INF, F32)
        l_ref[...] = jnp.zeros(l_ref.shape, F32)
        acc_ref[...] = jnp.zeros(acc_ref.shape, F32)

    def update(kb, vh, k_start, width):
        s = _dot_nt(qs_ref[...], kb)
        q_pos = past_len + _row_query_index(nsub, t, width)
        k_pos = k_start + lax.broadcasted_iota(jnp.int32, (rows, width), 1)
        s = jnp.where(_chunk_of(k_pos) <= _chunk_of(q_pos), s, NEG_INF)
        m = m_ref[...]
        m_new = jnp.maximum(m, jnp.max(s, axis=1, keepdims=True))
        alpha = jnp.exp2(m - m_new)
        p = jnp.exp2(s - m_new)
        l_ref[...] = alpha * l_ref[...] + jnp.sum(p, axis=1, keepdims=True)
        m_ref[...] = m_new
        pb = p.astype(BF16)
        for h in range(H_DIFF):
            rs = slice(2 * h * t, (2 * h + 2) * t)
            acc_ref[rs, :] = alpha[rs] * acc_ref[rs, :] + _dot(pb[rs], vh[h])

    update(kp_ref[0].astype(BF16),
           [vp_ref[0, pl.ds(h, tk, stride=H_DIFF), :].astype(BF16) for h in range(H_DIFF)], j * tk, tk)

    @pl.when(j == nk - 1)
    def _():
        vn = vn_ref[0]
        update(kn_ref[0], [vn[:, h * DV_DIFF:(h + 1) * DV_DIFF] for h in range(H_DIFF)], past_len, t)
        lam = _lambda_full(lq1_ref, lk1_ref, lq2_ref, lk2_ref, lam_init)
        on = acc_ref[...] / l_ref[...]
        for h in range(H_DIFF):
            o = on[2 * h * t:(2 * h + 1) * t] - lam * on[(2 * h + 1) * t:(2 * h + 2) * t]
            o_ref[0, :, h * LANE:(h + 1) * LANE] = _subln(o, gsub_ref, lam_init).astype(BF16)


def _diff_decode(q, k_past, v_past, kn, vn, lq1, lk1, lq2, lk2, gsub, lam_init, tk):
    b, t, _ = q.shape
    past_len = k_past.shape[1]
    nk = past_len // tk
    rows = 2 * H_DIFF * t
    lam_spec = _full((1, DH_QK))
    tokb = lambda w: pl.BlockSpec((1, t, w), lambda i, j: (i, 0, 0))
    return pl.pallas_call(
        functools.partial(_diff_decode_kernel, t=t, tk=tk, nk=nk, past_len=past_len,
                          lam_init=lam_init),
        grid=(b, nk),
        in_specs=[
            tokb(Q_DIFF),
            pl.BlockSpec((1, tk, Q_DIFF), lambda i, j: (i, j, 0)),
            pl.BlockSpec((1, tk * H_DIFF, DV_DIFF), lambda i, j: (i, j, 0)),
            tokb(Q_DIFF), tokb(D_DIFF),
            lam_spec, lam_spec, lam_spec, lam_spec,
            _full((1, DV_DIFF)),
        ],
        out_specs=tokb(D_DIFF),
        out_shape=jax.ShapeDtypeStruct((b, t, D_DIFF), BF16),
        scratch_shapes=[pltpu.VMEM((rows, Q_DIFF), BF16), pltpu.VMEM((rows, 1), F32),
                        pltpu.VMEM((rows, 1), F32), pltpu.VMEM((rows, DV_DIFF), F32)],
        compiler_params=_params("parallel", "arbitrary"),
        name="diff_decode",
    )(q, k_past, v_past, kn, vn, lq1, lk1, lq2, lk2, gsub)


def _softplus(z):
    return jnp.maximum(z, 0.0) + jnp.log(1.0 + jnp.exp2(-jnp.abs(z))) * LOG2E


def _rev_cumsum(sp, tri):
    hi = sp.astype(BF16)
    lo = (sp - hi.astype(F32)).astype(BF16)
    return _dot(hi, tri) + _dot(lo, tri)


def _tri(n):
    r = lax.broadcasted_iota(jnp.int32, (n, n), 0)
    c = lax.broadcasted_iota(jnp.int32, (n, n), 1)
    return jnp.where(r >= c, 1.0, 0.0).astype(BF16)


def _head_ones(width):
    r = lax.broadcasted_iota(jnp.int32, (LANE, LANE), 0)
    c = lax.broadcasted_iota(jnp.int32, (LANE, LANE), 1)
    return jnp.where((r < width) == (c < width), 1.0, 0.0).astype(BF16)


def _head_sq_norms(x_bf16, ones_blk):
    xf = x_bf16.astype(F32)
    return _dot((xf * xf).astype(BF16), ones_blk)


def _sb_prompt_kernel(q_ref, k_ref, v_ref, o_ref, kmax_ref, *, tq, t, ng):
    qi = pl.program_id(2)
    lane = lax.broadcasted_iota(jnp.int32, (tq, LANE), 1)
    tri = _tri(tq)
    ones_blk = _head_ones(DH_SB)
    gs = [slice(g * LANE, (g + 1) * LANE) for g in range(ng)]

    @pl.when(qi == 0)
    def _():
        rows = math.gcd(t, 4 * tq)
        for g in range(ng):
            def body(i, mx, g=g):
                kc = k_ref[0, pl.ds(pl.multiple_of(i * rows, rows), rows), gs[g]]
                return jnp.maximum(mx, _head_sq_norms(kc, ones_blk))
            mx = lax.fori_loop(0, t // rows, body, jnp.zeros((rows, LANE), F32))
            kmax_ref[:, gs[g]] = jnp.max(mx, axis=0, keepdims=True)

    qs, bzs = [], []
    for g in range(ng):
        q = q_ref[0, :, gs[g]]
        zero = jnp.zeros_like(q)
        qs.append(jnp.concatenate([jnp.where(lane < DH_SB, q, zero), jnp.where(lane >= DH_SB, q, zero)], axis=0))
        bound = jnp.sqrt(_head_sq_norms(q, ones_blk) * kmax_ref[:, gs[g]]) * SB_BOUND_SLACK
        bzs += [bound[:, 0:1], bound[:, DH_SB:DH_SB + 1]]
    bz = jnp.concatenate(bzs, axis=0)

    def alive(ccar):
        return (jnp.max(bz - ccar) > -SB_DEAD).astype(jnp.int32)

    def blk(ref, j, g):
        return ref[0, pl.ds(pl.multiple_of(j * tq, tq), tq), gs[g]]

    def scores(j):
        return jnp.concatenate([_dot_nt(qs[g], blk(k_ref, j, g)) for g in range(ng)], axis=0)

    def weighted(w, vals):
        return jnp.concatenate([_dot(w[2 * g * tq:2 * (g + 1) * tq], vals(g)) for g in range(ng)], axis=0)

    has_prev = qi > 0
    jp = jnp.maximum(qi - 1, 0)
    z_r = scores(qi)
    z_l = scores(jp)
    vis = lax.broadcasted_iota(jnp.int32, (2 * ng * tq, tq), 1) < _row_query_index(2 * ng, tq, tq)
    sp_r = jnp.where(vis, _softplus(z_r), 0.0)
    cl_r = _rev_cumsum(sp_r, tri)
    total_r = cl_r[:, 0:1]
    cl_l = _rev_cumsum(_softplus(z_l), tri) + jnp.where(has_prev, total_r, -NEG_INF)
    w_r = jnp.where(vis, jnp.exp2(z_r - cl_r), 0.0).astype(BF16)
    w_l = jnp.exp2(z_l - cl_l).astype(BF16)
    acc = weighted(jnp.concatenate([w_l, w_r], axis=1),
                   lambda g: jnp.concatenate([blk(v_ref, jp, g), blk(v_ref, qi, g)], axis=0))
    ccar = jnp.where(has_prev, cl_l[:, 0:1], total_r)

    def body(c):
        j, _, ccar, acc = c
        z = scores(j)
        cl = _rev_cumsum(_softplus(z), tri)
        w = jnp.exp2(z - ccar - cl).astype(BF16)
        ccar = ccar + cl[:, 0:1]
        return j - 1, alive(ccar), ccar, acc + weighted(w, lambda g: blk(v_ref, j, g))

    out = lax.while_loop(lambda c: jnp.logical_and(c[0] >= 0, c[1] > 0), body,
                         (qi - 2, alive(ccar), ccar, acc))
    acc = out[3]
    for g in range(ng):
        a = acc[2 * g * tq:2 * (g + 1) * tq]
        o_ref[0, :, gs[g]] = jnp.where(lane < DH_SB, a[:tq], a[tq:]).astype(BF16)


def _sb_prompt(q, kb, vb, tq, ng):
    b, t, _ = q.shape
    wg = ng * LANE
    return pl.pallas_call(
        functools.partial(_sb_prompt_kernel, tq=tq, t=t, ng=ng),
        scratch_shapes=[pltpu.VMEM((1, wg), F32)],
        grid=(b, D_SB // wg, t // tq),
        in_specs=[
            pl.BlockSpec((1, tq, wg), lambda i, h, j: (i, j, h)),
            pl.BlockSpec((1, t, wg), lambda i, h, j: (i, 0, h)),
            pl.BlockSpec((1, t, wg), lambda i, h, j: (i, 0, h)),
        ],
        out_specs=pl.BlockSpec((1, tq, wg), lambda i, h, j: (i, j, h)),
        out_shape=jax.ShapeDtypeStruct((b, t, D_SB), BF16),
        compiler_params=_params("parallel", "parallel", "arbitrary"),
        name="sb_prompt",
    )(q, kb, vb)


def _sb_kmax_kernel(k_ref, o_ref):
    j = pl.program_id(1)
    x = k_ref[0]
    ones_blk = _head_ones(DH_SB)
    blk = jnp.concatenate(
        [jnp.max(_head_sq_norms(x[:, g * LANE:(g + 1) * LANE], ones_blk), axis=0, keepdims=True)
         for g in range(D_SB // LANE)], axis=1)

    @pl.when(j == 0)
    def _():
        o_ref[0] = blk

    @pl.when(j > 0)
    def _():
        o_ref[0] = jnp.maximum(o_ref[0], blk)


def _sb_kmax(k_past, tk):
    b, past_len, _ = k_past.shape
    return pl.pallas_call(
        _sb_kmax_kernel,
        grid=(b, past_len // tk),
        in_specs=[pl.BlockSpec((1, tk, D_SB), lambda i, j: (i, j, 0))],
        out_specs=pl.BlockSpec((1, 1, D_SB), lambda i, j: (i, 0, 0)),
        out_shape=jax.ShapeDtypeStruct((b, 1, D_SB), F32),
        compiler_params=_params("parallel", "arbitrary"),
        name="sb_kmax",
    )(k_past)


def _heads(x):
    return [x[:, h * DH_SB:(h + 1) * DH_SB] for h in range(H_SB)]


def _sb_rows_attend(qh, kh, vh, t, width, ccar, acc, vis):
    z = jnp.concatenate([_dot_nt(qh[h], kh[h]) for h in range(H_SB)], axis=0)
    sp = _softplus(z)
    if vis is not None:
        sp = jnp.where(vis, sp, 0.0)
    cl = _rev_cumsum(sp, _tri(width))
    w = jnp.exp2(z - ccar - cl)
    if vis is not None:
        w = jnp.where(vis, w, 0.0)
    w = w.astype(BF16)
    pv = jnp.concatenate([_dot(w[h * t:(h + 1) * t], vh[h]) for h in range(H_SB)], axis=0)
    return ccar + cl[:, 0:1], acc + pv


def _sb_rows_bound(qh, kmax2):
    return jnp.concatenate(
        [jnp.sqrt(jnp.sum(jnp.square(qh[h].astype(F32)), axis=1, keepdims=True)
                  * kmax2[:, h * DH_SB:h * DH_SB + 1]) for h in range(H_SB)], axis=0) * SB_BOUND_SLACK


def _sb_rows_alive(bz, ccar):
    return (jnp.max(bz - ccar) > -SB_DEAD).astype(jnp.int32)


def _sb_rows_store(o_ref, acc, t):
    for h in range(H_SB):
        o_ref[0, :, h * DH_SB:(h + 1) * DH_SB] = acc[h * t:(h + 1) * t, :].astype(BF16)


def _sb_decode_head_kernel(q_ref, kn_ref, vn_ref, kt_ref, vt_ref, kmax_ref,
                           o_ref, c_ref, acc_ref, alive_ref, *, t, tk):
    rows = H_SB * t
    qh = _heads(q_ref[0])
    vis = lax.broadcasted_iota(jnp.int32, (rows, t), 1) < _row_query_index(H_SB, t, t)
    ccar, acc = _sb_rows_attend(qh, _heads(kn_ref[0]), _heads(vn_ref[0]), t, t,
                                jnp.zeros((rows, 1), F32), jnp.zeros((rows, DH_SB), F32), vis)
    ccar, acc = _sb_rows_attend(qh, _heads(kt_ref[0].astype(BF16)), _heads(vt_ref[0].astype(BF16)),
                                t, tk, ccar, acc, None)
    _sb_rows_store(o_ref, acc, t)
    c_ref[0] = ccar
    acc_ref[0] = acc
    alive_ref[0] = jnp.zeros((1, LANE), jnp.int32) + _sb_rows_alive(_sb_rows_bound(qh, kmax_ref[0]), ccar)


def _sb_decode_rest_kernel(q_ref, kmax_ref, c_ref, acc_ref, kp_hbm, vp_hbm, o_ref, kbuf, vbuf, sem,
                           *, t, tk, nk):
    b = pl.program_id(0)
    qh = _heads(q_ref[0])
    bz = _sb_rows_bound(qh, kmax_ref[0])

    def body(c):
        j, _, ccar, acc = c
        start = pl.multiple_of(j * tk, tk)
        ck = pltpu.make_async_copy(kp_hbm.at[b, pl.ds(start, tk), :], kbuf, sem.at[0])
        cv = pltpu.make_async_copy(vp_hbm.at[b, pl.ds(start, tk), :], vbuf, sem.at[1])
        ck.start()
        cv.start()
        ck.wait()
        cv.wait()
        ccar, acc = _sb_rows_attend(qh, _heads(kbuf[...].astype(BF16)), _heads(vbuf[...].astype(BF16)),
                                    t, tk, ccar, acc, None)
        return j - 1, _sb_rows_alive(bz, ccar), ccar, acc

    ccar = c_ref[0]
    out = lax.while_loop(lambda c: jnp.logical_and(c[0] >= 0, c[1] > 0), body,
                         (jnp.int32(nk - 2), _sb_rows_alive(bz, ccar), ccar, acc_ref[0]))
    _sb_rows_store(o_ref, out[3], t)


def _sb_decode(q, k_past, v_past, kn, vn, tk):
    b, t, _ = q.shape
    past_len = k_past.shape[1]
    nk = past_len // tk
    rows = H_SB * t
    k_flat = k_past.reshape(b, past_len, D_SB)
    kmax2 = _sb_kmax(k_flat, min(512, past_len))
    k_tail = k_past[:, past_len - tk:].reshape(b, tk, D_SB)
    v_tail = v_past[:, past_len - tk:].reshape(b, tk, D_SB)
    tokb = lambda: pl.BlockSpec((1, t, D_SB), lambda i: (i, 0, 0))
    tail = lambda: pl.BlockSpec((1, tk, D_SB), lambda i: (i, 0, 0))
    kmx = lambda: pl.BlockSpec((1, 1, D_SB), lambda i: (i, 0, 0))
    car = lambda: pl.BlockSpec((1, rows, 1), lambda i: (i, 0, 0))
    accs = lambda: pl.BlockSpec((1, rows, DH_SB), lambda i: (i, 0, 0))
    o, ccar, acc, alive = pl.pallas_call(
        functools.partial(_sb_decode_head_kernel, t=t, tk=tk),
        grid=(b,),
        in_specs=[tokb(), tokb(), tokb(), tail(), tail(), kmx()],
        out_specs=[tokb(), car(), accs(), pl.BlockSpec((1, 1, LANE), lambda i: (i, 0, 0))],
        out_shape=[jax.ShapeDtypeStruct((b, t, D_SB), BF16), jax.ShapeDtypeStruct((b, rows, 1), F32),
                   jax.ShapeDtypeStruct((b, rows, DH_SB), F32),
                   jax.ShapeDtypeStruct((b, 1, LANE), jnp.int32)],
        compiler_params=_params("parallel"),
        name="sb_decode_head",
    )(q, kn, vn, k_tail, v_tail, kmax2)
    if nk < 2:
        return o

    def rest(args):
        q_, kmax2_, ccar_, acc_, k_flat_, v_past_, _ = args
        return pl.pallas_call(
            functools.partial(_sb_decode_rest_kernel, t=t, tk=tk, nk=nk),
            grid=(b,),
            in_specs=[tokb(), kmx(), car(), accs(), pl.BlockSpec(memory_space=pl.ANY),
                      pl.BlockSpec(memory_space=pl.ANY)],
            out_specs=tokb(),
            out_shape=jax.ShapeDtypeStruct((b, t, D_SB), BF16),
            scratch_shapes=[pltpu.VMEM((tk, D_SB), F32), pltpu.VMEM((tk, D_SB), F32),
                            pltpu.SemaphoreType.DMA((2,))],
            compiler_params=_params("arbitrary"),
            name="sb_decode_rest",
        )(q_, kmax2_, ccar_, acc_, k_flat_, v_past_.reshape(b, past_len, D_SB))

    return lax.cond(jnp.max(alive) > 0, rest, lambda args: args[-1],
                    (q, kmax2, ccar, acc, k_flat, v_past, o))


def _post_mix_kernel(*refs, n_in):
    x_ref = refs[0]
    a_refs = refs[1:1 + n_in]
    w_ref, g_ref, wq_ref, mk_ref, mv_ref, wo_ref, o_ref = refs[1 + n_in:]
    a = a_refs[0][0] if n_in == 1 else jnp.concatenate([r[0] for r in a_refs], axis=1)
    x = x_ref[0] + _dot(a, w_ref[...])
    hq = _rms(x, g_ref[...], EPS).astype(BF16)
    q = (_dot(hq, wq_ref[...]) * (DH_MEM ** -0.5 * LOG2E)).astype(BF16)
    hs = [slice(h * DH_MEM, (h + 1) * DH_MEM) for h in range(H_MEM)]
    ss = [_dot_nt(q[:, hs[h]], mk_ref[0, :, hs[h]]) for h in range(H_MEM)]
    ps = [jnp.exp2(s - jnp.max(s, axis=1, keepdims=True)) for s in ss]
    ls = [jnp.sum(p, axis=1, keepdims=True) for p in ps]
    oh = [(_dot(ps[h].astype(BF16), mv_ref[0, :, hs[h]]) / ls[h]).astype(BF16) for h in range(H_MEM)]
    o_ref[0] = x + _dot(jnp.concatenate(oh, axis=1), wo_ref[...])


def _post_mix(x, acts, w, g, wq, mk, mv, wo, tm):
    b, t, d = x.shape
    n_in = len(acts)
    tok = lambda w: pl.BlockSpec((1, tm, w), lambda i, j: (i, j, 0))
    mem = pl.BlockSpec((1, N_MEM, D_MODEL), lambda i, j: (i, 0, 0))
    return pl.pallas_call(
        functools.partial(_post_mix_kernel, n_in=n_in),
        grid=(b, t // tm),
        in_specs=([tok(d)] + [tok(a.shape[-1]) for a in acts]
                  + [_full(w.shape), _full((1, d)), _full((d, d)), mem, mem, _full((d, d))]),
        out_specs=tok(d),
        out_shape=jax.ShapeDtypeStruct((b, t, d), F32),
        compiler_params=_params("parallel", "parallel"),
        name="post_mix",
    )(x, *acts, w, g, wq, mk, mv, wo)


def _ffn_kernel(x_ref, g_ref, wu_ref, wd_ref, gf_ref, o_ref, *, fc, final):
    x = x_ref[...]
    h = _rms(x, g_ref[...], EPS).astype(BF16)
    y = x
    for c in range(D_FF // fc):
        u = jnp.maximum(_dot(h, wu_ref[:, c * fc:(c + 1) * fc]), 0.0)
        y = y + _dot((u * u).astype(BF16), wd_ref[c * fc:(c + 1) * fc, :])
    if final:
        y = _rms(y, gf_ref[...], EPS)
    o_ref[...] = y


def _ffn(x2d, g, wu, wd, gf, final, tm, fc):
    m, d = x2d.shape
    return pl.pallas_call(
        functools.partial(_ffn_kernel, fc=fc, final=final),
        grid=(m // tm,),
        in_specs=[pl.BlockSpec((tm, d), lambda i: (i, 0)), _full((1, d)), _full((d, D_FF)),
                  _full((D_FF, d)), _full((1, d))],
        out_specs=pl.BlockSpec((tm, d), lambda i: (i, 0)),
        out_shape=jax.ShapeDtypeStruct((m, d), F32),
        compiler_params=_params("parallel"),
        name="ffn",
    )(x2d, g, wu, wd, gf)


def _rope_tables(pos):
    half = DH_QK // 2
    inv = jnp.power(ROPE_THETA, -jnp.arange(half, dtype=F32) * (2.0 / DH_QK))
    ang = pos.astype(F32)[:, None] * inv[None, :]
    cos, sin = jnp.cos(ang), jnp.sin(ang)
    reps = LANE // DH_QK
    return (jnp.tile(jnp.concatenate([cos, cos], axis=1), (1, reps)),
            jnp.tile(jnp.concatenate([-sin, sin], axis=1), (1, reps)))


def _row(v):
    return v.reshape(1, -1)


def _run(x, mem_kb, mem_vb, past, p, tm, tq):
    b, t, d = x.shape
    past_len = 0 if past is None else past[0].shape[2]
    pos = jnp.arange(past_len, past_len + t)
    cos, sin = _rope_tables(pos)
    outs = {}
    for i in range(DEPTH):
        if i % 2 == 0:
            e = i // 2
            lam_init = 0.8 - 0.6 * math.exp(-0.3 * i)
            state = jnp.zeros((b, CONV_W - 1, D_CONV), F32) if past is None else past[2][e]
            q, k, kb, v, vb, gc, nc = _even_in(x, _row(p['norm_mix'][i]), p['w_in_even_b'][e], cos, sin,
                                               p['conv_w'][e], state, tm)
            lam_args = (_row(p['lambda_q1'][e]), _row(p['lambda_k1'][e]), _row(p['lambda_q2'][e]),
                        _row(p['lambda_k2'][e]), _row(p['subln_gain'][e]))
            if past is None:
                o = _diff_prompt(q, kb, vb, *lam_args, lam_init, min(4 * tq, t), min(1024, t))
            else:
                o = _diff_decode(q, past[0][e].reshape(b, past_len, Q_DIFF),
                                 past[1][e].reshape(b, past_len * H_DIFF, DV_DIFF), kb, vb, *lam_args,
                                 lam_init,
                                 min(512, past_len))
            outs['dk'] = k.reshape(1, b, t, 2 * H_DIFF, DH_QK)
            outs['dv'] = v.reshape(1, b, t, H_DIFF, DV_DIFF)
            outs['conv'] = nc[None]
            acts = [o, gc]
            w_out = p['w_out_even_b'][e]
        else:
            o_idx = i // 2
            segs = ((0, D_SB, False, True, DH_SB ** -0.5 * LOG2E), (D_SB, D_SB, True, True, 1.0),
                    (2 * D_SB, D_SB, True, True, 1.0))
            q, k, kb, v, vb = _norm_proj(x, p['norm_mix'][i].reshape(1, 1, d),
                                         p['w_in_odd_b'][o_idx][None], segs, tm)
            if past is None:
                o = _sb_prompt(q, kb, vb, tq, 2)
            else:
                o = _sb_decode(q, past[3][o_idx], past[4][o_idx], kb, vb, min(256, past_len))
            outs['sk'] = k.reshape(1, b, t, H_SB, DH_SB)
            outs['sv'] = v.reshape(1, b, t, H_SB, DH_SB)
            acts = [o]
            w_out = p['w_out_odd_b'][o_idx]
        x = _post_mix(x, acts, w_out, _row(p['norm_cross'][i]), p['w_q_mem_b'][i], mem_kb[i], mem_vb[i],
                      p['w_o_mem_b'][i], tm)
        m = b * t
        x = _ffn(x.reshape(m, d), _row(p['norm_ffn'][i]), p['w_ffn_up_b'][i], p['w_ffn_down_b'][i],
                 _row(p['norm_final']), i == DEPTH - 1, min(512, m), 1024).reshape(b, t, d)
    return x, outs


def kernel(x_prompt, x_sample, cache_diff_k, cache_diff_v, state_conv, cache_sb_k, cache_sb_v, cache_mem_k, cache_mem_v, mem_prompt, w_in_even, w_out_even, lambda_q1, lambda_k1, lambda_q2, lambda_k2, subln_gain, conv_w, w_in_odd, w_out_odd, norm_mix, norm_mem, norm_cross, w_q_mem, w_k_mem, w_v_mem, w_o_mem, norm_ffn, w_ffn_up, w_ffn_down, norm_final):
    p = dict(norm_mix=norm_mix, norm_cross=norm_cross, norm_ffn=norm_ffn, norm_final=norm_final,
             lambda_q1=lambda_q1, lambda_k1=lambda_k1, lambda_q2=lambda_q2, lambda_k2=lambda_k2,
             subln_gain=subln_gain, conv_w=conv_w)
    for name, w in (('w_in_even', w_in_even), ('w_out_even', w_out_even), ('w_in_odd', w_in_odd),
                    ('w_out_odd', w_out_odd), ('w_q_mem', w_q_mem), ('w_o_mem', w_o_mem),
                    ('w_ffn_up', w_ffn_up), ('w_ffn_down', w_ffn_down)):
        p[name + '_b'] = w.astype(BF16)

    bp = mem_prompt.shape[0]
    w_kv = jnp.concatenate([w_k_mem, w_v_mem], axis=-1).astype(BF16)
    segs = ((0, D_MODEL, True, True, 1.0), (D_MODEL, D_MODEL, True, True, 1.0))
    mk, mkb, mv, mvb = _norm_proj(mem_prompt, norm_mem.reshape(DEPTH, 1, D_MODEL), w_kv, segs, N_MEM)
    p_mem_k = mk.reshape(DEPTH, bp, N_MEM, H_MEM, DH_MEM)
    p_mem_v = mv.reshape(DEPTH, bp, N_MEM, H_MEM, DH_MEM)
    mkb = mkb.reshape(DEPTH, bp, N_MEM, D_MODEL)
    mvb = mvb.reshape(DEPTH, bp, N_MEM, D_MODEL)

    t_p = x_prompt.shape[1]
    y_prompt, po = _run(x_prompt, mkb, mvb, None, p, min(512, t_p), min(256, t_p))

    bs = x_sample.shape[0]
    t_s = x_sample.shape[1]
    cmk = cache_mem_k.reshape(DEPTH, bs, N_MEM, D_MODEL).astype(BF16)
    cmv = cache_mem_v.reshape(DEPTH, bs, N_MEM, D_MODEL).astype(BF16)
    y_sample, so = _run(x_sample, cmk, cmv,
                        (cache_diff_k, cache_diff_v, state_conv, cache_sb_k, cache_sb_v), p, t_s, t_s)

    return (y_prompt, y_sample, po['dk'], po['dv'], po['conv'], po['sk'], po['sv'], p_mem_k, p_mem_v,
            so['dk'], so['dv'], so['conv'], so['sk'], so['sv'])
```

```python
import functools
import math

import jax
import jax.numpy as jnp
from jax import lax
from jax.experimental import pallas as pl
from jax.experimental.pallas import tpu as pltpu

F32 = jnp.float32
BF16 = jnp.bfloat16

D_MODEL = 1024
DEPTH = 2
CHUNK = 64
H_DIFF = 4
DH_QK = 64
DV_DIFF = 2 * DH_QK
D_DIFF = H_DIFF * DV_DIFF
Q_DIFF = 2 * H_DIFF * DH_QK
D_CONV = D_MODEL - D_DIFF
CONV_W = 3
EVEN_IN = 2 * Q_DIFF + D_DIFF + 3 * D_CONV
H_SB = 16
DH_SB = 64
D_SB = H_SB * DH_SB
N_MEM = 256
H_MEM = 4
DH_MEM = D_MODEL // H_MEM
D_FF = 4 * D_MODEL
ROPE_THETA = 10000.0
EPS = 1e-6
SUBLN_EPS = 1e-5
NEG_INF = -1e30
LOG2E = math.log2(math.e)
SB_DEAD = 105.0 * LOG2E
SB_BOUND_SLACK = 1.01

LANE = 128
SUBLANE = 8
VMEM_LIMIT = 52 * 1024 * 1024


def _params(*sem):
    return pltpu.CompilerParams(dimension_semantics=sem, vmem_limit_bytes=VMEM_LIMIT)


def _rms(x, g, eps):
    ms = jnp.mean(x * x, axis=-1, keepdims=True)
    return x * lax.rsqrt(ms + eps) * g


def _dot(a, b):
    return jnp.dot(a, b, preferred_element_type=F32)


def _dot_nt(a, b):
    return lax.dot_general(a, b, (((1,), (1,)), ((), ())), preferred_element_type=F32)


def _full(shape):
    n = len(shape)
    return pl.BlockSpec(shape, lambda *_: (0,) * n)


def _chunk_of(pos):
    shift = CHUNK.bit_length() - 1
    assert 1 << shift == CHUNK
    return lax.shift_right_logical(pos, shift)


def _row_query_index(groups, t, width):
    r = lax.broadcasted_iota(jnp.int32, (t, width), 0)
    return jnp.concatenate([r] * groups, axis=0)


def _lane_band(lane, s, width):
    return (lane >= s * width) & (lane < (s + 1) * width)


def _even_in_kernel(x_ref, g_ref, w_ref, cos_ref, sin_ref, cw_ref, st_ref,
                    q_ref, k_ref, kb_ref, v_ref, vb_ref, gc_ref, nc_ref, ext_ref, *, tm, nt):
    t = pl.program_id(1)
    h = _rms(x_ref[0], g_ref[...], EPS).astype(BF16)

    def proj(lo, width):
        return _dot(h, w_ref[:, lo:lo + width])

    cos = cos_ref[...]
    sin = sin_ref[...]
    lane = lax.broadcasted_iota(jnp.int32, (tm, LANE), 1)
    first_half = (lane & (DH_QK // 2)) == 0

    def rope_group(yj):
        sw = jnp.where(first_half, pltpu.roll(yj, LANE - DH_QK // 2, 1), pltpu.roll(yj, DH_QK // 2, 1))
        return yj * cos + sw * sin

    yq = proj(0, Q_DIFF)
    yk = proj(Q_DIFF, Q_DIFF)
    for j in range(Q_DIFF // LANE):
        sl = slice(j * LANE, (j + 1) * LANE)
        q_ref[0, :, sl] = (rope_group(yq[:, sl]) * (DH_QK ** -0.5 * LOG2E)).astype(BF16)
        kj = rope_group(yk[:, sl])
        k_ref[0, :, sl] = kj
        kb_ref[0, :, sl] = kj.astype(BF16)
    yv = proj(2 * Q_DIFF, D_DIFF)
    for hh in range(H_DIFF):
        v_ref[0, pl.ds(hh, tm, stride=H_DIFF), :] = yv[:, hh * DV_DIFF:(hh + 1) * DV_DIFF]
    vb_ref[0] = yv.astype(BF16)

    base = 2 * Q_DIFF + D_DIFF
    gate_b = proj(base, D_CONV)
    cu = proj(base + D_CONV, D_CONV) * proj(base + 2 * D_CONV, D_CONV)

    @pl.when(t == 0)
    def _():
        ext_ref[SUBLANE - 2:SUBLANE, :] = st_ref[0]

    ext_ref[SUBLANE:SUBLANE + tm, :] = cu
    cw = cw_ref[...]
    conv = (ext_ref[SUBLANE - 2:SUBLANE - 2 + tm, :] * cw[0:1, :]
            + ext_ref[SUBLANE - 1:SUBLANE - 1 + tm, :] * cw[1:2, :]
            + cu * cw[2:3, :])
    gc_ref[0] = (gate_b * conv).astype(BF16)
    ext_ref[0:SUBLANE, :] = ext_ref[tm:tm + SUBLANE, :]

    @pl.when(t == nt - 1)
    def _():
        nc_ref[0] = ext_ref[tm + SUBLANE - 2:tm + SUBLANE, :]


def _even_in(x, g, wb, cos, sin, cw, state, tm):
    b, t, _ = x.shape
    nt = t // tm
    tok = lambda w: pl.BlockSpec((1, tm, w), lambda i, j: (i, j, 0))
    outs = [
        jax.ShapeDtypeStruct((b, t, Q_DIFF), BF16),
        jax.ShapeDtypeStruct((b, t, Q_DIFF), F32),
        jax.ShapeDtypeStruct((b, t, Q_DIFF), BF16),
        jax.ShapeDtypeStruct((b, t * H_DIFF, DV_DIFF), F32),
        jax.ShapeDtypeStruct((b, t, D_DIFF), BF16),
        jax.ShapeDtypeStruct((b, t, D_CONV), BF16),
        jax.ShapeDtypeStruct((b, CONV_W - 1, D_CONV), F32),
    ]
    return pl.pallas_call(
        functools.partial(_even_in_kernel, tm=tm, nt=nt),
        grid=(b, nt),
        in_specs=[
            tok(D_MODEL),
            _full((1, D_MODEL)),
            _full((D_MODEL, EVEN_IN)),
            pl.BlockSpec((tm, LANE), lambda i, j: (j, 0)),
            pl.BlockSpec((tm, LANE), lambda i, j: (j, 0)),
            _full((CONV_W, D_CONV)),
            pl.BlockSpec((1, CONV_W - 1, D_CONV), lambda i, j: (i, 0, 0)),
        ],
        out_specs=[tok(Q_DIFF), tok(Q_DIFF), tok(Q_DIFF),
                   pl.BlockSpec((1, tm * H_DIFF, DV_DIFF), lambda i, j: (i, j, 0)), tok(D_DIFF), tok(D_CONV),
                   pl.BlockSpec((1, CONV_W - 1, D_CONV), lambda i, j: (i, 0, 0))],
        out_shape=outs,
        scratch_shapes=[pltpu.VMEM((tm + 2 * SUBLANE, D_CONV), F32)],
        compiler_params=_params("parallel", "arbitrary"),
        name="even_in",
    )(x, g, wb, cos, sin, cw, state)


def _norm_proj_kernel(x_ref, g_ref, w_ref, *out_refs, segs):
    h = _rms(x_ref[0], g_ref[0], EPS).astype(BF16)
    i = 0
    for lo, width, want_f32, want_bf16, scale in segs:
        y = _dot(h, w_ref[0, :, lo:lo + width])
        if scale != 1.0:
            y = y * scale
        if want_f32:
            out_refs[i][0] = y
            i += 1
        if want_bf16:
            out_refs[i][0] = y.astype(BF16)
            i += 1


def _norm_proj(x, g, wb, segs, tm):
    b, t, d = x.shape
    n = wb.shape[-1]
    ng = wb.shape[0]
    nt = t // tm
    outs, specs = [], []
    for lo, width, want_f32, want_bf16, _ in segs:
        for want, dt in ((want_f32, F32), (want_bf16, BF16)):
            if want:
                outs.append(jax.ShapeDtypeStruct((ng * b, t, width), dt))
                specs.append(pl.BlockSpec((1, tm, width), lambda i, j: (i, j, 0)))
    return pl.pallas_call(
        functools.partial(_norm_proj_kernel, segs=segs),
        grid=(ng * b, nt),
        in_specs=[
            pl.BlockSpec((1, tm, d), lambda i, j: (i % b, j, 0)),
            pl.BlockSpec((1, 1, d), lambda i, j: (i // b, 0, 0)),
            pl.BlockSpec((1, d, n), lambda i, j: (i // b, 0, 0)),
        ],
        out_specs=specs,
        out_shape=outs,
        compiler_params=_params("parallel", "parallel"),
        name="norm_proj",
    )(x, g, wb)


def _lambda_full(lq1_ref, lk1_ref, lq2_ref, lk2_ref, lam_init):
    s1 = jnp.sum(lq1_ref[...] * lk1_ref[...], axis=1, keepdims=True)
    s2 = jnp.sum(lq2_ref[...] * lk2_ref[...], axis=1, keepdims=True)
    return jnp.exp(s1) - jnp.exp(s2) + lam_init


def _subln(o, gsub_ref, lam_init):
    return _rms(o, gsub_ref[...], SUBLN_EPS) * (1.0 - lam_init)


def _diff_prompt_kernel(q_ref, k_ref, v_ref, lq1_ref, lk1_ref, lq2_ref, lk2_ref, gsub_ref,
                        o_ref, *, tq, tk, lam_init):
    qi = pl.program_id(2)
    ratio = tk // tq
    q = q_ref[0]
    lane = lax.broadcasted_iota(jnp.int32, (tq, LANE), 1)
    zero = jnp.zeros_like(q)
    qm = (jnp.where(lane < DH_QK, q, zero), jnp.where(lane >= DH_QK, q, zero))
    ones = jnp.ones((tk, LANE), BF16)

    def step(j, carry, masked):
        start = pl.multiple_of(j * tk, tk)
        kb = k_ref[0, pl.ds(start, tk), :]
        vb = jnp.concatenate([v_ref[0, pl.ds(start, tk), :], ones], axis=1)
        if masked:
            r = _chunk_of(lax.broadcasted_iota(jnp.int32, (tq, tk), 0)) + lax.rem(qi, ratio) * (tq // CHUNK)
            c = _chunk_of(lax.broadcasted_iota(jnp.int32, (tq, tk), 1))
            vis = c <= r
        ss = [_dot_nt(qm[mi], kb) for mi in range(2)]
        if masked:
            ss = [jnp.where(vis, s, NEG_INF) for s in ss]
        ms = [jnp.maximum(carry[mi][0], jnp.max(ss[mi], axis=1, keepdims=True)) for mi in range(2)]
        ps = [jnp.exp2(ss[mi] - ms[mi]).astype(BF16) for mi in range(2)]
        pv = [_dot(ps[mi], vb) for mi in range(2)]
        return tuple((ms[mi], jnp.exp2(carry[mi][0] - ms[mi]) * carry[mi][1] + pv[mi]) for mi in range(2))

    init = tuple((jnp.full((tq, 1), NEG_INF, F32), jnp.zeros((tq, 2 * LANE), F32)) for _ in range(2))
    n_full = lax.div(qi, ratio)
    carry = lax.fori_loop(0, n_full, lambda j, c: step(j, c, False), init)
    carry = step(n_full, carry, True)
    lam = _lambda_full(lq1_ref, lk1_ref, lq2_ref, lk2_ref, lam_init)
    (_, a1), (_, a2) = carry
    o = a1[:, :LANE] / a1[:, LANE:] - lam * (a2[:, :LANE] / a2[:, LANE:])
    o_ref[0] = _subln(o, gsub_ref, lam_init).astype(BF16)


def _diff_prompt(q, kb, vb, lq1, lk1, lq2, lk2, gsub, lam_init, tq, tk):
    b, t, _ = q.shape
    lam_spec = _full((1, DH_QK))
    return pl.pallas_call(
        functools.partial(_diff_prompt_kernel, tq=tq, tk=tk, lam_init=lam_init),
        grid=(b, H_DIFF, t // tq),
        in_specs=[
            pl.BlockSpec((1, tq, LANE), lambda i, h, j: (i, j, h)),
            pl.BlockSpec((1, t, LANE), lambda i, h, j: (i, 0, h)),
            pl.BlockSpec((1, t, LANE), lambda i, h, j: (i, 0, h)),
            lam_spec, lam_spec, lam_spec, lam_spec,
            _full((1, DV_DIFF)),
        ],
        out_specs=pl.BlockSpec((1, tq, LANE), lambda i, h, j: (i, j, h)),
        out_shape=jax.ShapeDtypeStruct((b, t, D_DIFF), BF16),
        compiler_params=_params("parallel", "parallel", "arbitrary"),
        name="diff_prompt",
    )(q, kb, vb, lq1, lk1, lq2, lk2, gsub)


def _diff_decode_kernel(q_ref, kp_ref, vp_ref, kn_ref, vn_ref, lq1_ref, lk1_ref, lq2_ref, lk2_ref,
                        gsub_ref, o_ref, qs_ref, m_ref, l_ref, acc_ref, *, t, tk, nk, past_len,
                        lam_init):
    j = pl.program_id(1)
    nsub = 2 * H_DIFF
    rows = nsub * t

    @pl.when(j == 0)
    def _():
        q = q_ref[0]
        lane = lax.broadcasted_iota(jnp.int32, (t, Q_DIFF), 1)
        for s in range(nsub):
            qs_ref[s * t:(s + 1) * t, :] = jnp.where(_lane_band(lane, s, DH_QK), q, jnp.zeros_like(q))
        m_ref[...] = jnp.full(m_ref.shape, NEG_INF, F32)
        l_ref[...] = jnp.zeros(l_ref.shape, F32)
        acc_ref[...] = jnp.zeros(acc_ref.shape, F32)

    def update(kb, vh, k_start, width):
        s = _dot_nt(qs_ref[...], kb)
        q_pos = past_len + _row_query_index(nsub, t, width)
        k_pos = k_start + lax.broadcasted_iota(jnp.int32, (rows, width), 1)
        s = jnp.where(_chunk_of(k_pos) <= _chunk_of(q_pos), s, NEG_INF)
        m = m_ref[...]
        m_new = jnp.maximum(m, jnp.max(s, axis=1, keepdims=True))
        alpha = jnp.exp2(m - m_new)
        p = jnp.exp2(s - m_new)
        l_ref[...] = alpha * l_ref[...] + jnp.sum(p, axis=1, keepdims=True)
        m_ref[...] = m_new
        pb = p.astype(BF16)
        for h in range(H_DIFF):
            rs = slice(2 * h * t, (2 * h + 2) * t)
            acc_ref[rs, :] = alpha[rs] * acc_ref[rs, :] + _dot(pb[rs], vh[h])

    kp = jnp.concatenate([kp_ref[0, pl.ds(h, tk, stride=H_DIFF), :] for h in range(H_DIFF)], axis=1)
    update(kp.astype(BF16),
           [vp_ref[0, pl.ds(h, tk, stride=H_DIFF), :].astype(BF16) for h in range(H_DIFF)], j * tk, tk)

    @pl.when(j == nk - 1)
    def _():
        vn = vn_ref[0]
        update(kn_ref[0], [vn[:, h * DV_DIFF:(h + 1) * DV_DIFF] for h in range(H_DIFF)], past_len, t)
        lam = _lambda_full(lq1_ref, lk1_ref, lq2_ref, lk2_ref, lam_init)
        on = acc_ref[...] / l_ref[...]
        for h in range(H_DIFF):
            o = on[2 * h * t:(2 * h + 1) * t] - lam * on[(2 * h + 1) * t:(2 * h + 2) * t]
            o_ref[0, :, h * LANE:(h + 1) * LANE] = _subln(o, gsub_ref, lam_init).astype(BF16)


def _diff_decode(q, k_past, v_past, kn, vn, lq1, lk1, lq2, lk2, gsub, lam_init, tk):
    b, t, _ = q.shape
    past_len = k_past.shape[1] // H_DIFF
    nk = past_len // tk
    rows = 2 * H_DIFF * t
    lam_spec = _full((1, DH_QK))
    tokb = lambda w: pl.BlockSpec((1, t, w), lambda i, j: (i, 0, 0))
    return pl.pallas_call(
        functools.partial(_diff_decode_kernel, t=t, tk=tk, nk=nk, past_len=past_len,
                          lam_init=lam_init),
        grid=(b, nk),
        in_specs=[
            tokb(Q_DIFF),
            pl.BlockSpec((1, tk * H_DIFF, LANE), lambda i, j: (i, j, 0)),
            pl.BlockSpec((1, tk * H_DIFF, DV_DIFF), lambda i, j: (i, j, 0)),
            tokb(Q_DIFF), tokb(D_DIFF),
            lam_spec, lam_spec, lam_spec, lam_spec,
            _full((1, DV_DIFF)),
        ],
        out_specs=tokb(D_DIFF),
        out_shape=jax.ShapeDtypeStruct((b, t, D_DIFF), BF16),
        scratch_shapes=[pltpu.VMEM((rows, Q_DIFF), BF16), pltpu.VMEM((rows, 1), F32),
                        pltpu.VMEM((rows, 1), F32), pltpu.VMEM((rows, DV_DIFF), F32)],
        compiler_params=_params("parallel", "arbitrary"),
        name="diff_decode",
    )(q, k_past, v_past, kn, vn, lq1, lk1, lq2, lk2, gsub)


def _softplus(z):
    return jnp.maximum(z, 0.0) + jnp.log(1.0 + jnp.exp2(-jnp.abs(z))) * LOG2E


def _rev_cumsum(sp, tri):
    hi = sp.astype(BF16)
    lo = (sp - hi.astype(F32)).astype(BF16)
    return _dot(hi, tri) + _dot(lo, tri)


def _tri(n):
    r = lax.broadcasted_iota(jnp.int32, (n, n), 0)
    c = lax.broadcasted_iota(jnp.int32, (n, n), 1)
    return jnp.where(r >= c, 1.0, 0.0).astype(BF16)


def _head_ones(width):
    r = lax.broadcasted_iota(jnp.int32, (LANE, LANE), 0)
    c = lax.broadcasted_iota(jnp.int32, (LANE, LANE), 1)
    return jnp.where((r < width) == (c < width), 1.0, 0.0).astype(BF16)


def _head_sq_norms(x_bf16, ones_blk):
    xf = x_bf16.astype(F32)
    return _dot((xf * xf).astype(BF16), ones_blk)


def _sb_prompt_kernel(q_ref, k_ref, v_ref, o_ref, kmax_ref, *, tq, t, ng):
    qi = pl.program_id(2)
    lane = lax.broadcasted_iota(jnp.int32, (tq, LANE), 1)
    tri = _tri(tq)
    ones_blk = _head_ones(DH_SB)
    gs = [slice(g * LANE, (g + 1) * LANE) for g in range(ng)]

    @pl.when(qi == 0)
    def _():
        rows = math.gcd(t, 4 * tq)
        for g in range(ng):
            def body(i, mx, g=g):
                kc = k_ref[0, pl.ds(pl.multiple_of(i * rows, rows), rows), gs[g]]
                return jnp.maximum(mx, _head_sq_norms(kc, ones_blk))
            mx = lax.fori_loop(0, t // rows, body, jnp.zeros((rows, LANE), F32))
            kmax_ref[:, gs[g]] = jnp.max(mx, axis=0, keepdims=True)

    qs, bzs = [], []
    for g in range(ng):
        q = q_ref[0, :, gs[g]]
        zero = jnp.zeros_like(q)
        qs.append(jnp.concatenate([jnp.where(lane < DH_SB, q, zero), jnp.where(lane >= DH_SB, q, zero)], axis=0))
        bound = jnp.sqrt(_head_sq_norms(q, ones_blk) * kmax_ref[:, gs[g]]) * SB_BOUND_SLACK
        bzs += [bound[:, 0:1], bound[:, DH_SB:DH_SB + 1]]
    bz = jnp.concatenate(bzs, axis=0)

    def alive(ccar):
        return (jnp.max(bz - ccar) > -SB_DEAD).astype(jnp.int32)

    def blk(ref, j, g):
        return ref[0, pl.ds(pl.multiple_of(j * tq, tq), tq), gs[g]]

    def scores(j):
        return jnp.concatenate([_dot_nt(qs[g], blk(k_ref, j, g)) for g in range(ng)], axis=0)

    def weighted(w, vals):
        return jnp.concatenate([_dot(w[2 * g * tq:2 * (g + 1) * tq], vals(g)) for g in range(ng)], axis=0)

    has_prev = qi > 0
    jp = jnp.maximum(qi - 1, 0)
    z_r = scores(qi)
    z_l = scores(jp)
    vis = lax.broadcasted_iota(jnp.int32, (2 * ng * tq, tq), 1) < _row_query_index(2 * ng, tq, tq)
    sp_r = jnp.where(vis, _softplus(z_r), 0.0)
    cl_r = _rev_cumsum(sp_r, tri)
    total_r = cl_r[:, 0:1]
    cl_l = _rev_cumsum(_softplus(z_l), tri) + jnp.where(has_prev, total_r, -NEG_INF)
    w_r = jnp.where(vis, jnp.exp2(z_r - cl_r), 0.0).astype(BF16)
    w_l = jnp.exp2(z_l - cl_l).astype(BF16)
    acc = weighted(jnp.concatenate([w_l, w_r], axis=1),
                   lambda g: jnp.concatenate([blk(v_ref, jp, g), blk(v_ref, qi, g)], axis=0))
    ccar = jnp.where(has_prev, cl_l[:, 0:1], total_r)

    def body(c):
        j, _, ccar, acc = c
        z = scores(j)
        cl = _rev_cumsum(_softplus(z), tri)
        w = jnp.exp2(z - ccar - cl).astype(BF16)
        ccar = ccar + cl[:, 0:1]
        return j - 1, alive(ccar), ccar, acc + weighted(w, lambda g: blk(v_ref, j, g))

    out = lax.while_loop(lambda c: jnp.logical_and(c[0] >= 0, c[1] > 0), body,
                         (qi - 2, alive(ccar), ccar, acc))
    acc = out[3]
    for g in range(ng):
        a = acc[2 * g * tq:2 * (g + 1) * tq]
        o_ref[0, :, gs[g]] = jnp.where(lane < DH_SB, a[:tq], a[tq:]).astype(BF16)


def _sb_prompt(q, kb, vb, tq, ng):
    b, t, _ = q.shape
    wg = ng * LANE
    return pl.pallas_call(
        functools.partial(_sb_prompt_kernel, tq=tq, t=t, ng=ng),
        scratch_shapes=[pltpu.VMEM((1, wg), F32)],
        grid=(b, D_SB // wg, t // tq),
        in_specs=[
            pl.BlockSpec((1, tq, wg), lambda i, h, j: (i, j, h)),
            pl.BlockSpec((1, t, wg), lambda i, h, j: (i, 0, h)),
            pl.BlockSpec((1, t, wg), lambda i, h, j: (i, 0, h)),
        ],
        out_specs=pl.BlockSpec((1, tq, wg), lambda i, h, j: (i, j, h)),
        out_shape=jax.ShapeDtypeStruct((b, t, D_SB), BF16),
        compiler_params=_params("parallel", "parallel", "arbitrary"),
        name="sb_prompt",
    )(q, kb, vb)


SB_PAIRS = D_SB // LANE


def _sb_kmax_kernel(k_ref, o_ref, *, tk):
    j = pl.program_id(1)
    n2 = _head_sq_norms(k_ref[0], _head_ones(DH_SB)).reshape(tk, SB_PAIRS, LANE)
    blk = jnp.max(n2, axis=0)

    @pl.when(j == 0)
    def _():
        o_ref[0] = blk

    @pl.when(j > 0)
    def _():
        o_ref[0] = jnp.maximum(o_ref[0], blk)


def _sb_kmax(k_rows, tk):
    b, rows, _ = k_rows.shape
    return pl.pallas_call(
        functools.partial(_sb_kmax_kernel, tk=tk),
        grid=(b, rows // (tk * SB_PAIRS)),
        in_specs=[pl.BlockSpec((1, tk * SB_PAIRS, LANE), lambda i, j: (i, j, 0))],
        out_specs=pl.BlockSpec((1, SB_PAIRS, LANE), lambda i, j: (i, 0, 0)),
        out_shape=jax.ShapeDtypeStruct((b, SB_PAIRS, LANE), F32),
        compiler_params=_params("parallel", "arbitrary"),
        name="sb_kmax",
    )(k_rows)


def _heads(x):
    return [x[:, h * DH_SB:(h + 1) * DH_SB] for h in range(H_SB)]


def _sb_rows_attend(qh, kh, vh, t, width, ccar, acc, vis):
    z = jnp.concatenate([_dot_nt(qh[h], kh[h]) for h in range(H_SB)], axis=0)
    sp = _softplus(z)
    if vis is not None:
        sp = jnp.where(vis, sp, 0.0)
    cl = _rev_cumsum(sp, _tri(width))
    w = jnp.exp2(z - ccar - cl)
    if vis is not None:
        w = jnp.where(vis, w, 0.0)
    w = w.astype(BF16)
    pv = jnp.concatenate([_dot(w[h * t:(h + 1) * t], vh[h]) for h in range(H_SB)], axis=0)
    return ccar + cl[:, 0:1], acc + pv


def _sb_rows_bound(qh, kmax2):
    return jnp.concatenate(
        [jnp.sqrt(jnp.sum(jnp.square(qh[h].astype(F32)), axis=1, keepdims=True)
                  * kmax2[h // 2:h // 2 + 1, (h % 2) * DH_SB:(h % 2) * DH_SB + 1])
         for h in range(H_SB)], axis=0) * SB_BOUND_SLACK


def _sb_rows_alive(bz, ccar):
    return (jnp.max(bz - ccar) > -SB_DEAD).astype(jnp.int32)


def _sb_rows_store(o_ref, acc, t):
    for h in range(H_SB):
        o_ref[0, :, h * DH_SB:(h + 1) * DH_SB] = acc[h * t:(h + 1) * t, :].astype(BF16)


def _sb_decode_head_kernel(q_ref, kn_ref, vn_ref, kt_ref, vt_ref, kmax_ref,
                           o_ref, c_ref, acc_ref, alive_ref, *, t, tk):
    rows = H_SB * t
    qh = _heads(q_ref[0])
    vis = lax.broadcasted_iota(jnp.int32, (rows, t), 1) < _row_query_index(H_SB, t, t)
    ccar, acc = _sb_rows_attend(qh, _heads(kn_ref[0]), _heads(vn_ref[0]), t, t,
                                jnp.zeros((rows, 1), F32), jnp.zeros((rows, DH_SB), F32), vis)
    ccar, acc = _sb_rows_attend(qh, _heads(kt_ref[0].astype(BF16)), _heads(vt_ref[0].astype(BF16)),
                                t, tk, ccar, acc, None)
    _sb_rows_store(o_ref, acc, t)
    c_ref[0] = ccar
    acc_ref[0] = acc
    alive_ref[0] = jnp.zeros((1, LANE), jnp.int32) + _sb_rows_alive(_sb_rows_bound(qh, kmax_ref[0]), ccar)


def _sb_decode_rest_kernel(q_ref, kmax_ref, c_ref, acc_ref, kp_hbm, vp_hbm, o_ref, kbuf, vbuf, sem,
                           *, t, tk, nk):
    b = pl.program_id(0)
    qh = _heads(q_ref[0])
    bz = _sb_rows_bound(qh, kmax_ref[0])

    def body(c):
        j, _, ccar, acc = c
        start = pl.multiple_of(j * tk, tk)
        ck = pltpu.make_async_copy(kp_hbm.at[b, pl.ds(start, tk), :], kbuf, sem.at[0])
        cv = pltpu.make_async_copy(vp_hbm.at[b, pl.ds(start, tk), :], vbuf, sem.at[1])
        ck.start()
        cv.start()
        ck.wait()
        cv.wait()
        ccar, acc = _sb_rows_attend(qh, _heads(kbuf[...].astype(BF16)), _heads(vbuf[...].astype(BF16)),
                                    t, tk, ccar, acc, None)
        return j - 1, _sb_rows_alive(bz, ccar), ccar, acc

    ccar = c_ref[0]
    out = lax.while_loop(lambda c: jnp.logical_and(c[0] >= 0, c[1] > 0), body,
                         (jnp.int32(nk - 2), _sb_rows_alive(bz, ccar), ccar, acc_ref[0]))
    _sb_rows_store(o_ref, out[3], t)


def _sb_decode(q, k_past, v_past, kn, vn, tk):
    b, t, _ = q.shape
    past_len = k_past.shape[1]
    nk = past_len // tk
    rows = H_SB * t
    kmax2 = _sb_kmax(k_past.reshape(b, past_len * SB_PAIRS, LANE), min(512, past_len))
    k_tail = k_past[:, past_len - tk:].reshape(b, tk, D_SB)
    v_tail = v_past[:, past_len - tk:].reshape(b, tk, D_SB)
    tokb = lambda: pl.BlockSpec((1, t, D_SB), lambda i: (i, 0, 0))
    tail = lambda: pl.BlockSpec((1, tk, D_SB), lambda i: (i, 0, 0))
    kmx = lambda: pl.BlockSpec((1, SB_PAIRS, LANE), lambda i: (i, 0, 0))
    car = lambda: pl.BlockSpec((1, rows, 1), lambda i: (i, 0, 0))
    accs = lambda: pl.BlockSpec((1, rows, DH_SB), lambda i: (i, 0, 0))
    o, ccar, acc, alive = pl.pallas_call(
        functools.partial(_sb_decode_head_kernel, t=t, tk=tk),
        grid=(b,),
        in_specs=[tokb(), tokb(), tokb(), tail(), tail(), kmx()],
        out_specs=[tokb(), car(), accs(), pl.BlockSpec((1, 1, LANE), lambda i: (i, 0, 0))],
        out_shape=[jax.ShapeDtypeStruct((b, t, D_SB), BF16), jax.ShapeDtypeStruct((b, rows, 1), F32),
                   jax.ShapeDtypeStruct((b, rows, DH_SB), F32),
                   jax.ShapeDtypeStruct((b, 1, LANE), jnp.int32)],
        compiler_params=_params("parallel"),
        name="sb_decode_head",
    )(q, kn, vn, k_tail, v_tail, kmax2)
    if nk < 2:
        return o

    def rest(args):
        q_, kmax2_, ccar_, acc_, k_past_, v_past_, _ = args
        return pl.pallas_call(
            functools.partial(_sb_decode_rest_kernel, t=t, tk=tk, nk=nk),
            grid=(b,),
            in_specs=[tokb(), kmx(), car(), accs(), pl.BlockSpec(memory_space=pl.ANY),
                      pl.BlockSpec(memory_space=pl.ANY)],
            out_specs=tokb(),
            out_shape=jax.ShapeDtypeStruct((b, t, D_SB), BF16),
            scratch_shapes=[pltpu.VMEM((tk, D_SB), F32), pltpu.VMEM((tk, D_SB), F32),
                            pltpu.SemaphoreType.DMA((2,))],
            compiler_params=_params("arbitrary"),
            name="sb_decode_rest",
        )(q_, kmax2_, ccar_, acc_, k_past_.reshape(b, past_len, D_SB), v_past_.reshape(b, past_len, D_SB))

    return lax.cond(jnp.max(alive) > 0, rest, lambda args: args[-1],
                    (q, kmax2, ccar, acc, k_past, v_past, o))


def _post_mix_kernel(*refs, n_in):
    x_ref = refs[0]
    a_refs = refs[1:1 + n_in]
    w_ref, g_ref, wq_ref, mk_ref, mv_ref, wo_ref, o_ref = refs[1 + n_in:]
    a = a_refs[0][0] if n_in == 1 else jnp.concatenate([r[0] for r in a_refs], axis=1)
    x = x_ref[0] + _dot(a, w_ref[...])
    hq = _rms(x, g_ref[...], EPS).astype(BF16)
    q = (_dot(hq, wq_ref[...]) * (DH_MEM ** -0.5 * LOG2E)).astype(BF16)
    hs = [slice(h * DH_MEM, (h + 1) * DH_MEM) for h in range(H_MEM)]
    ss = [_dot_nt(q[:, hs[h]], mk_ref[0, :, hs[h]]) for h in range(H_MEM)]
    ps = [jnp.exp2(s - jnp.max(s, axis=1, keepdims=True)) for s in ss]
    ls = [jnp.sum(p, axis=1, keepdims=True) for p in ps]
    oh = [(_dot(ps[h].astype(BF16), mv_ref[0, :, hs[h]]) / ls[h]).astype(BF16) for h in range(H_MEM)]
    o_ref[0] = x + _dot(jnp.concatenate(oh, axis=1), wo_ref[...])


def _post_mix(x, acts, w, g, wq, mk, mv, wo, tm):
    b, t, d = x.shape
    n_in = len(acts)
    tok = lambda w: pl.BlockSpec((1, tm, w), lambda i, j: (i, j, 0))
    mem = pl.BlockSpec((1, N_MEM, D_MODEL), lambda i, j: (i, 0, 0))
    return pl.pallas_call(
        functools.partial(_post_mix_kernel, n_in=n_in),
        grid=(b, t // tm),
        in_specs=([tok(d)] + [tok(a.shape[-1]) for a in acts]
                  + [_full(w.shape), _full((1, d)), _full((d, d)), mem, mem, _full((d, d))]),
        out_specs=tok(d),
        out_shape=jax.ShapeDtypeStruct((b, t, d), F32),
        compiler_params=_params("parallel", "parallel"),
        name="post_mix",
    )(x, *acts, w, g, wq, mk, mv, wo)


def _ffn_kernel(x_ref, g_ref, wu_ref, wd_ref, gf_ref, o_ref, *, fc, final):
    x = x_ref[...]
    h = _rms(x, g_ref[...], EPS).astype(BF16)
    y = x
    for c in range(D_FF // fc):
        u = jnp.maximum(_dot(h, wu_ref[:, c * fc:(c + 1) * fc]), 0.0)
        y = y + _dot((u * u).astype(BF16), wd_ref[c * fc:(c + 1) * fc, :])
    if final:
        y = _rms(y, gf_ref[...], EPS)
    o_ref[...] = y


def _ffn(x2d, g, wu, wd, gf, final, tm, fc):
    m, d = x2d.shape
    return pl.pallas_call(
        functools.partial(_ffn_kernel, fc=fc, final=final),
        grid=(m // tm,),
        in_specs=[pl.BlockSpec((tm, d), lambda i: (i, 0)), _full((1, d)), _full((d, D_FF)),
                  _full((D_FF, d)), _full((1, d))],
        out_specs=pl.BlockSpec((tm, d), lambda i: (i, 0)),
        out_shape=jax.ShapeDtypeStruct((m, d), F32),
        compiler_params=_params("parallel"),
        name="ffn",
    )(x2d, g, wu, wd, gf)


def _rope_tables(pos):
    half = DH_QK // 2
    inv = jnp.power(ROPE_THETA, -jnp.arange(half, dtype=F32) * (2.0 / DH_QK))
    ang = pos.astype(F32)[:, None] * inv[None, :]
    cos, sin = jnp.cos(ang), jnp.sin(ang)
    reps = LANE // DH_QK
    return (jnp.tile(jnp.concatenate([cos, cos], axis=1), (1, reps)),
            jnp.tile(jnp.concatenate([-sin, sin], axis=1), (1, reps)))


def _row(v):
    return v.reshape(1, -1)


def _run(x, mem_kb, mem_vb, past, p, tm, tq):
    b, t, d = x.shape
    past_len = 0 if past is None else past[0].shape[2]
    pos = jnp.arange(past_len, past_len + t)
    cos, sin = _rope_tables(pos)
    outs = {}
    for i in range(DEPTH):
        if i % 2 == 0:
            e = i // 2
            lam_init = 0.8 - 0.6 * math.exp(-0.3 * i)
            state = jnp.zeros((b, CONV_W - 1, D_CONV), F32) if past is None else past[2][e]
            q, k, kb, v, vb, gc, nc = _even_in(x, _row(p['norm_mix'][i]), p['w_in_even_b'][e], cos, sin,
                                               p['conv_w'][e], state, tm)
            lam_args = (_row(p['lambda_q1'][e]), _row(p['lambda_k1'][e]), _row(p['lambda_q2'][e]),
                        _row(p['lambda_k2'][e]), _row(p['subln_gain'][e]))
            if past is None:
                o = _diff_prompt(q, kb, vb, *lam_args, lam_init, min(4 * tq, t), min(1024, t))
            else:
                o = _diff_decode(q, past[0][e].reshape(b, past_len * H_DIFF, LANE),
                                 past[1][e].reshape(b, past_len * H_DIFF, DV_DIFF), kb, vb, *lam_args,
                                 lam_init,
                                 min(512, past_len))
            outs['dk'] = k.reshape(1, b, t, 2 * H_DIFF, DH_QK)
            outs['dv'] = v.reshape(1, b, t, H_DIFF, DV_DIFF)
            outs['conv'] = nc[None]
            acts = [o, gc]
            w_out = p['w_out_even_b'][e]
        else:
            o_idx = i // 2
            segs = ((0, D_SB, False, True, DH_SB ** -0.5 * LOG2E), (D_SB, D_SB, True, True, 1.0),
                    (2 * D_SB, D_SB, True, True, 1.0))
            q, k, kb, v, vb = _norm_proj(x, p['norm_mix'][i].reshape(1, 1, d),
                                         p['w_in_odd_b'][o_idx][None], segs, tm)
            if past is None:
                o = _sb_prompt(q, kb, vb, tq, 2)
            else:
                o = _sb_decode(q, past[3][o_idx], past[4][o_idx], kb, vb, min(256, past_len))
            outs['sk'] = k.reshape(1, b, t, H_SB, DH_SB)
            outs['sv'] = v.reshape(1, b, t, H_SB, DH_SB)
            acts = [o]
            w_out = p['w_out_odd_b'][o_idx]
        x = _post_mix(x, acts, w_out, _row(p['norm_cross'][i]), p['w_q_mem_b'][i], mem_kb[i], mem_vb[i],
                      p['w_o_mem_b'][i], tm)
        m = b * t
        x = _ffn(x.reshape(m, d), _row(p['norm_ffn'][i]), p['w_ffn_up_b'][i], p['w_ffn_down_b'][i],
                 _row(p['norm_final']), i == DEPTH - 1, min(512, m), 1024).reshape(b, t, d)
    return x, outs


def kernel(x_prompt, x_sample, cache_diff_k, cache_diff_v, state_conv, cache_sb_k, cache_sb_v, cache_mem_k, cache_mem_v, mem_prompt, w_in_even, w_out_even, lambda_q1, lambda_k1, lambda_q2, lambda_k2, subln_gain, conv_w, w_in_odd, w_out_odd, norm_mix, norm_mem, norm_cross, w_q_mem, w_k_mem, w_v_mem, w_o_mem, norm_ffn, w_ffn_up, w_ffn_down, norm_final):
    p = dict(norm_mix=norm_mix, norm_cross=norm_cross, norm_ffn=norm_ffn, norm_final=norm_final,
             lambda_q1=lambda_q1, lambda_k1=lambda_k1, lambda_q2=lambda_q2, lambda_k2=lambda_k2,
             subln_gain=subln_gain, conv_w=conv_w)
    for name, w in (('w_in_even', w_in_even), ('w_out_even', w_out_even), ('w_in_odd', w_in_odd),
                    ('w_out_odd', w_out_odd), ('w_q_mem', w_q_mem), ('w_o_mem', w_o_mem),
                    ('w_ffn_up', w_ffn_up), ('w_ffn_down', w_ffn_down)):
        p[name + '_b'] = w.astype(BF16)

    bp = mem_prompt.shape[0]
    w_kv = jnp.concatenate([w_k_mem, w_v_mem], axis=-1).astype(BF16)
    segs = ((0, D_MODEL, True, True, 1.0), (D_MODEL, D_MODEL, True, True, 1.0))
    mk, mkb, mv, mvb = _norm_proj(mem_prompt, norm_mem.reshape(DEPTH, 1, D_MODEL), w_kv, segs, N_MEM)
    p_mem_k = mk.reshape(DEPTH, bp, N_MEM, H_MEM, DH_MEM)
    p_mem_v = mv.reshape(DEPTH, bp, N_MEM, H_MEM, DH_MEM)
    mkb = mkb.reshape(DEPTH, bp, N_MEM, D_MODEL)
    mvb = mvb.reshape(DEPTH, bp, N_MEM, D_MODEL)

    t_p = x_prompt.shape[1]
    y_prompt, po = _run(x_prompt, mkb, mvb, None, p, min(512, t_p), min(256, t_p))

    bs = x_sample.shape[0]
    t_s = x_sample.shape[1]
    cmk = cache_mem_k.reshape(DEPTH, bs, N_MEM, D_MODEL).astype(BF16)
    cmv = cache_mem_v.reshape(DEPTH, bs, N_MEM, D_MODEL).astype(BF16)
    y_sample, so = _run(x_sample, cmk, cmv,
                        (cache_diff_k, cache_diff_v, state_conv, cache_sb_k, cache_sb_v), p, t_s, t_s)

    return (y_prompt, y_sample, po['dk'], po['dv'], po['conv'], po['sk'], po['sv'], p_mem_k, p_mem_v,
            so['dk'], so['dv'], so['conv'], so['sk'], so['sv'])
```

```python
import functools
import math

import jax
import jax.numpy as jnp
from jax import lax
from jax.experimental import pallas as pl
from jax.experimental.pallas import tpu as pltpu

F32 = jnp.float32
BF16 = jnp.bfloat16

D_MODEL = 1024
DEPTH = 2
CHUNK = 64
H_DIFF = 4
DH_QK = 64
DV_DIFF = 2 * DH_QK
D_DIFF = H_DIFF * DV_DIFF
Q_DIFF = 2 * H_DIFF * DH_QK
D_CONV = D_MODEL - D_DIFF
CONV_W = 3
EVEN_IN = 2 * Q_DIFF + D_DIFF + 3 * D_CONV
H_SB = 16
DH_SB = 64
D_SB = H_SB * DH_SB
N_MEM = 256
H_MEM = 4
DH_MEM = D_MODEL // H_MEM
D_FF = 4 * D_MODEL
ROPE_THETA = 10000.0
EPS = 1e-6
SUBLN_EPS = 1e-5
NEG_INF = -1e30
LOG2E = math.log2(math.e)
SB_DEAD = 105.0 * LOG2E
SB_BOUND_SLACK = 1.01

LANE = 128
SUBLANE = 8
VMEM_LIMIT = 52 * 1024 * 1024


def _params(*sem):
    return pltpu.CompilerParams(dimension_semantics=sem, vmem_limit_bytes=VMEM_LIMIT)


def _rms(x, g, eps):
    ms = jnp.mean(x * x, axis=-1, keepdims=True)
    return x * lax.rsqrt(ms + eps) * g


def _dot(a, b):
    return jnp.dot(a, b, preferred_element_type=F32)


def _dot_nt(a, b):
    return lax.dot_general(a, b, (((1,), (1,)), ((), ())), preferred_element_type=F32)


def _full(shape):
    n = len(shape)
    return pl.BlockSpec(shape, lambda *_: (0,) * n)


def _chunk_of(pos):
    shift = CHUNK.bit_length() - 1
    assert 1 << shift == CHUNK
    return lax.shift_right_logical(pos, shift)


def _row_query_index(groups, t, width):
    r = lax.broadcasted_iota(jnp.int32, (t, width), 0)
    return jnp.concatenate([r] * groups, axis=0)


def _lane_band(lane, s, width):
    return (lane >= s * width) & (lane < (s + 1) * width)


def _even_in_kernel(x_ref, g_ref, w_ref, cos_ref, sin_ref, cw_ref, st_ref,
                    q_ref, k_ref, kb_ref, v_ref, vb_ref, gc_ref, nc_ref, ext_ref, *, tm, nt):
    t = pl.program_id(1)
    h = _rms(x_ref[0], g_ref[...], EPS).astype(BF16)

    def proj(lo, width):
        return _dot(h, w_ref[:, lo:lo + width])

    cos = cos_ref[...]
    sin = sin_ref[...]
    lane = lax.broadcasted_iota(jnp.int32, (tm, LANE), 1)
    first_half = (lane & (DH_QK // 2)) == 0

    def rope_group(yj):
        sw = jnp.where(first_half, pltpu.roll(yj, LANE - DH_QK // 2, 1), pltpu.roll(yj, DH_QK // 2, 1))
        return yj * cos + sw * sin

    yq = proj(0, Q_DIFF)
    yk = proj(Q_DIFF, Q_DIFF)
    for j in range(Q_DIFF // LANE):
        sl = slice(j * LANE, (j + 1) * LANE)
        q_ref[0, :, sl] = (rope_group(yq[:, sl]) * (DH_QK ** -0.5 * LOG2E)).astype(BF16)
        kj = rope_group(yk[:, sl])
        k_ref[0, :, sl] = kj
        kb_ref[0, :, sl] = kj.astype(BF16)
    yv = proj(2 * Q_DIFF, D_DIFF)
    for hh in range(H_DIFF):
        v_ref[0, pl.ds(hh, tm, stride=H_DIFF), :] = yv[:, hh * DV_DIFF:(hh + 1) * DV_DIFF]
    vb_ref[0] = yv.astype(BF16)

    base = 2 * Q_DIFF + D_DIFF
    gate_b = proj(base, D_CONV)
    cu = proj(base + D_CONV, D_CONV) * proj(base + 2 * D_CONV, D_CONV)

    @pl.when(t == 0)
    def _():
        ext_ref[SUBLANE - 2:SUBLANE, :] = st_ref[0]

    ext_ref[SUBLANE:SUBLANE + tm, :] = cu
    cw = cw_ref[...]
    conv = (ext_ref[SUBLANE - 2:SUBLANE - 2 + tm, :] * cw[0:1, :]
            + ext_ref[SUBLANE - 1:SUBLANE - 1 + tm, :] * cw[1:2, :]
            + cu * cw[2:3, :])
    gc_ref[0] = (gate_b * conv).astype(BF16)
    ext_ref[0:SUBLANE, :] = ext_ref[tm:tm + SUBLANE, :]

    @pl.when(t == nt - 1)
    def _():
        nc_ref[0] = ext_ref[tm + SUBLANE - 2:tm + SUBLANE, :]


def _even_in(x, g, wb, cos, sin, cw, state, tm):
    b, t, _ = x.shape
    nt = t // tm
    tok = lambda w: pl.BlockSpec((1, tm, w), lambda i, j: (i, j, 0))
    outs = [
        jax.ShapeDtypeStruct((b, t, Q_DIFF), BF16),
        jax.ShapeDtypeStruct((b, t, Q_DIFF), F32),
        jax.ShapeDtypeStruct((b, t, Q_DIFF), BF16),
        jax.ShapeDtypeStruct((b, t * H_DIFF, DV_DIFF), F32),
        jax.ShapeDtypeStruct((b, t, D_DIFF), BF16),
        jax.ShapeDtypeStruct((b, t, D_CONV), BF16),
        jax.ShapeDtypeStruct((b, CONV_W - 1, D_CONV), F32),
    ]
    return pl.pallas_call(
        functools.partial(_even_in_kernel, tm=tm, nt=nt),
        grid=(b, nt),
        in_specs=[
            tok(D_MODEL),
            _full((1, D_MODEL)),
            _full((D_MODEL, EVEN_IN)),
            pl.BlockSpec((tm, LANE), lambda i, j: (j, 0)),
            pl.BlockSpec((tm, LANE), lambda i, j: (j, 0)),
            _full((CONV_W, D_CONV)),
            pl.BlockSpec((1, CONV_W - 1, D_CONV), lambda i, j: (i, 0, 0)),
        ],
        out_specs=[tok(Q_DIFF), tok(Q_DIFF), tok(Q_DIFF),
                   pl.BlockSpec((1, tm * H_DIFF, DV_DIFF), lambda i, j: (i, j, 0)), tok(D_DIFF), tok(D_CONV),
                   pl.BlockSpec((1, CONV_W - 1, D_CONV), lambda i, j: (i, 0, 0))],
        out_shape=outs,
        scratch_shapes=[pltpu.VMEM((tm + 2 * SUBLANE, D_CONV), F32)],
        compiler_params=_params("parallel", "arbitrary"),
        name="even_in",
    )(x, g, wb, cos, sin, cw, state)


def _norm_proj_kernel(x_ref, g_ref, w_ref, *out_refs, segs):
    h = _rms(x_ref[0], g_ref[0], EPS).astype(BF16)
    i = 0
    for lo, width, want_f32, want_bf16, scale in segs:
        y = _dot(h, w_ref[0, :, lo:lo + width])
        if scale != 1.0:
            y = y * scale
        if want_f32:
            out_refs[i][0] = y
            i += 1
        if want_bf16:
            out_refs[i][0] = y.astype(BF16)
            i += 1


def _norm_proj(x, g, wb, segs, tm):
    b, t, d = x.shape
    n = wb.shape[-1]
    ng = wb.shape[0]
    nt = t // tm
    outs, specs = [], []
    for lo, width, want_f32, want_bf16, _ in segs:
        for want, dt in ((want_f32, F32), (want_bf16, BF16)):
            if want:
                outs.append(jax.ShapeDtypeStruct((ng * b, t, width), dt))
                specs.append(pl.BlockSpec((1, tm, width), lambda i, j: (i, j, 0)))
    return pl.pallas_call(
        functools.partial(_norm_proj_kernel, segs=segs),
        grid=(ng * b, nt),
        in_specs=[
            pl.BlockSpec((1, tm, d), lambda i, j: (i % b, j, 0)),
            pl.BlockSpec((1, 1, d), lambda i, j: (i // b, 0, 0)),
            pl.BlockSpec((1, d, n), lambda i, j: (i // b, 0, 0)),
        ],
        out_specs=specs,
        out_shape=outs,
        compiler_params=_params("parallel", "parallel"),
        name="norm_proj",
    )(x, g, wb)


def _lambda_full(lq1_ref, lk1_ref, lq2_ref, lk2_ref, lam_init):
    s1 = jnp.sum(lq1_ref[...] * lk1_ref[...], axis=1, keepdims=True)
    s2 = jnp.sum(lq2_ref[...] * lk2_ref[...], axis=1, keepdims=True)
    return jnp.exp(s1) - jnp.exp(s2) + lam_init


def _subln(o, gsub_ref, lam_init):
    return _rms(o, gsub_ref[...], SUBLN_EPS) * (1.0 - lam_init)


def _diff_prompt_kernel(q_ref, k_ref, v_ref, lq1_ref, lk1_ref, lq2_ref, lk2_ref, gsub_ref,
                        o_ref, *, tq, tk, lam_init):
    qi = pl.program_id(2)
    ratio = tk // tq
    q = q_ref[0]
    lane = lax.broadcasted_iota(jnp.int32, (tq, LANE), 1)
    zero = jnp.zeros_like(q)
    qm = (jnp.where(lane < DH_QK, q, zero), jnp.where(lane >= DH_QK, q, zero))
    ones = jnp.ones((tk, LANE), BF16)

    def step(j, carry, masked):
        start = pl.multiple_of(j * tk, tk)
        kb = k_ref[0, pl.ds(start, tk), :]
        vb = jnp.concatenate([v_ref[0, pl.ds(start, tk), :], ones], axis=1)
        if masked:
            r = _chunk_of(lax.broadcasted_iota(jnp.int32, (tq, tk), 0)) + lax.rem(qi, ratio) * (tq // CHUNK)
            c = _chunk_of(lax.broadcasted_iota(jnp.int32, (tq, tk), 1))
            vis = c <= r
        ss = [_dot_nt(qm[mi], kb) for mi in range(2)]
        if masked:
            ss = [jnp.where(vis, s, NEG_INF) for s in ss]
        ms = [jnp.maximum(carry[mi][0], jnp.max(ss[mi], axis=1, keepdims=True)) for mi in range(2)]
        ps = [jnp.exp2(ss[mi] - ms[mi]).astype(BF16) for mi in range(2)]
        pv = [_dot(ps[mi], vb) for mi in range(2)]
        return tuple((ms[mi], jnp.exp2(carry[mi][0] - ms[mi]) * carry[mi][1] + pv[mi]) for mi in range(2))

    init = tuple((jnp.full((tq, 1), NEG_INF, F32), jnp.zeros((tq, 2 * LANE), F32)) for _ in range(2))
    n_full = lax.div(qi, ratio)
    carry = lax.fori_loop(0, n_full, lambda j, c: step(j, c, False), init)
    carry = step(n_full, carry, True)
    lam = _lambda_full(lq1_ref, lk1_ref, lq2_ref, lk2_ref, lam_init)
    (_, a1), (_, a2) = carry
    o = a1[:, :LANE] / a1[:, LANE:] - lam * (a2[:, :LANE] / a2[:, LANE:])
    o_ref[0] = _subln(o, gsub_ref, lam_init).astype(BF16)


def _diff_prompt(q, kb, vb, lq1, lk1, lq2, lk2, gsub, lam_init, tq, tk):
    b, t, _ = q.shape
    lam_spec = _full((1, DH_QK))
    return pl.pallas_call(
        functools.partial(_diff_prompt_kernel, tq=tq, tk=tk, lam_init=lam_init),
        grid=(b, H_DIFF, t // tq),
        in_specs=[
            pl.BlockSpec((1, tq, LANE), lambda i, h, j: (i, j, h)),
            pl.BlockSpec((1, t, LANE), lambda i, h, j: (i, 0, h)),
            pl.BlockSpec((1, t, LANE), lambda i, h, j: (i, 0, h)),
            lam_spec, lam_spec, lam_spec, lam_spec,
            _full((1, DV_DIFF)),
        ],
        out_specs=pl.BlockSpec((1, tq, LANE), lambda i, h, j: (i, j, h)),
        out_shape=jax.ShapeDtypeStruct((b, t, D_DIFF), BF16),
        compiler_params=_params("parallel", "parallel", "arbitrary"),
        name="diff_prompt",
    )(q, kb, vb, lq1, lk1, lq2, lk2, gsub)


def _diff_decode_kernel(q_ref, kp_ref, vp_ref, kn_ref, vn_ref, lq1_ref, lk1_ref, lq2_ref, lk2_ref,
                        gsub_ref, o_ref, qs_ref, m_ref, l_ref, acc_ref, *, t, tk, nk, past_len,
                        lam_init):
    j = pl.program_id(1)
    nsub = 2 * H_DIFF
    rows = nsub * t

    @pl.when(j == 0)
    def _():
        q = q_ref[0]
        lane = lax.broadcasted_iota(jnp.int32, (t, Q_DIFF), 1)
        for s in range(nsub):
            qs_ref[s * t:(s + 1) * t, :] = jnp.where(_lane_band(lane, s, DH_QK), q, jnp.zeros_like(q))
        m_ref[...] = jnp.full(m_ref.shape, NEG_INF, F32)
        l_ref[...] = jnp.zeros(l_ref.shape, F32)
        acc_ref[...] = jnp.zeros(acc_ref.shape, F32)

    def update(kb, vh, k_start, width):
        s = _dot_nt(qs_ref[...], kb)
        q_pos = past_len + _row_query_index(nsub, t, width)
        k_pos = k_start + lax.broadcasted_iota(jnp.int32, (rows, width), 1)
        s = jnp.where(_chunk_of(k_pos) <= _chunk_of(q_pos), s, NEG_INF)
        m = m_ref[...]
        m_new = jnp.maximum(m, jnp.max(s, axis=1, keepdims=True))
        alpha = jnp.exp2(m - m_new)
        p = jnp.exp2(s - m_new)
        l_ref[...] = alpha * l_ref[...] + jnp.sum(p, axis=1, keepdims=True)
        m_ref[...] = m_new
        pb = p.astype(BF16)
        for h in range(H_DIFF):
            rs = slice(2 * h * t, (2 * h + 2) * t)
            acc_ref[rs, :] = alpha[rs] * acc_ref[rs, :] + _dot(pb[rs], vh[h])

    update(kp_ref[0].astype(BF16),
           [vp_ref[0, pl.ds(h, tk, stride=H_DIFF), :].astype(BF16) for h in range(H_DIFF)], j * tk, tk)

    @pl.when(j == nk - 1)
    def _():
        vn = vn_ref[0]
        update(kn_ref[0], [vn[:, h * DV_DIFF:(h + 1) * DV_DIFF] for h in range(H_DIFF)], past_len, t)
        lam = _lambda_full(lq1_ref, lk1_ref, lq2_ref, lk2_ref, lam_init)
        on = acc_ref[...] / l_ref[...]
        for h in range(H_DIFF):
            o = on[2 * h * t:(2 * h + 1) * t] - lam * on[(2 * h + 1) * t:(2 * h + 2) * t]
            o_ref[0, :, h * LANE:(h + 1) * LANE] = _subln(o, gsub_ref, lam_init).astype(BF16)


def _diff_decode(q, k_past, v_past, kn, vn, lq1, lk1, lq2, lk2, gsub, lam_init, tk):
    b, t, _ = q.shape
    past_len = k_past.shape[1]
    nk = past_len // tk
    rows = 2 * H_DIFF * t
    lam_spec = _full((1, DH_QK))
    tokb = lambda w: pl.BlockSpec((1, t, w), lambda i, j: (i, 0, 0))
    return pl.pallas_call(
        functools.partial(_diff_decode_kernel, t=t, tk=tk, nk=nk, past_len=past_len,
                          lam_init=lam_init),
        grid=(b, nk),
        in_specs=[
            tokb(Q_DIFF),
            pl.BlockSpec((1, tk, Q_DIFF), lambda i, j: (i, j, 0)),
            pl.BlockSpec((1, tk * H_DIFF, DV_DIFF), lambda i, j: (i, j, 0)),
            tokb(Q_DIFF), tokb(D_DIFF),
            lam_spec, lam_spec, lam_spec, lam_spec,
            _full((1, DV_DIFF)),
        ],
        out_specs=tokb(D_DIFF),
        out_shape=jax.ShapeDtypeStruct((b, t, D_DIFF), BF16),
        scratch_shapes=[pltpu.VMEM((rows, Q_DIFF), BF16), pltpu.VMEM((rows, 1), F32),
                        pltpu.VMEM((rows, 1), F32), pltpu.VMEM((rows, DV_DIFF), F32)],
        compiler_params=_params("parallel", "arbitrary"),
        name="diff_decode",
    )(q, k_past, v_past, kn, vn, lq1, lk1, lq2, lk2, gsub)


def _softplus(z):
    return jnp.maximum(z, 0.0) + jnp.log(1.0 + jnp.exp2(-jnp.abs(z))) * LOG2E


def _rev_cumsum(sp, tri):
    hi = sp.astype(BF16)
    lo = (sp - hi.astype(F32)).astype(BF16)
    return _dot(hi, tri) + _dot(lo, tri)


def _tri(n):
    r = lax.broadcasted_iota(jnp.int32, (n, n), 0)
    c = lax.broadcasted_iota(jnp.int32, (n, n), 1)
    return jnp.where(r >= c, 1.0, 0.0).astype(BF16)


def _head_ones(width):
    r = lax.broadcasted_iota(jnp.int32, (LANE, LANE), 0)
    c = lax.broadcasted_iota(jnp.int32, (LANE, LANE), 1)
    return jnp.where((r < width) == (c < width), 1.0, 0.0).astype(BF16)


def _head_sq_norms(x_bf16, ones_blk):
    xf = x_bf16.astype(F32)
    return _dot((xf * xf).astype(BF16), ones_blk)


def _sb_prompt_kernel(q_ref, k_ref, v_ref, o_ref, kmax_ref, *, tq, t, ng):
    qi = pl.program_id(2)
    lane = lax.broadcasted_iota(jnp.int32, (tq, LANE), 1)
    tri = _tri(tq)
    ones_blk = _head_ones(DH_SB)
    gs = [slice(g * LANE, (g + 1) * LANE) for g in range(ng)]

    @pl.when(qi == 0)
    def _():
        rows = math.gcd(t, 4 * tq)
        for g in range(ng):
            def body(i, mx, g=g):
                kc = k_ref[0, pl.ds(pl.multiple_of(i * rows, rows), rows), gs[g]]
                return jnp.maximum(mx, _head_sq_norms(kc, ones_blk))
            mx = lax.fori_loop(0, t // rows, body, jnp.zeros((rows, LANE), F32))
            kmax_ref[:, gs[g]] = jnp.max(mx, axis=0, keepdims=True)

    qs, bzs = [], []
    for g in range(ng):
        q = q_ref[0, :, gs[g]]
        zero = jnp.zeros_like(q)
        qs.append(jnp.concatenate([jnp.where(lane < DH_SB, q, zero), jnp.where(lane >= DH_SB, q, zero)], axis=0))
        bound = jnp.sqrt(_head_sq_norms(q, ones_blk) * kmax_ref[:, gs[g]]) * SB_BOUND_SLACK
        bzs += [bound[:, 0:1], bound[:, DH_SB:DH_SB + 1]]
    bz = jnp.concatenate(bzs, axis=0)

    def alive(ccar):
        return (jnp.max(bz - ccar) > -SB_DEAD).astype(jnp.int32)

    def blk(ref, j, g):
        return ref[0, pl.ds(pl.multiple_of(j * tq, tq), tq), gs[g]]

    def scores(j):
        return jnp.concatenate([_dot_nt(qs[g], blk(k_ref, j, g)) for g in range(ng)], axis=0)

    def weighted(w, vals):
        return jnp.concatenate([_dot(w[2 * g * tq:2 * (g + 1) * tq], vals(g)) for g in range(ng)], axis=0)

    has_prev = qi > 0
    jp = jnp.maximum(qi - 1, 0)
    z_r = scores(qi)
    z_l = scores(jp)
    vis = lax.broadcasted_iota(jnp.int32, (2 * ng * tq, tq), 1) < _row_query_index(2 * ng, tq, tq)
    sp_r = jnp.where(vis, _softplus(z_r), 0.0)
    cl_r = _rev_cumsum(sp_r, tri)
    total_r = cl_r[:, 0:1]
    cl_l = _rev_cumsum(_softplus(z_l), tri) + jnp.where(has_prev, total_r, -NEG_INF)
    w_r = jnp.where(vis, jnp.exp2(z_r - cl_r), 0.0).astype(BF16)
    w_l = jnp.exp2(z_l - cl_l).astype(BF16)
    acc = weighted(jnp.concatenate([w_l, w_r], axis=1),
                   lambda g: jnp.concatenate([blk(v_ref, jp, g), blk(v_ref, qi, g)], axis=0))
    ccar = jnp.where(has_prev, cl_l[:, 0:1], total_r)

    def body(c):
        j, _, ccar, acc = c
        z = scores(j)
        cl = _rev_cumsum(_softplus(z), tri)
        w = jnp.exp2(z - ccar - cl).astype(BF16)
        ccar = ccar + cl[:, 0:1]
        return j - 1, alive(ccar), ccar, acc + weighted(w, lambda g: blk(v_ref, j, g))

    out = lax.while_loop(lambda c: jnp.logical_and(c[0] >= 0, c[1] > 0), body,
                         (qi - 2, alive(ccar), ccar, acc))
    acc = out[3]
    for g in range(ng):
        a = acc[2 * g * tq:2 * (g + 1) * tq]
        o_ref[0, :, gs[g]] = jnp.where(lane < DH_SB, a[:tq], a[tq:]).astype(BF16)


def _sb_prompt(q, kb, vb, tq, ng):
    b, t, _ = q.shape
    wg = ng * LANE
    return pl.pallas_call(
        functools.partial(_sb_prompt_kernel, tq=tq, t=t, ng=ng),
        scratch_shapes=[pltpu.VMEM((1, wg), F32)],
        grid=(b, D_SB // wg, t // tq),
        in_specs=[
            pl.BlockSpec((1, tq, wg), lambda i, h, j: (i, j, h)),
            pl.BlockSpec((1, t, wg), lambda i, h, j: (i, 0, h)),
            pl.BlockSpec((1, t, wg), lambda i, h, j: (i, 0, h)),
        ],
        out_specs=pl.BlockSpec((1, tq, wg), lambda i, h, j: (i, j, h)),
        out_shape=jax.ShapeDtypeStruct((b, t, D_SB), BF16),
        compiler_params=_params("parallel", "parallel", "arbitrary"),
        name="sb_prompt",
    )(q, kb, vb)


def _sb_kmax_kernel(k_ref, o_ref):
    j = pl.program_id(1)
    x = k_ref[0]
    ones_blk = _head_ones(DH_SB)
    blk = jnp.concatenate(
        [jnp.max(_head_sq_norms(x[:, g * LANE:(g + 1) * LANE], ones_blk), axis=0, keepdims=True)
         for g in range(D_SB // LANE)], axis=1)

    @pl.when(j == 0)
    def _():
        o_ref[0] = blk

    @pl.when(j > 0)
    def _():
        o_ref[0] = jnp.maximum(o_ref[0], blk)


def _sb_kmax(k_past, tk):
    b, past_len, _ = k_past.shape
    return pl.pallas_call(
        _sb_kmax_kernel,
        grid=(b, past_len // tk),
        in_specs=[pl.BlockSpec((1, tk, D_SB), lambda i, j: (i, j, 0))],
        out_specs=pl.BlockSpec((1, 1, D_SB), lambda i, j: (i, 0, 0)),
        out_shape=jax.ShapeDtypeStruct((b, 1, D_SB), F32),
        compiler_params=_params("parallel", "arbitrary"),
        name="sb_kmax",
    )(k_past)


def _heads(x):
    return [x[:, h * DH_SB:(h + 1) * DH_SB] for h in range(H_SB)]


def _sb_rows_attend(qh, kh, vh, t, width, ccar, acc, vis):
    z = jnp.concatenate([_dot_nt(qh[h], kh[h]) for h in range(H_SB)], axis=0)
    sp = _softplus(z)
    if vis is not None:
        sp = jnp.where(vis, sp, 0.0)
    cl = _rev_cumsum(sp, _tri(width))
    w = jnp.exp2(z - ccar - cl)
    if vis is not None:
        w = jnp.where(vis, w, 0.0)
    w = w.astype(BF16)
    pv = jnp.concatenate([_dot(w[h * t:(h + 1) * t], vh[h]) for h in range(H_SB)], axis=0)
    return ccar + cl[:, 0:1], acc + pv


def _sb_rows_bound(qh, kmax2):
    return jnp.concatenate(
        [jnp.sqrt(jnp.sum(jnp.square(qh[h].astype(F32)), axis=1, keepdims=True)
                  * kmax2[:, h * DH_SB:h * DH_SB + 1]) for h in range(H_SB)], axis=0) * SB_BOUND_SLACK


def _sb_rows_alive(bz, ccar):
    return (jnp.max(bz - ccar) > -SB_DEAD).astype(jnp.int32)


def _sb_rows_store(o_ref, acc, t):
    for h in range(H_SB):
        o_ref[0, :, h * DH_SB:(h + 1) * DH_SB] = acc[h * t:(h + 1) * t, :].astype(BF16)


def _sb_decode_head_kernel(q_ref, kn_ref, vn_ref, kt_ref, vt_ref, kmax_ref,
                           o_ref, c_ref, acc_ref, alive_ref, *, t, tk):
    rows = H_SB * t
    qh = _heads(q_ref[0])
    vis = lax.broadcasted_iota(jnp.int32, (rows, t), 1) < _row_query_index(H_SB, t, t)
    ccar, acc = _sb_rows_attend(qh, _heads(kn_ref[0]), _heads(vn_ref[0]), t, t,
                                jnp.zeros((rows, 1), F32), jnp.zeros((rows, DH_SB), F32), vis)
    ccar, acc = _sb_rows_attend(qh, _heads(kt_ref[0].astype(BF16)), _heads(vt_ref[0].astype(BF16)),
                                t, tk, ccar, acc, None)
    _sb_rows_store(o_ref, acc, t)
    c_ref[0] = ccar
    acc_ref[0] = acc
    alive_ref[0] = jnp.zeros((1, LANE), jnp.int32) + _sb_rows_alive(_sb_rows_bound(qh, kmax_ref[0]), ccar)


def _sb_decode_rest_kernel(q_ref, kmax_ref, c_ref, acc_ref, kp_hbm, vp_hbm, o_ref, kbuf, vbuf, sem,
                           *, t, tk, nk):
    b = pl.program_id(0)
    qh = _heads(q_ref[0])
    bz = _sb_rows_bound(qh, kmax_ref[0])

    def body(c):
        j, _, ccar, acc = c
        start = pl.multiple_of(j * tk, tk)
        ck = pltpu.make_async_copy(kp_hbm.at[b, pl.ds(start, tk), :], kbuf, sem.at[0])
        cv = pltpu.make_async_copy(vp_hbm.at[b, pl.ds(start, tk), :], vbuf, sem.at[1])
        ck.start()
        cv.start()
        ck.wait()
        cv.wait()
        ccar, acc = _sb_rows_attend(qh, _heads(kbuf[...].astype(BF16)), _heads(vbuf[...].astype(BF16)),
                                    t, tk, ccar, acc, None)
        return j - 1, _sb_rows_alive(bz, ccar), ccar, acc

    ccar = c_ref[0]
    out = lax.while_loop(lambda c: jnp.logical_and(c[0] >= 0, c[1] > 0), body,
                         (jnp.int32(nk - 2), _sb_rows_alive(bz, ccar), ccar, acc_ref[0]))
    _sb_rows_store(o_ref, out[3], t)


def _sb_decode(q, k_past, v_past, kn, vn, tk):
    b, t, _ = q.shape
    past_len = k_past.shape[1]
    nk = past_len // tk
    rows = H_SB * t
    k_flat = k_past.reshape(b, past_len, D_SB)
    kmax2 = _sb_kmax(k_flat, min(1024, past_len))
    k_tail = k_past[:, past_len - tk:].reshape(b, tk, D_SB)
    v_tail = v_past[:, past_len - tk:].reshape(b, tk, D_SB)
    tokb = lambda: pl.BlockSpec((1, t, D_SB), lambda i: (i, 0, 0))
    tail = lambda: pl.BlockSpec((1, tk, D_SB), lambda i: (i, 0, 0))
    kmx = lambda: pl.BlockSpec((1, 1, D_SB), lambda i: (i, 0, 0))
    car = lambda: pl.BlockSpec((1, rows, 1), lambda i: (i, 0, 0))
    accs = lambda: pl.BlockSpec((1, rows, DH_SB), lambda i: (i, 0, 0))
    o, ccar, acc, alive = pl.pallas_call(
        functools.partial(_sb_decode_head_kernel, t=t, tk=tk),
        grid=(b,),
        in_specs=[tokb(), tokb(), tokb(), tail(), tail(), kmx()],
        out_specs=[tokb(), car(), accs(), pl.BlockSpec((1, 1, LANE), lambda i: (i, 0, 0))],
        out_shape=[jax.ShapeDtypeStruct((b, t, D_SB), BF16), jax.ShapeDtypeStruct((b, rows, 1), F32),
                   jax.ShapeDtypeStruct((b, rows, DH_SB), F32),
                   jax.ShapeDtypeStruct((b, 1, LANE), jnp.int32)],
        compiler_params=_params("parallel"),
        name="sb_decode_head",
    )(q, kn, vn, k_tail, v_tail, kmax2)
    if nk < 2:
        return o

    def rest(args):
        q_, kmax2_, ccar_, acc_, k_flat_, v_past_, _ = args
        return pl.pallas_call(
            functools.partial(_sb_decode_rest_kernel, t=t, tk=tk, nk=nk),
            grid=(b,),
            in_specs=[tokb(), kmx(), car(), accs(), pl.BlockSpec(memory_space=pl.ANY),
                      pl.BlockSpec(memory_space=pl.ANY)],
            out_specs=tokb(),
            out_shape=jax.ShapeDtypeStruct((b, t, D_SB), BF16),
            scratch_shapes=[pltpu.VMEM((tk, D_SB), F32), pltpu.VMEM((tk, D_SB), F32),
                            pltpu.SemaphoreType.DMA((2,))],
            compiler_params=_params("arbitrary"),
            name="sb_decode_rest",
        )(q_, kmax2_, ccar_, acc_, k_flat_, v_past_.reshape(b, past_len, D_SB))

    return lax.cond(jnp.max(alive) > 0, rest, lambda args: args[-1],
                    (q, kmax2, ccar, acc, k_flat, v_past, o))


def _post_mix_kernel(*refs, n_in):
    x_ref = refs[0]
    a_refs = refs[1:1 + n_in]
    w_ref, g_ref, wq_ref, mk_ref, mv_ref, wo_ref, o_ref = refs[1 + n_in:]
    a = a_refs[0][0] if n_in == 1 else jnp.concatenate([r[0] for r in a_refs], axis=1)
    x = x_ref[0] + _dot(a, w_ref[...])
    hq = _rms(x, g_ref[...], EPS).astype(BF16)
    q = (_dot(hq, wq_ref[...]) * (DH_MEM ** -0.5 * LOG2E)).astype(BF16)
    hs = [slice(h * DH_MEM, (h + 1) * DH_MEM) for h in range(H_MEM)]
    ss = [_dot_nt(q[:, hs[h]], mk_ref[0, :, hs[h]]) for h in range(H_MEM)]
    ps = [jnp.exp2(s - jnp.max(s, axis=1, keepdims=True)) for s in ss]
    ls = [jnp.sum(p, axis=1, keepdims=True) for p in ps]
    oh = [(_dot(ps[h].astype(BF16), mv_ref[0, :, hs[h]]) / ls[h]).astype(BF16) for h in range(H_MEM)]
    o_ref[0] = x + _dot(jnp.concatenate(oh, axis=1), wo_ref[...])


def _post_mix(x, acts, w, g, wq, mk, mv, wo, tm):
    b, t, d = x.shape
    n_in = len(acts)
    tok = lambda w: pl.BlockSpec((1, tm, w), lambda i, j: (i, j, 0))
    mem = pl.BlockSpec((1, N_MEM, D_MODEL), lambda i, j: (i, 0, 0))
    return pl.pallas_call(
        functools.partial(_post_mix_kernel, n_in=n_in),
        grid=(b, t // tm),
        in_specs=([tok(d)] + [tok(a.shape[-1]) for a in acts]
                  + [_full(w.shape), _full((1, d)), _full((d, d)), mem, mem, _full((d, d))]),
        out_specs=tok(d),
        out_shape=jax.ShapeDtypeStruct((b, t, d), F32),
        compiler_params=_params("parallel", "parallel"),
        name="post_mix",
    )(x, *acts, w, g, wq, mk, mv, wo)


def _ffn_kernel(x_ref, g_ref, wu_ref, wd_ref, gf_ref, o_ref, *, fc, final):
    x = x_ref[...]
    h = _rms(x, g_ref[...], EPS).astype(BF16)
    y = x
    for c in range(D_FF // fc):
        u = jnp.maximum(_dot(h, wu_ref[:, c * fc:(c + 1) * fc]), 0.0)
        y = y + _dot((u * u).astype(BF16), wd_ref[c * fc:(c + 1) * fc, :])
    if final:
        y = _rms(y, gf_ref[...], EPS)
    o_ref[...] = y


def _ffn(x2d, g, wu, wd, gf, final, tm, fc):
    m, d = x2d.shape
    return pl.pallas_call(
        functools.partial(_ffn_kernel, fc=fc, final=final),
        grid=(m // tm,),
        in_specs=[pl.BlockSpec((tm, d), lambda i: (i, 0)), _full((1, d)), _full((d, D_FF)),
                  _full((D_FF, d)), _full((1, d))],
        out_specs=pl.BlockSpec((tm, d), lambda i: (i, 0)),
        out_shape=jax.ShapeDtypeStruct((m, d), F32),
        compiler_params=_params("parallel"),
        name="ffn",
    )(x2d, g, wu, wd, gf)


def _rope_tables(pos):
    half = DH_QK // 2
    inv = jnp.power(ROPE_THETA, -jnp.arange(half, dtype=F32) * (2.0 / DH_QK))
    ang = pos.astype(F32)[:, None] * inv[None, :]
    cos, sin = jnp.cos(ang), jnp.sin(ang)
    reps = LANE // DH_QK
    return (jnp.tile(jnp.concatenate([cos, cos], axis=1), (1, reps)),
            jnp.tile(jnp.concatenate([-sin, sin], axis=1), (1, reps)))


def _row(v):
    return v.reshape(1, -1)


def _run(x, mem_kb, mem_vb, past, p, tm, tq):
    b, t, d = x.shape
    past_len = 0 if past is None else past[0].shape[2]
    pos = jnp.arange(past_len, past_len + t)
    cos, sin = _rope_tables(pos)
    outs = {}
    for i in range(DEPTH):
        if i % 2 == 0:
            e = i // 2
            lam_init = 0.8 - 0.6 * math.exp(-0.3 * i)
            state = jnp.zeros((b, CONV_W - 1, D_CONV), F32) if past is None else past[2][e]
            q, k, kb, v, vb, gc, nc = _even_in(x, _row(p['norm_mix'][i]), p['w_in_even_b'][e], cos, sin,
                                               p['conv_w'][e], state, tm)
            lam_args = (_row(p['lambda_q1'][e]), _row(p['lambda_k1'][e]), _row(p['lambda_q2'][e]),
                        _row(p['lambda_k2'][e]), _row(p['subln_gain'][e]))
            if past is None:
                o = _diff_prompt(q, kb, vb, *lam_args, lam_init, min(4 * tq, t), min(1024, t))
            else:
                o = _diff_decode(q, past[0][e].reshape(b, past_len, Q_DIFF),
                                 past[1][e].reshape(b, past_len * H_DIFF, DV_DIFF), kb, vb, *lam_args,
                                 lam_init, min(1024, past_len))
            outs['dk'] = k.reshape(1, b, t, 2 * H_DIFF, DH_QK)
            outs['dv'] = v.reshape(1, b, t, H_DIFF, DV_DIFF)
            outs['conv'] = nc[None]
            acts = [o, gc]
            w_out = p['w_out_even_b'][e]
        else:
            o_idx = i // 2
            segs = ((0, D_SB, False, True, DH_SB ** -0.5 * LOG2E), (D_SB, D_SB, True, True, 1.0),
                    (2 * D_SB, D_SB, True, True, 1.0))
            q, k, kb, v, vb = _norm_proj(x, p['norm_mix'][i].reshape(1, 1, d),
                                         p['w_in_odd_b'][o_idx][None], segs, tm)
            if past is None:
                o = _sb_prompt(q, kb, vb, tq, 2)
            else:
                o = _sb_decode(q, past[3][o_idx], past[4][o_idx], kb, vb, min(256, past_len))
            outs['sk'] = k.reshape(1, b, t, H_SB, DH_SB)
            outs['sv'] = v.reshape(1, b, t, H_SB, DH_SB)
            acts = [o]
            w_out = p['w_out_odd_b'][o_idx]
        x = _post_mix(x, acts, w_out, _row(p['norm_cross'][i]), p['w_q_mem_b'][i], mem_kb[i], mem_vb[i],
                      p['w_o_mem_b'][i], tm)
        m = b * t
        x = _ffn(x.reshape(m, d), _row(p['norm_ffn'][i]), p['w_ffn_up_b'][i], p['w_ffn_down_b'][i],
                 _row(p['norm_final']), i == DEPTH - 1, min(512, m), 1024).reshape(b, t, d)
    return x, outs


def kernel(x_prompt, x_sample, cache_diff_k, cache_diff_v, state_conv, cache_sb_k, cache_sb_v, cache_mem_k, cache_mem_v, mem_prompt, w_in_even, w_out_even, lambda_q1, lambda_k1, lambda_q2, lambda_k2, subln_gain, conv_w, w_in_odd, w_out_odd, norm_mix, norm_mem, norm_cross, w_q_mem, w_k_mem, w_v_mem, w_o_mem, norm_ffn, w_ffn_up, w_ffn_down, norm_final):
    p = dict(norm_mix=norm_mix, norm_cross=norm_cross, norm_ffn=norm_ffn, norm_final=norm_final,
             lambda_q1=lambda_q1, lambda_k1=lambda_k1, lambda_q2=lambda_q2, lambda_k2=lambda_k2,
             subln_gain=subln_gain, conv_w=conv_w)
    for name, w in (('w_in_even', w_in_even), ('w_out_even', w_out_even), ('w_in_odd', w_in_odd),
                    ('w_out_odd', w_out_odd), ('w_q_mem', w_q_mem), ('w_o_mem', w_o_mem),
                    ('w_ffn_up', w_ffn_up), ('w_ffn_down', w_ffn_down)):
        p[name + '_b'] = w.astype(BF16)

    bp = mem_prompt.shape[0]
    w_kv = jnp.concatenate([w_k_mem, w_v_mem], axis=-1).astype(BF16)
    segs = ((0, D_MODEL, True, True, 1.0), (D_MODEL, D_MODEL, True, True, 1.0))
    mk, mkb, mv, mvb = _norm_proj(mem_prompt, norm_mem.reshape(DEPTH, 1, D_MODEL), w_kv, segs, N_MEM)
    p_mem_k = mk.reshape(DEPTH, bp, N_MEM, H_MEM, DH_MEM)
    p_mem_v = mv.reshape(DEPTH, bp, N_MEM, H_MEM, DH_MEM)
    mkb = mkb.reshape(DEPTH, bp, N_MEM, D_MODEL)
    mvb = mvb.reshape(DEPTH, bp, N_MEM, D_MODEL)

    t_p = x_prompt.shape[1]
    y_prompt, po = _run(x_prompt, mkb, mvb, None, p, min(512, t_p), min(256, t_p))

    bs = x_sample.shape[0]
    t_s = x_sample.shape[1]
    cmk = cache_mem_k.reshape(DEPTH, bs, N_MEM, D_MODEL).astype(BF16)
    cmv = cache_mem_v.reshape(DEPTH, bs, N_MEM, D_MODEL).astype(BF16)
    y_sample, so = _run(x_sample, cmk, cmv,
                        (cache_diff_k, cache_diff_v, state_conv, cache_sb_k, cache_sb_v), p, t_s, t_s)

    return (y_prompt, y_sample, po['dk'], po['dv'], po['conv'], po['sk'], po['sv'], p_mem_k, p_mem_v,
            so['dk'], so['dv'], so['conv'], so['sk'], so['sv'])
```

```python
import functools
import math

import jax
import jax.numpy as jnp
from jax import lax
from jax.experimental import pallas as pl
from jax.experimental.pallas import tpu as pltpu

F32 = jnp.float32
BF16 = jnp.bfloat16

D_MODEL = 1024
DEPTH = 2
CHUNK = 64
H_DIFF = 4
DH_QK = 64
DV_DIFF = 2 * DH_QK
D_DIFF = H_DIFF * DV_DIFF
Q_DIFF = 2 * H_DIFF * DH_QK
D_CONV = D_MODEL - D_DIFF
CONV_W = 3
EVEN_IN = 2 * Q_DIFF + D_DIFF + 3 * D_CONV
H_SB = 16
DH_SB = 64
D_SB = H_SB * DH_SB
N_MEM = 256
H_MEM = 4
DH_MEM = D_MODEL // H_MEM
D_FF = 4 * D_MODEL
ROPE_THETA = 10000.0
EPS = 1e-6
SUBLN_EPS = 1e-5
NEG_INF = -1e30
LOG2E = math.log2(math.e)
SB_DEAD = 105.0 * LOG2E
SB_BOUND_SLACK = 1.01

LANE = 128
SUBLANE = 8
VMEM_LIMIT = 52 * 1024 * 1024


def _params(*sem):
    return pltpu.CompilerParams(dimension_semantics=sem, vmem_limit_bytes=VMEM_LIMIT)


def _rms(x, g, eps):
    ms = jnp.mean(x * x, axis=-1, keepdims=True)
    return x * lax.rsqrt(ms + eps) * g


def _dot(a, b):
    return jnp.dot(a, b, preferred_element_type=F32)


def _dot_nt(a, b):
    return lax.dot_general(a, b, (((1,), (1,)), ((), ())), preferred_element_type=F32)


def _full(shape):
    n = len(shape)
    return pl.BlockSpec(shape, lambda *_: (0,) * n)


def _chunk_of(pos):
    shift = CHUNK.bit_length() - 1
    assert 1 << shift == CHUNK
    return lax.shift_right_logical(pos, shift)


def _row_query_index(groups, t, width):
    r = lax.broadcasted_iota(jnp.int32, (t, width), 0)
    return jnp.concatenate([r] * groups, axis=0)


def _lane_band(lane, s, width):
    return (lane >= s * width) & (lane < (s + 1) * width)


def _even_in_kernel(x_ref, g_ref, w_ref, cos_ref, sin_ref, cw_ref, st_ref,
                    q_ref, k_ref, kb_ref, v_ref, vb_ref, gc_ref, nc_ref, ext_ref, *, tm, nt):
    t = pl.program_id(1)
    h = _rms(x_ref[0], g_ref[...], EPS).astype(BF16)

    def proj(lo, width):
        return _dot(h, w_ref[:, lo:lo + width])

    cos = cos_ref[...]
    sin = sin_ref[...]
    lane = lax.broadcasted_iota(jnp.int32, (tm, LANE), 1)
    first_half = (lane & (DH_QK // 2)) == 0

    def rope_group(yj):
        sw = jnp.where(first_half, pltpu.roll(yj, LANE - DH_QK // 2, 1), pltpu.roll(yj, DH_QK // 2, 1))
        return yj * cos + sw * sin

    yq = proj(0, Q_DIFF)
    yk = proj(Q_DIFF, Q_DIFF)
    for j in range(Q_DIFF // LANE):
        sl = slice(j * LANE, (j + 1) * LANE)
        q_ref[0, :, sl] = (rope_group(yq[:, sl]) * (DH_QK ** -0.5 * LOG2E)).astype(BF16)
        kj = rope_group(yk[:, sl])
        k_ref[0, :, sl] = kj
        kb_ref[0, :, sl] = kj.astype(BF16)
    yv = proj(2 * Q_DIFF, D_DIFF)
    for hh in range(H_DIFF):
        v_ref[0, pl.ds(hh, tm, stride=H_DIFF), :] = yv[:, hh * DV_DIFF:(hh + 1) * DV_DIFF]
    vb_ref[0] = yv.astype(BF16)

    base = 2 * Q_DIFF + D_DIFF
    gate_b = proj(base, D_CONV)
    cu = proj(base + D_CONV, D_CONV) * proj(base + 2 * D_CONV, D_CONV)

    @pl.when(t == 0)
    def _():
        ext_ref[SUBLANE - 2:SUBLANE, :] = st_ref[0]

    ext_ref[SUBLANE:SUBLANE + tm, :] = cu
    cw = cw_ref[...]
    conv = (ext_ref[SUBLANE - 2:SUBLANE - 2 + tm, :] * cw[0:1, :]
            + ext_ref[SUBLANE - 1:SUBLANE - 1 + tm, :] * cw[1:2, :]
            + cu * cw[2:3, :])
    gc_ref[0] = (gate_b * conv).astype(BF16)
    ext_ref[0:SUBLANE, :] = ext_ref[tm:tm + SUBLANE, :]

    @pl.when(t == nt - 1)
    def _():
        nc_ref[0] = ext_ref[tm + SUBLANE - 2:tm + SUBLANE, :]


def _even_in(x, g, wb, cos, sin, cw, state, tm):
    b, t, _ = x.shape
    nt = t // tm
    tok = lambda w: pl.BlockSpec((1, tm, w), lambda i, j: (i, j, 0))
    outs = [
        jax.ShapeDtypeStruct((b, t, Q_DIFF), BF16),
        jax.ShapeDtypeStruct((b, t, Q_DIFF), F32),
        jax.ShapeDtypeStruct((b, t, Q_DIFF), BF16),
        jax.ShapeDtypeStruct((b, t * H_DIFF, DV_DIFF), F32),
        jax.ShapeDtypeStruct((b, t, D_DIFF), BF16),
        jax.ShapeDtypeStruct((b, t, D_CONV), BF16),
        jax.ShapeDtypeStruct((b, CONV_W - 1, D_CONV), F32),
    ]
    return pl.pallas_call(
        functools.partial(_even_in_kernel, tm=tm, nt=nt),
        grid=(b, nt),
        in_specs=[
            tok(D_MODEL),
            _full((1, D_MODEL)),
            _full((D_MODEL, EVEN_IN)),
            pl.BlockSpec((tm, LANE), lambda i, j: (j, 0)),
            pl.BlockSpec((tm, LANE), lambda i, j: (j, 0)),
            _full((CONV_W, D_CONV)),
            pl.BlockSpec((1, CONV_W - 1, D_CONV), lambda i, j: (i, 0, 0)),
        ],
        out_specs=[tok(Q_DIFF), tok(Q_DIFF), tok(Q_DIFF),
                   pl.BlockSpec((1, tm * H_DIFF, DV_DIFF), lambda i, j: (i, j, 0)), tok(D_DIFF), tok(D_CONV),
                   pl.BlockSpec((1, CONV_W - 1, D_CONV), lambda i, j: (i, 0, 0))],
        out_shape=outs,
        scratch_shapes=[pltpu.VMEM((tm + 2 * SUBLANE, D_CONV), F32)],
        compiler_params=_params("parallel", "arbitrary"),
        name="even_in",
    )(x, g, wb, cos, sin, cw, state)


def _norm_proj_kernel(x_ref, g_ref, w_ref, *out_refs, segs):
    h = _rms(x_ref[0], g_ref[0], EPS).astype(BF16)
    i = 0
    for lo, width, want_f32, want_bf16, scale in segs:
        y = _dot(h, w_ref[0, :, lo:lo + width])
        if scale != 1.0:
            y = y * scale
        if want_f32:
            out_refs[i][0] = y
            i += 1
        if want_bf16:
            out_refs[i][0] = y.astype(BF16)
            i += 1


def _norm_proj(x, g, wb, segs, tm):
    b, t, d = x.shape
    n = wb.shape[-1]
    ng = wb.shape[0]
    nt = t // tm
    outs, specs = [], []
    for lo, width, want_f32, want_bf16, _ in segs:
        for want, dt in ((want_f32, F32), (want_bf16, BF16)):
            if want:
                outs.append(jax.ShapeDtypeStruct((ng * b, t, width), dt))
                specs.append(pl.BlockSpec((1, tm, width), lambda i, j: (i, j, 0)))
    return pl.pallas_call(
        functools.partial(_norm_proj_kernel, segs=segs),
        grid=(ng * b, nt),
        in_specs=[
            pl.BlockSpec((1, tm, d), lambda i, j: (i % b, j, 0)),
            pl.BlockSpec((1, 1, d), lambda i, j: (i // b, 0, 0)),
            pl.BlockSpec((1, d, n), lambda i, j: (i // b, 0, 0)),
        ],
        out_specs=specs,
        out_shape=outs,
        compiler_params=_params("parallel", "parallel"),
        name="norm_proj",
    )(x, g, wb)


def _lambda_full(lq1_ref, lk1_ref, lq2_ref, lk2_ref, lam_init):
    s1 = jnp.sum(lq1_ref[...] * lk1_ref[...], axis=1, keepdims=True)
    s2 = jnp.sum(lq2_ref[...] * lk2_ref[...], axis=1, keepdims=True)
    return jnp.exp(s1) - jnp.exp(s2) + lam_init


def _subln(o, gsub_ref, lam_init):
    return _rms(o, gsub_ref[...], SUBLN_EPS) * (1.0 - lam_init)


def _diff_prompt_kernel(q_ref, k_ref, v_ref, lq1_ref, lk1_ref, lq2_ref, lk2_ref, gsub_ref,
                        o_ref, *, tq, tk, lam_init):
    qi = pl.program_id(2)
    ratio = tk // tq
    q = q_ref[0]
    lane = lax.broadcasted_iota(jnp.int32, (tq, LANE), 1)
    zero = jnp.zeros_like(q)
    qm = (jnp.where(lane < DH_QK, q, zero), jnp.where(lane >= DH_QK, q, zero))
    ones = jnp.ones((tk, LANE), BF16)

    def step(j, carry, masked):
        start = pl.multiple_of(j * tk, tk)
        kb = k_ref[0, pl.ds(start, tk), :]
        vb = jnp.concatenate([v_ref[0, pl.ds(start, tk), :], ones], axis=1)
        if masked:
            r = _chunk_of(lax.broadcasted_iota(jnp.int32, (tq, tk), 0)) + lax.rem(qi, ratio) * (tq // CHUNK)
            c = _chunk_of(lax.broadcasted_iota(jnp.int32, (tq, tk), 1))
            vis = c <= r
        ss = [_dot_nt(qm[mi], kb) for mi in range(2)]
        if masked:
            ss = [jnp.where(vis, s, NEG_INF) for s in ss]
        ms = [jnp.maximum(carry[mi][0], jnp.max(ss[mi], axis=1, keepdims=True)) for mi in range(2)]
        ps = [jnp.exp2(ss[mi] - ms[mi]).astype(BF16) for mi in range(2)]
        pv = [_dot(ps[mi], vb) for mi in range(2)]
        return tuple((ms[mi], jnp.exp2(carry[mi][0] - ms[mi]) * carry[mi][1] + pv[mi]) for mi in range(2))

    init = tuple((jnp.full((tq, 1), NEG_INF, F32), jnp.zeros((tq, 2 * LANE), F32)) for _ in range(2))
    n_full = lax.div(qi, ratio)
    carry = lax.fori_loop(0, n_full, lambda j, c: step(j, c, False), init)
    carry = step(n_full, carry, True)
    lam = _lambda_full(lq1_ref, lk1_ref, lq2_ref, lk2_ref, lam_init)
    (_, a1), (_, a2) = carry
    o = a1[:, :LANE] / a1[:, LANE:] - lam * (a2[:, :LANE] / a2[:, LANE:])
    o_ref[0] = _subln(o, gsub_ref, lam_init).astype(BF16)


def _diff_prompt(q, kb, vb, lq1, lk1, lq2, lk2, gsub, lam_init, tq, tk):
    b, t, _ = q.shape
    lam_spec = _full((1, DH_QK))
    return pl.pallas_call(
        functools.partial(_diff_prompt_kernel, tq=tq, tk=tk, lam_init=lam_init),
        grid=(b, H_DIFF, t // tq),
        in_specs=[
            pl.BlockSpec((1, tq, LANE), lambda i, h, j: (i, j, h)),
            pl.BlockSpec((1, t, LANE), lambda i, h, j: (i, 0, h)),
            pl.BlockSpec((1, t, LANE), lambda i, h, j: (i, 0, h)),
            lam_spec, lam_spec, lam_spec, lam_spec,
            _full((1, DV_DIFF)),
        ],
        out_specs=pl.BlockSpec((1, tq, LANE), lambda i, h, j: (i, j, h)),
        out_shape=jax.ShapeDtypeStruct((b, t, D_DIFF), BF16),
        compiler_params=_params("parallel", "parallel", "arbitrary"),
        name="diff_prompt",
    )(q, kb, vb, lq1, lk1, lq2, lk2, gsub)


def _diff_decode_kernel(q_ref, kp_ref, vp_ref, kn_ref, vn_ref, lq1_ref, lk1_ref, lq2_ref, lk2_ref,
                        gsub_ref, o_ref, qs_ref, m_ref, l_ref, acc_ref, *, t, tk, nk, past_len,
                        lam_init):
    j = pl.program_id(1)
    nsub = 2 * H_DIFF
    rows = nsub * t

    @pl.when(j == 0)
    def _():
        q = q_ref[0]
        lane = lax.broadcasted_iota(jnp.int32, (t, Q_DIFF), 1)
        for s in range(nsub):
            qs_ref[s * t:(s + 1) * t, :] = jnp.where(_lane_band(lane, s, DH_QK), q, jnp.zeros_like(q))
        m_ref[...] = jnp.full(m_ref.shape, NEG_INF, F32)
        l_ref[...] = jnp.zeros(l_ref.shape, F32)
        acc_ref[...] = jnp.zeros(acc_ref.shape, F32)

    def update(kb, vh, k_start, width):
        s = _dot_nt(qs_ref[...], kb)
        q_pos = past_len + _row_query_index(nsub, t, width)
        k_pos = k_start + lax.broadcasted_iota(jnp.int32, (rows, width), 1)
        s = jnp.where(_chunk_of(k_pos) <= _chunk_of(q_pos), s, NEG_INF)
        m = m_ref[...]
        m_new = jnp.maximum(m, jnp.max(s, axis=1, keepdims=True))
        alpha = jnp.exp2(m - m_new)
        p = jnp.exp2(s - m_new)
        l_ref[...] = alpha * l_ref[...] + jnp.sum(p, axis=1, keepdims=True)
        m_ref[...] = m_new
        pb = p.astype(BF16)
        for h in range(H_DIFF):
            rs = slice(2 * h * t, (2 * h + 2) * t)
            acc_ref[rs, :] = alpha[rs] * acc_ref[rs, :] + _dot(pb[rs], vh[h])

    update(kp_ref[0].astype(BF16),
           [vp_ref[0, pl.ds(h, tk, stride=H_DIFF), :].astype(BF16) for h in range(H_DIFF)], j * tk, tk)

    @pl.when(j == nk - 1)
    def _():
        vn = vn_ref[0]
        update(kn_ref[0], [vn[:, h * DV_DIFF:(h + 1) * DV_DIFF] for h in range(H_DIFF)], past_len, t)
        lam = _lambda_full(lq1_ref, lk1_ref, lq2_ref, lk2_ref, lam_init)
        on = acc_ref[...] / l_ref[...]
        for h in range(H_DIFF):
            o = on[2 * h * t:(2 * h + 1) * t] - lam * on[(2 * h + 1) * t:(2 * h + 2) * t]
            o_ref[0, :, h * LANE:(h + 1) * LANE] = _subln(o, gsub_ref, lam_init).astype(BF16)


def _diff_decode(q, k_past, v_past, kn, vn, lq1, lk1, lq2, lk2, gsub, lam_init, tk):
    b, t, _ = q.shape
    past_len = k_past.shape[1]
    nk = past_len // tk
    rows = 2 * H_DIFF * t
    lam_spec = _full((1, DH_QK))
    tokb = lambda w: pl.BlockSpec((1, t, w), lambda i, j: (i, 0, 0))
    return pl.pallas_call(
        functools.partial(_diff_decode_kernel, t=t, tk=tk, nk=nk, past_len=past_len,
                          lam_init=lam_init),
        grid=(b, nk),
        in_specs=[
            tokb(Q_DIFF),
            pl.BlockSpec((1, tk, Q_DIFF), lambda i, j: (i, j, 0)),
            pl.BlockSpec((1, tk * H_DIFF, DV_DIFF), lambda i, j: (i, j, 0)),
            tokb(Q_DIFF), tokb(D_DIFF),
            lam_spec, lam_spec, lam_spec, lam_spec,
            _full((1, DV_DIFF)),
        ],
        out_specs=tokb(D_DIFF),
        out_shape=jax.ShapeDtypeStruct((b, t, D_DIFF), BF16),
        scratch_shapes=[pltpu.VMEM((rows, Q_DIFF), BF16), pltpu.VMEM((rows, 1), F32),
                        pltpu.VMEM((rows, 1), F32), pltpu.VMEM((rows, DV_DIFF), F32)],
        compiler_params=_params("parallel", "arbitrary"),
        name="diff_decode",
    )(q, k_past, v_past, kn, vn, lq1, lk1, lq2, lk2, gsub)


def _softplus(z):
    return jnp.maximum(z, 0.0) + jnp.log(1.0 + jnp.exp2(-jnp.abs(z))) * LOG2E


def _rev_cumsum(sp, tri):
    hi = sp.astype(BF16)
    lo = (sp - hi.astype(F32)).astype(BF16)
    return _dot(hi, tri) + _dot(lo, tri)


def _tri(n):
    r = lax.broadcasted_iota(jnp.int32, (n, n), 0)
    c = lax.broadcasted_iota(jnp.int32, (n, n), 1)
    return jnp.where(r >= c, 1.0, 0.0).astype(BF16)


def _head_ones(width):
    r = lax.broadcasted_iota(jnp.int32, (LANE, LANE), 0)
    c = lax.broadcasted_iota(jnp.int32, (LANE, LANE), 1)
    return jnp.where((r < width) == (c < width), 1.0, 0.0).astype(BF16)


def _head_sq_norms(x_bf16, ones_blk):
    xf = x_bf16.astype(F32)
    return _dot((xf * xf).astype(BF16), ones_blk)


def _sb_prompt_kernel(q_ref, k_ref, v_ref, o_ref, kmax_ref, *, tq, t, ng):
    qi = pl.program_id(2)
    lane = lax.broadcasted_iota(jnp.int32, (tq, LANE), 1)
    tri = _tri(tq)
    ones_blk = _head_ones(DH_SB)
    gs = [slice(g * LANE, (g + 1) * LANE) for g in range(ng)]

    @pl.when(qi == 0)
    def _():
        rows = math.gcd(t, 4 * tq)
        for g in range(ng):
            def body(i, mx, g=g):
                kc = k_ref[0, pl.ds(pl.multiple_of(i * rows, rows), rows), gs[g]]
                return jnp.maximum(mx, _head_sq_norms(kc, ones_blk))
            mx = lax.fori_loop(0, t // rows, body, jnp.zeros((rows, LANE), F32))
            kmax_ref[:, gs[g]] = jnp.max(mx, axis=0, keepdims=True)

    qs, bzs = [], []
    for g in range(ng):
        q = q_ref[0, :, gs[g]]
        zero = jnp.zeros_like(q)
        qs.append(jnp.concatenate([jnp.where(lane < DH_SB, q, zero), jnp.where(lane >= DH_SB, q, zero)], axis=0))
        bound = jnp.sqrt(_head_sq_norms(q, ones_blk) * kmax_ref[:, gs[g]]) * SB_BOUND_SLACK
        bzs += [bound[:, 0:1], bound[:, DH_SB:DH_SB + 1]]
    bz = jnp.concatenate(bzs, axis=0)

    def alive(ccar):
        return (jnp.max(bz - ccar) > -SB_DEAD).astype(jnp.int32)

    def blk(ref, j, g):
        return ref[0, pl.ds(pl.multiple_of(j * tq, tq), tq), gs[g]]

    def scores(j):
        return jnp.concatenate([_dot_nt(qs[g], blk(k_ref, j, g)) for g in range(ng)], axis=0)

    def weighted(w, vals):
        return jnp.concatenate([_dot(w[2 * g * tq:2 * (g + 1) * tq], vals(g)) for g in range(ng)], axis=0)

    has_prev = qi > 0
    jp = jnp.maximum(qi - 1, 0)
    z_r = scores(qi)
    z_l = scores(jp)
    vis = lax.broadcasted_iota(jnp.int32, (2 * ng * tq, tq), 1) < _row_query_index(2 * ng, tq, tq)
    sp_r = jnp.where(vis, _softplus(z_r), 0.0)
    cl_r = _rev_cumsum(sp_r, tri)
    total_r = cl_r[:, 0:1]
    cl_l = _rev_cumsum(_softplus(z_l), tri) + jnp.where(has_prev, total_r, -NEG_INF)
    w_r = jnp.where(vis, jnp.exp2(z_r - cl_r), 0.0).astype(BF16)
    w_l = jnp.exp2(z_l - cl_l).astype(BF16)
    acc = weighted(jnp.concatenate([w_l, w_r], axis=1),
                   lambda g: jnp.concatenate([blk(v_ref, jp, g), blk(v_ref, qi, g)], axis=0))
    ccar = jnp.where(has_prev, cl_l[:, 0:1], total_r)

    def body(c):
        j, _, ccar, acc = c
        z = scores(j)
        cl = _rev_cumsum(_softplus(z), tri)
        w = jnp.exp2(z - ccar - cl).astype(BF16)
        ccar = ccar + cl[:, 0:1]
        return j - 1, alive(ccar), ccar, acc + weighted(w, lambda g: blk(v_ref, j, g))

    out = lax.while_loop(lambda c: jnp.logical_and(c[0] >= 0, c[1] > 0), body,
                         (qi - 2, alive(ccar), ccar, acc))
    acc = out[3]
    for g in range(ng):
        a = acc[2 * g * tq:2 * (g + 1) * tq]
        o_ref[0, :, gs[g]] = jnp.where(lane < DH_SB, a[:tq], a[tq:]).astype(BF16)


def _sb_prompt(q, kb, vb, tq, ng):
    b, t, _ = q.shape
    wg = ng * LANE
    return pl.pallas_call(
        functools.partial(_sb_prompt_kernel, tq=tq, t=t, ng=ng),
        scratch_shapes=[pltpu.VMEM((1, wg), F32)],
        grid=(b, D_SB // wg, t // tq),
        in_specs=[
            pl.BlockSpec((1, tq, wg), lambda i, h, j: (i, j, h)),
            pl.BlockSpec((1, t, wg), lambda i, h, j: (i, 0, h)),
            pl.BlockSpec((1, t, wg), lambda i, h, j: (i, 0, h)),
        ],
        out_specs=pl.BlockSpec((1, tq, wg), lambda i, h, j: (i, j, h)),
        out_shape=jax.ShapeDtypeStruct((b, t, D_SB), BF16),
        compiler_params=_params("parallel", "parallel", "arbitrary"),
        name="sb_prompt",
    )(q, kb, vb)


def _sb_kmax_kernel(k_ref, o_ref):
    j = pl.program_id(1)
    x = k_ref[0]
    ones_blk = _head_ones(DH_SB)
    blk = jnp.concatenate(
        [jnp.max(_head_sq_norms(x[:, g * LANE:(g + 1) * LANE], ones_blk), axis=0, keepdims=True)
         for g in range(D_SB // LANE)], axis=1)

    @pl.when(j == 0)
    def _():
        o_ref[0] = blk

    @pl.when(j > 0)
    def _():
        o_ref[0] = jnp.maximum(o_ref[0], blk)


def _sb_kmax(k_past, tk):
    b, past_len, _ = k_past.shape
    return pl.pallas_call(
        _sb_kmax_kernel,
        grid=(b, past_len // tk),
        in_specs=[pl.BlockSpec((1, tk, D_SB), lambda i, j: (i, j, 0))],
        out_specs=pl.BlockSpec((1, 1, D_SB), lambda i, j: (i, 0, 0)),
        out_shape=jax.ShapeDtypeStruct((b, 1, D_SB), F32),
        compiler_params=_params("parallel", "arbitrary"),
        name="sb_kmax",
    )(k_past)


def _heads(x):
    return [x[:, h * DH_SB:(h + 1) * DH_SB] for h in range(H_SB)]


def _sb_rows_attend(qh, kh, vh, t, width, ccar, acc, vis):
    z = jnp.concatenate([_dot_nt(qh[h], kh[h]) for h in range(H_SB)], axis=0)
    sp = _softplus(z)
    if vis is not None:
        sp = jnp.where(vis, sp, 0.0)
    cl = _rev_cumsum(sp, _tri(width))
    w = jnp.exp2(z - ccar - cl)
    if vis is not None:
        w = jnp.where(vis, w, 0.0)
    w = w.astype(BF16)
    pv = jnp.concatenate([_dot(w[h * t:(h + 1) * t], vh[h]) for h in range(H_SB)], axis=0)
    return ccar + cl[:, 0:1], acc + pv


def _sb_rows_bound(qh, kmax2):
    return jnp.concatenate(
        [jnp.sqrt(jnp.sum(jnp.square(qh[h].astype(F32)), axis=1, keepdims=True)
                  * kmax2[:, h * DH_SB:h * DH_SB + 1]) for h in range(H_SB)], axis=0) * SB_BOUND_SLACK


def _sb_rows_alive(bz, ccar):
    return (jnp.max(bz - ccar) > -SB_DEAD).astype(jnp.int32)


def _sb_rows_store(o_ref, acc, t):
    for h in range(H_SB):
        o_ref[0, :, h * DH_SB:(h + 1) * DH_SB] = acc[h * t:(h + 1) * t, :].astype(BF16)


def _sb_decode_head_kernel(q_ref, kn_ref, vn_ref, kt_ref, vt_ref, kmax_ref,
                           o_ref, c_ref, acc_ref, alive_ref, *, t, tk):
    rows = H_SB * t
    qh = _heads(q_ref[0])
    vis = lax.broadcasted_iota(jnp.int32, (rows, t), 1) < _row_query_index(H_SB, t, t)
    ccar, acc = _sb_rows_attend(qh, _heads(kn_ref[0]), _heads(vn_ref[0]), t, t,
                                jnp.zeros((rows, 1), F32), jnp.zeros((rows, DH_SB), F32), vis)
    ccar, acc = _sb_rows_attend(qh, _heads(kt_ref[0].astype(BF16)), _heads(vt_ref[0].astype(BF16)),
                                t, tk, ccar, acc, None)
    _sb_rows_store(o_ref, acc, t)
    c_ref[0] = ccar
    acc_ref[0] = acc
    alive_ref[0] = jnp.zeros((1, LANE), jnp.int32) + _sb_rows_alive(_sb_rows_bound(qh, kmax_ref[0]), ccar)


def _sb_decode_rest_kernel(q_ref, kmax_ref, c_ref, acc_ref, kp_hbm, vp_hbm, o_ref, kbuf, vbuf, sem,
                           *, t, tk, nk):
    b = pl.program_id(0)
    qh = _heads(q_ref[0])
    bz = _sb_rows_bound(qh, kmax_ref[0])

    def body(c):
        j, _, ccar, acc = c
        start = pl.multiple_of(j * tk, tk)
        ck = pltpu.make_async_copy(kp_hbm.at[b, pl.ds(start, tk), :], kbuf, sem.at[0])
        cv = pltpu.make_async_copy(vp_hbm.at[b, pl.ds(start, tk), :], vbuf, sem.at[1])
        ck.start()
        cv.start()
        ck.wait()
        cv.wait()
        ccar, acc = _sb_rows_attend(qh, _heads(kbuf[...].astype(BF16)), _heads(vbuf[...].astype(BF16)),
                                    t, tk, ccar, acc, None)
        return j - 1, _sb_rows_alive(bz, ccar), ccar, acc

    ccar = c_ref[0]
    out = lax.while_loop(lambda c: jnp.logical_and(c[0] >= 0, c[1] > 0), body,
                         (jnp.int32(nk - 2), _sb_rows_alive(bz, ccar), ccar, acc_ref[0]))
    _sb_rows_store(o_ref, out[3], t)


def _sb_decode(q, k_past, v_past, kn, vn, tk):
    b, t, _ = q.shape
    past_len = k_past.shape[1]
    nk = past_len // tk
    rows = H_SB * t
    k_flat = k_past.reshape(b, past_len, D_SB)
    kmax2 = _sb_kmax(k_flat, min(2048, past_len))
    k_tail = k_past[:, past_len - tk:].reshape(b, tk, D_SB)
    v_tail = v_past[:, past_len - tk:].reshape(b, tk, D_SB)
    tokb = lambda: pl.BlockSpec((1, t, D_SB), lambda i: (i, 0, 0))
    tail = lambda: pl.BlockSpec((1, tk, D_SB), lambda i: (i, 0, 0))
    kmx = lambda: pl.BlockSpec((1, 1, D_SB), lambda i: (i, 0, 0))
    car = lambda: pl.BlockSpec((1, rows, 1), lambda i: (i, 0, 0))
    accs = lambda: pl.BlockSpec((1, rows, DH_SB), lambda i: (i, 0, 0))
    o, ccar, acc, alive = pl.pallas_call(
        functools.partial(_sb_decode_head_kernel, t=t, tk=tk),
        grid=(b,),
        in_specs=[tokb(), tokb(), tokb(), tail(), tail(), kmx()],
        out_specs=[tokb(), car(), accs(), pl.BlockSpec((1, 1, LANE), lambda i: (i, 0, 0))],
        out_shape=[jax.ShapeDtypeStruct((b, t, D_SB), BF16), jax.ShapeDtypeStruct((b, rows, 1), F32),
                   jax.ShapeDtypeStruct((b, rows, DH_SB), F32),
                   jax.ShapeDtypeStruct((b, 1, LANE), jnp.int32)],
        compiler_params=_params("parallel"),
        name="sb_decode_head",
    )(q, kn, vn, k_tail, v_tail, kmax2)
    if nk < 2:
        return o

    def rest(args):
        q_, kmax2_, ccar_, acc_, k_flat_, v_past_, _ = args
        return pl.pallas_call(
            functools.partial(_sb_decode_rest_kernel, t=t, tk=tk, nk=nk),
            grid=(b,),
            in_specs=[tokb(), kmx(), car(), accs(), pl.BlockSpec(memory_space=pl.ANY),
                      pl.BlockSpec(memory_space=pl.ANY)],
            out_specs=tokb(),
            out_shape=jax.ShapeDtypeStruct((b, t, D_SB), BF16),
            scratch_shapes=[pltpu.VMEM((tk, D_SB), F32), pltpu.VMEM((tk, D_SB), F32),
                            pltpu.SemaphoreType.DMA((2,))],
            compiler_params=_params("arbitrary"),
            name="sb_decode_rest",
        )(q_, kmax2_, ccar_, acc_, k_flat_, v_past_.reshape(b, past_len, D_SB))

    return lax.cond(jnp.max(alive) > 0, rest, lambda args: args[-1],
                    (q, kmax2, ccar, acc, k_flat, v_past, o))


def _post_mix_kernel(*refs, n_in):
    x_ref = refs[0]
    a_refs = refs[1:1 + n_in]
    w_ref, g_ref, wq_ref, mk_ref, mv_ref, wo_ref, o_ref = refs[1 + n_in:]
    a = a_refs[0][0] if n_in == 1 else jnp.concatenate([r[0] for r in a_refs], axis=1)
    x = x_ref[0] + _dot(a, w_ref[...])
    hq = _rms(x, g_ref[...], EPS).astype(BF16)
    q = (_dot(hq, wq_ref[...]) * (DH_MEM ** -0.5 * LOG2E)).astype(BF16)
    hs = [slice(h * DH_MEM, (h + 1) * DH_MEM) for h in range(H_MEM)]
    ss = [_dot_nt(q[:, hs[h]], mk_ref[0, :, hs[h]]) for h in range(H_MEM)]
    ps = [jnp.exp2(s - jnp.max(s, axis=1, keepdims=True)) for s in ss]
    ls = [jnp.sum(p, axis=1, keepdims=True) for p in ps]
    oh = [(_dot(ps[h].astype(BF16), mv_ref[0, :, hs[h]]) / ls[h]).astype(BF16) for h in range(H_MEM)]
    o_ref[0] = x + _dot(jnp.concatenate(oh, axis=1), wo_ref[...])


def _post_mix(x, acts, w, g, wq, mk, mv, wo, tm):
    b, t, d = x.shape
    n_in = len(acts)
    tok = lambda w: pl.BlockSpec((1, tm, w), lambda i, j: (i, j, 0))
    mem = pl.BlockSpec((1, N_MEM, D_MODEL), lambda i, j: (i, 0, 0))
    return pl.pallas_call(
        functools.partial(_post_mix_kernel, n_in=n_in),
        grid=(b, t // tm),
        in_specs=([tok(d)] + [tok(a.shape[-1]) for a in acts]
                  + [_full(w.shape), _full((1, d)), _full((d, d)), mem, mem, _full((d, d))]),
        out_specs=tok(d),
        out_shape=jax.ShapeDtypeStruct((b, t, d), F32),
        compiler_params=_params("parallel", "parallel"),
        name="post_mix",
    )(x, *acts, w, g, wq, mk, mv, wo)


def _ffn_kernel(x_ref, g_ref, wu_ref, wd_ref, gf_ref, o_ref, *, fc, final):
    x = x_ref[...]
    h = _rms(x, g_ref[...], EPS).astype(BF16)
    y = x
    for c in range(D_FF // fc):
        u = jnp.maximum(_dot(h, wu_ref[:, c * fc:(c + 1) * fc]), 0.0)
        y = y + _dot((u * u).astype(BF16), wd_ref[c * fc:(c + 1) * fc, :])
    if final:
        y = _rms(y, gf_ref[...], EPS)
    o_ref[...] = y


def _ffn(x2d, g, wu, wd, gf, final, tm, fc):
    m, d = x2d.shape
    return pl.pallas_call(
        functools.partial(_ffn_kernel, fc=fc, final=final),
        grid=(m // tm,),
        in_specs=[pl.BlockSpec((tm, d), lambda i: (i, 0)), _full((1, d)), _full((d, D_FF)),
                  _full((D_FF, d)), _full((1, d))],
        out_specs=pl.BlockSpec((tm, d), lambda i: (i, 0)),
        out_shape=jax.ShapeDtypeStruct((m, d), F32),
        compiler_params=_params("parallel"),
        name="ffn",
    )(x2d, g, wu, wd, gf)


def _rope_tables(pos):
    half = DH_QK // 2
    inv = jnp.power(ROPE_THETA, -jnp.arange(half, dtype=F32) * (2.0 / DH_QK))
    ang = pos.astype(F32)[:, None] * inv[None, :]
    cos, sin = jnp.cos(ang), jnp.sin(ang)
    reps = LANE // DH_QK
    return (jnp.tile(jnp.concatenate([cos, cos], axis=1), (1, reps)),
            jnp.tile(jnp.concatenate([-sin, sin], axis=1), (1, reps)))


def _row(v):
    return v.reshape(1, -1)


def _run(x, mem_kb, mem_vb, past, p, tm, tq):
    b, t, d = x.shape
    past_len = 0 if past is None else past[0].shape[2]
    pos = jnp.arange(past_len, past_len + t)
    cos, sin = _rope_tables(pos)
    outs = {}
    for i in range(DEPTH):
        if i % 2 == 0:
            e = i // 2
            lam_init = 0.8 - 0.6 * math.exp(-0.3 * i)
            state = jnp.zeros((b, CONV_W - 1, D_CONV), F32) if past is None else past[2][e]
            q, k, kb, v, vb, gc, nc = _even_in(x, _row(p['norm_mix'][i]), p['w_in_even_b'][e], cos, sin,
                                               p['conv_w'][e], state, tm)
            lam_args = (_row(p['lambda_q1'][e]), _row(p['lambda_k1'][e]), _row(p['lambda_q2'][e]),
                        _row(p['lambda_k2'][e]), _row(p['subln_gain'][e]))
            if past is None:
                o = _diff_prompt(q, kb, vb, *lam_args, lam_init, min(4 * tq, t), min(1024, t))
            else:
                o = _diff_decode(q, past[0][e].reshape(b, past_len, Q_DIFF),
                                 past[1][e].reshape(b, past_len * H_DIFF, DV_DIFF), kb, vb, *lam_args,
                                 lam_init, min(2048, past_len))
            outs['dk'] = k.reshape(1, b, t, 2 * H_DIFF, DH_QK)
            outs['dv'] = v.reshape(1, b, t, H_DIFF, DV_DIFF)
            outs['conv'] = nc[None]
            acts = [o, gc]
            w_out = p['w_out_even_b'][e]
        else:
            o_idx = i // 2
            segs = ((0, D_SB, False, True, DH_SB ** -0.5 * LOG2E), (D_SB, D_SB, True, True, 1.0),
                    (2 * D_SB, D_SB, True, True, 1.0))
            q, k, kb, v, vb = _norm_proj(x, p['norm_mix'][i].reshape(1, 1, d),
                                         p['w_in_odd_b'][o_idx][None], segs, tm)
            if past is None:
                o = _sb_prompt(q, kb, vb, tq, 2)
            else:
                o = _sb_decode(q, past[3][o_idx], past[4][o_idx], kb, vb, min(256, past_len))
            outs['sk'] = k.reshape(1, b, t, H_SB, DH_SB)
            outs['sv'] = v.reshape(1, b, t, H_SB, DH_SB)
            acts = [o]
            w_out = p['w_out_odd_b'][o_idx]
        x = _post_mix(x, acts, w_out, _row(p['norm_cross'][i]), p['w_q_mem_b'][i], mem_kb[i], mem_vb[i],
                      p['w_o_mem_b'][i], tm)
        m = b * t
        x = _ffn(x.reshape(m, d), _row(p['norm_ffn'][i]), p['w_ffn_up_b'][i], p['w_ffn_down_b'][i],
                 _row(p['norm_final']), i == DEPTH - 1, min(512, m), 1024).reshape(b, t, d)
    return x, outs


def kernel(x_prompt, x_sample, cache_diff_k, cache_diff_v, state_conv, cache_sb_k, cache_sb_v, cache_mem_k, cache_mem_v, mem_prompt, w_in_even, w_out_even, lambda_q1, lambda_k1, lambda_q2, lambda_k2, subln_gain, conv_w, w_in_odd, w_out_odd, norm_mix, norm_mem, norm_cross, w_q_mem, w_k_mem, w_v_mem, w_o_mem, norm_ffn, w_ffn_up, w_ffn_down, norm_final):
    p = dict(norm_mix=norm_mix, norm_cross=norm_cross, norm_ffn=norm_ffn, norm_final=norm_final,
             lambda_q1=lambda_q1, lambda_k1=lambda_k1, lambda_q2=lambda_q2, lambda_k2=lambda_k2,
             subln_gain=subln_gain, conv_w=conv_w)
    for name, w in (('w_in_even', w_in_even), ('w_out_even', w_out_even), ('w_in_odd', w_in_odd),
                    ('w_out_odd', w_out_odd), ('w_q_mem', w_q_mem), ('w_o_mem', w_o_mem),
                    ('w_ffn_up', w_ffn_up), ('w_ffn_down', w_ffn_down)):
        p[name + '_b'] = w.astype(BF16)

    bp = mem_prompt.shape[0]
    w_kv = jnp.concatenate([w_k_mem, w_v_mem], axis=-1).astype(BF16)
    segs = ((0, D_MODEL, True, True, 1.0), (D_MODEL, D_MODEL, True, True, 1.0))
    mk, mkb, mv, mvb = _norm_proj(mem_prompt, norm_mem.reshape(DEPTH, 1, D_MODEL), w_kv, segs, N_MEM)
    p_mem_k = mk.reshape(DEPTH, bp, N_MEM, H_MEM, DH_MEM)
    p_mem_v = mv.reshape(DEPTH, bp, N_MEM, H_MEM, DH_MEM)
    mkb = mkb.reshape(DEPTH, bp, N_MEM, D_MODEL)
    mvb = mvb.reshape(DEPTH, bp, N_MEM, D_MODEL)

    t_p = x_prompt.shape[1]
    y_prompt, po = _run(x_prompt, mkb, mvb, None, p, min(512, t_p), min(256, t_p))

    bs = x_sample.shape[0]
    t_s = x_sample.shape[1]
    cmk = cache_mem_k.reshape(DEPTH, bs, N_MEM, D_MODEL).astype(BF16)
    cmv = cache_mem_v.reshape(DEPTH, bs, N_MEM, D_MODEL).astype(BF16)
    y_sample, so = _run(x_sample, cmk, cmv,
                        (cache_diff_k, cache_diff_v, state_conv, cache_sb_k, cache_sb_v), p, t_s, t_s)

    return (y_prompt, y_sample, po['dk'], po['dv'], po['conv'], po['sk'], po['sv'], p_mem_k, p_mem_v,
            so['dk'], so['dv'], so['conv'], so['sk'], so['sv'])
```

```python
import functools
import math

import jax
import jax.numpy as jnp
from jax import lax
from jax.experimental import pallas as pl
from jax.experimental.pallas import tpu as pltpu

F32 = jnp.float32
BF16 = jnp.bfloat16

D_MODEL = 1024
DEPTH = 2
CHUNK = 64
H_DIFF = 4
DH_QK = 64
DV_DIFF = 2 * DH_QK
D_DIFF = H_DIFF * DV_DIFF
Q_DIFF = 2 * H_DIFF * DH_QK
D_CONV = D_MODEL - D_DIFF
CONV_W = 3
EVEN_IN = 2 * Q_DIFF + D_DIFF + 3 * D_CONV
H_SB = 16
DH_SB = 64
D_SB = H_SB * DH_SB
N_MEM = 256
H_MEM = 4
DH_MEM = D_MODEL // H_MEM
D_FF = 4 * D_MODEL
ROPE_THETA = 10000.0
EPS = 1e-6
SUBLN_EPS = 1e-5
NEG_INF = -1e30
LOG2E = math.log2(math.e)
SB_DEAD = 105.0 * LOG2E
SB_BOUND_SLACK = 1.01

LANE = 128
SUBLANE = 8
VMEM_LIMIT = 52 * 1024 * 1024


def _params(*sem):
    return pltpu.CompilerParams(dimension_semantics=sem, vmem_limit_bytes=VMEM_LIMIT)


def _rms(x, g, eps):
    ms = jnp.mean(x * x, axis=-1, keepdims=True)
    return x * lax.rsqrt(ms + eps) * g


def _dot(a, b):
    return jnp.dot(a, b, preferred_element_type=F32)


def _dot_nt(a, b):
    return lax.dot_general(a, b, (((1,), (1,)), ((), ())), preferred_element_type=F32)


def _full(shape):
    n = len(shape)
    return pl.BlockSpec(shape, lambda *_: (0,) * n)


def _chunk_of(pos):
    shift = CHUNK.bit_length() - 1
    assert 1 << shift == CHUNK
    return lax.shift_right_logical(pos, shift)


def _row_query_index(groups, t, width):
    r = lax.broadcasted_iota(jnp.int32, (t, width), 0)
    return jnp.concatenate([r] * groups, axis=0)


def _lane_band(lane, s, width):
    return (lane >= s * width) & (lane < (s + 1) * width)


def _even_in_kernel(x_ref, g_ref, w_ref, cos_ref, sin_ref, cw_ref, st_ref,
                    q_ref, k_ref, kb_ref, v_ref, vb_ref, gc_ref, nc_ref, ext_ref, *, tm, nt):
    t = pl.program_id(1)
    h = _rms(x_ref[0], g_ref[...], EPS).astype(BF16)

    def proj(lo, width):
        return _dot(h, w_ref[:, lo:lo + width])

    cos = cos_ref[...]
    sin = sin_ref[...]
    lane = lax.broadcasted_iota(jnp.int32, (tm, LANE), 1)
    first_half = (lane & (DH_QK // 2)) == 0

    def rope_group(yj):
        sw = jnp.where(first_half, pltpu.roll(yj, LANE - DH_QK // 2, 1), pltpu.roll(yj, DH_QK // 2, 1))
        return yj * cos + sw * sin

    yq = proj(0, Q_DIFF)
    yk = proj(Q_DIFF, Q_DIFF)
    for j in range(Q_DIFF // LANE):
        sl = slice(j * LANE, (j + 1) * LANE)
        q_ref[0, :, sl] = (rope_group(yq[:, sl]) * (DH_QK ** -0.5 * LOG2E)).astype(BF16)
        kj = rope_group(yk[:, sl])
        k_ref[0, :, sl] = kj
        kb_ref[0, :, sl] = kj.astype(BF16)
    yv = proj(2 * Q_DIFF, D_DIFF)
    for hh in range(H_DIFF):
        v_ref[0, pl.ds(hh, tm, stride=H_DIFF), :] = yv[:, hh * DV_DIFF:(hh + 1) * DV_DIFF]
    vb_ref[0] = yv.astype(BF16)

    base = 2 * Q_DIFF + D_DIFF
    gate_b = proj(base, D_CONV)
    cu = proj(base + D_CONV, D_CONV) * proj(base + 2 * D_CONV, D_CONV)

    @pl.when(t == 0)
    def _():
        ext_ref[SUBLANE - 2:SUBLANE, :] = st_ref[0]

    ext_ref[SUBLANE:SUBLANE + tm, :] = cu
    cw = cw_ref[...]
    conv = (ext_ref[SUBLANE - 2:SUBLANE - 2 + tm, :] * cw[0:1, :]
            + ext_ref[SUBLANE - 1:SUBLANE - 1 + tm, :] * cw[1:2, :]
            + cu * cw[2:3, :])
    gc_ref[0] = (gate_b * conv).astype(BF16)
    ext_ref[0:SUBLANE, :] = ext_ref[tm:tm + SUBLANE, :]

    @pl.when(t == nt - 1)
    def _():
        nc_ref[0] = ext_ref[tm + SUBLANE - 2:tm + SUBLANE, :]


def _even_in(x, g, wb, cos, sin, cw, state, tm):
    b, t, _ = x.shape
    nt = t // tm
    tok = lambda w: pl.BlockSpec((1, tm, w), lambda i, j: (i, j, 0))
    outs = [
        jax.ShapeDtypeStruct((b, t, Q_DIFF), BF16),
        jax.ShapeDtypeStruct((b, t, Q_DIFF), F32),
        jax.ShapeDtypeStruct((b, t, Q_DIFF), BF16),
        jax.ShapeDtypeStruct((b, t * H_DIFF, DV_DIFF), F32),
        jax.ShapeDtypeStruct((b, t, D_DIFF), BF16),
        jax.ShapeDtypeStruct((b, t, D_CONV), BF16),
        jax.ShapeDtypeStruct((b, CONV_W - 1, D_CONV), F32),
    ]
    return pl.pallas_call(
        functools.partial(_even_in_kernel, tm=tm, nt=nt),
        grid=(b, nt),
        in_specs=[
            tok(D_MODEL),
            _full((1, D_MODEL)),
            _full((D_MODEL, EVEN_IN)),
            pl.BlockSpec((tm, LANE), lambda i, j: (j, 0)),
            pl.BlockSpec((tm, LANE), lambda i, j: (j, 0)),
            _full((CONV_W, D_CONV)),
            pl.BlockSpec((1, CONV_W - 1, D_CONV), lambda i, j: (i, 0, 0)),
        ],
        out_specs=[tok(Q_DIFF), tok(Q_DIFF), tok(Q_DIFF),
                   pl.BlockSpec((1, tm * H_DIFF, DV_DIFF), lambda i, j: (i, j, 0)), tok(D_DIFF), tok(D_CONV),
                   pl.BlockSpec((1, CONV_W - 1, D_CONV), lambda i, j: (i, 0, 0))],
        out_shape=outs,
        scratch_shapes=[pltpu.VMEM((tm + 2 * SUBLANE, D_CONV), F32)],
        compiler_params=_params("parallel", "arbitrary"),
        name="even_in",
    )(x, g, wb, cos, sin, cw, state)


def _norm_proj_kernel(x_ref, g_ref, w_ref, *out_refs, segs):
    h = _rms(x_ref[0], g_ref[0], EPS).astype(BF16)
    i = 0
    for lo, width, want_f32, want_bf16, scale in segs:
        y = _dot(h, w_ref[0, :, lo:lo + width])
        if scale != 1.0:
            y = y * scale
        if want_f32:
            out_refs[i][0] = y
            i += 1
        if want_bf16:
            out_refs[i][0] = y.astype(BF16)
            i += 1


def _norm_proj(x, g, wb, segs, tm):
    b, t, d = x.shape
    n = wb.shape[-1]
    ng = wb.shape[0]
    nt = t // tm
    outs, specs = [], []
    for lo, width, want_f32, want_bf16, _ in segs:
        for want, dt in ((want_f32, F32), (want_bf16, BF16)):
            if want:
                outs.append(jax.ShapeDtypeStruct((ng * b, t, width), dt))
                specs.append(pl.BlockSpec((1, tm, width), lambda i, j: (i, j, 0)))
    return pl.pallas_call(
        functools.partial(_norm_proj_kernel, segs=segs),
        grid=(ng * b, nt),
        in_specs=[
            pl.BlockSpec((1, tm, d), lambda i, j: (i % b, j, 0)),
            pl.BlockSpec((1, 1, d), lambda i, j: (i // b, 0, 0)),
            pl.BlockSpec((1, d, n), lambda i, j: (i // b, 0, 0)),
        ],
        out_specs=specs,
        out_shape=outs,
        compiler_params=_params("parallel", "parallel"),
        name="norm_proj",
    )(x, g, wb)


def _lambda_full(lq1_ref, lk1_ref, lq2_ref, lk2_ref, lam_init):
    s1 = jnp.sum(lq1_ref[...] * lk1_ref[...], axis=1, keepdims=True)
    s2 = jnp.sum(lq2_ref[...] * lk2_ref[...], axis=1, keepdims=True)
    return jnp.exp(s1) - jnp.exp(s2) + lam_init


def _subln(o, gsub_ref, lam_init):
    return _rms(o, gsub_ref[...], SUBLN_EPS) * (1.0 - lam_init)


def _diff_prompt_kernel(q_ref, k_ref, v_ref, lq1_ref, lk1_ref, lq2_ref, lk2_ref, gsub_ref,
                        o_ref, *, tq, tk, lam_init):
    qi = pl.program_id(2)
    ratio = tk // tq
    q = q_ref[0]
    lane = lax.broadcasted_iota(jnp.int32, (tq, LANE), 1)
    zero = jnp.zeros_like(q)
    qm = (jnp.where(lane < DH_QK, q, zero), jnp.where(lane >= DH_QK, q, zero))
    ones = jnp.ones((tk, LANE), BF16)

    def step(j, carry, masked):
        start = pl.multiple_of(j * tk, tk)
        kb = k_ref[0, pl.ds(start, tk), :]
        vb = jnp.concatenate([v_ref[0, pl.ds(start, tk), :], ones], axis=1)
        if masked:
            r = _chunk_of(lax.broadcasted_iota(jnp.int32, (tq, tk), 0)) + lax.rem(qi, ratio) * (tq // CHUNK)
            c = _chunk_of(lax.broadcasted_iota(jnp.int32, (tq, tk), 1))
            vis = c <= r
        ss = [_dot_nt(qm[mi], kb) for mi in range(2)]
        if masked:
            ss = [jnp.where(vis, s, NEG_INF) for s in ss]
        ms = [jnp.maximum(carry[mi][0], jnp.max(ss[mi], axis=1, keepdims=True)) for mi in range(2)]
        ps = [jnp.exp2(ss[mi] - ms[mi]).astype(BF16) for mi in range(2)]
        pv = [_dot(ps[mi], vb) for mi in range(2)]
        return tuple((ms[mi], jnp.exp2(carry[mi][0] - ms[mi]) * carry[mi][1] + pv[mi]) for mi in range(2))

    init = tuple((jnp.full((tq, 1), NEG_INF, F32), jnp.zeros((tq, 2 * LANE), F32)) for _ in range(2))
    n_full = lax.div(qi, ratio)
    carry = lax.fori_loop(0, n_full, lambda j, c: step(j, c, False), init)
    carry = step(n_full, carry, True)
    lam = _lambda_full(lq1_ref, lk1_ref, lq2_ref, lk2_ref, lam_init)
    (_, a1), (_, a2) = carry
    o = a1[:, :LANE] / a1[:, LANE:] - lam * (a2[:, :LANE] / a2[:, LANE:])
    o_ref[0] = _subln(o, gsub_ref, lam_init).astype(BF16)


def _diff_prompt(q, kb, vb, lq1, lk1, lq2, lk2, gsub, lam_init, tq, tk):
    b, t, _ = q.shape
    lam_spec = _full((1, DH_QK))
    return pl.pallas_call(
        functools.partial(_diff_prompt_kernel, tq=tq, tk=tk, lam_init=lam_init),
        grid=(b, H_DIFF, t // tq),
        in_specs=[
            pl.BlockSpec((1, tq, LANE), lambda i, h, j: (i, j, h)),
            pl.BlockSpec((1, t, LANE), lambda i, h, j: (i, 0, h)),
            pl.BlockSpec((1, t, LANE), lambda i, h, j: (i, 0, h)),
            lam_spec, lam_spec, lam_spec, lam_spec,
            _full((1, DV_DIFF)),
        ],
        out_specs=pl.BlockSpec((1, tq, LANE), lambda i, h, j: (i, j, h)),
        out_shape=jax.ShapeDtypeStruct((b, t, D_DIFF), BF16),
        compiler_params=_params("parallel", "parallel", "arbitrary"),
        name="diff_prompt",
    )(q, kb, vb, lq1, lk1, lq2, lk2, gsub)


def _diff_decode_kernel(q_ref, kp_ref, vp_ref, kn_ref, vn_ref, lq1_ref, lk1_ref, lq2_ref, lk2_ref,
                        gsub_ref, o_ref, qs_ref, m_ref, l_ref, acc_ref, *, t, tk, nk, past_len,
                        lam_init):
    j = pl.program_id(1)
    nsub = 2 * H_DIFF
    rows = nsub * t

    @pl.when(j == 0)
    def _():
        q = q_ref[0]
        lane = lax.broadcasted_iota(jnp.int32, (t, Q_DIFF), 1)
        for s in range(nsub):
            qs_ref[s * t:(s + 1) * t, :] = jnp.where(_lane_band(lane, s, DH_QK), q, jnp.zeros_like(q))
        m_ref[...] = jnp.full(m_ref.shape, NEG_INF, F32)
        l_ref[...] = jnp.zeros(l_ref.shape, F32)
        acc_ref[...] = jnp.zeros(acc_ref.shape, F32)

    def update(kb, vh, k_start, width):
        s = _dot_nt(qs_ref[...], kb)
        q_pos = past_len + _row_query_index(nsub, t, width)
        k_pos = k_start + lax.broadcasted_iota(jnp.int32, (rows, width), 1)
        s = jnp.where(_chunk_of(k_pos) <= _chunk_of(q_pos), s, NEG_INF)
        m = m_ref[...]
        m_new = jnp.maximum(m, jnp.max(s, axis=1, keepdims=True))
        alpha = jnp.exp2(m - m_new)
        p = jnp.exp2(s - m_new)
        l_ref[...] = alpha * l_ref[...] + jnp.sum(p, axis=1, keepdims=True)
        m_ref[...] = m_new
        pb = p.astype(BF16)
        for h in range(H_DIFF):
            rs = slice(2 * h * t, (2 * h + 2) * t)
            acc_ref[rs, :] = alpha[rs] * acc_ref[rs, :] + _dot(pb[rs], vh[h])

    update(kp_ref[0].astype(BF16),
           [vp_ref[0, pl.ds(h, tk, stride=H_DIFF), :].astype(BF16) for h in range(H_DIFF)], j * tk, tk)

    @pl.when(j == nk - 1)
    def _():
        vn = vn_ref[0]
        update(kn_ref[0], [vn[:, h * DV_DIFF:(h + 1) * DV_DIFF] for h in range(H_DIFF)], past_len, t)
        lam = _lambda_full(lq1_ref, lk1_ref, lq2_ref, lk2_ref, lam_init)
        on = acc_ref[...] / l_ref[...]
        for h in range(H_DIFF):
            o = on[2 * h * t:(2 * h + 1) * t] - lam * on[(2 * h + 1) * t:(2 * h + 2) * t]
            o_ref[0, :, h * LANE:(h + 1) * LANE] = _subln(o, gsub_ref, lam_init).astype(BF16)


def _diff_decode(q, k_past, v_past, kn, vn, lq1, lk1, lq2, lk2, gsub, lam_init, tk):
    b, t, _ = q.shape
    past_len = k_past.shape[1]
    nk = past_len // tk
    rows = 2 * H_DIFF * t
    lam_spec = _full((1, DH_QK))
    tokb = lambda w: pl.BlockSpec((1, t, w), lambda i, j: (i, 0, 0))
    return pl.pallas_call(
        functools.partial(_diff_decode_kernel, t=t, tk=tk, nk=nk, past_len=past_len,
                          lam_init=lam_init),
        grid=(b, nk),
        in_specs=[
            tokb(Q_DIFF),
            pl.BlockSpec((1, tk, Q_DIFF), lambda i, j: (i, j, 0)),
            pl.BlockSpec((1, tk * H_DIFF, DV_DIFF), lambda i, j: (i, j, 0)),
            tokb(Q_DIFF), tokb(D_DIFF),
            lam_spec, lam_spec, lam_spec, lam_spec,
            _full((1, DV_DIFF)),
        ],
        out_specs=tokb(D_DIFF),
        out_shape=jax.ShapeDtypeStruct((b, t, D_DIFF), BF16),
        scratch_shapes=[pltpu.VMEM((rows, Q_DIFF), BF16), pltpu.VMEM((rows, 1), F32),
                        pltpu.VMEM((rows, 1), F32), pltpu.VMEM((rows, DV_DIFF), F32)],
        compiler_params=_params("parallel", "arbitrary"),
        name="diff_decode",
    )(q, k_past, v_past, kn, vn, lq1, lk1, lq2, lk2, gsub)


def _softplus(z):
    neg_abs = lax.bitcast_convert_type(lax.bitcast_convert_type(z, jnp.uint32) | jnp.uint32(0x80000000), F32)
    return jnp.maximum(z, 0.0) + jnp.log(1.0 + jnp.exp2(neg_abs)) * LOG2E


def _rev_cumsum(sp, tri):
    hi = sp.astype(BF16)
    lo = (sp - hi.astype(F32)).astype(BF16)
    return _dot(hi, tri) + _dot(lo, tri)


def _tri(n):
    r = lax.broadcasted_iota(jnp.int32, (n, n), 0)
    c = lax.broadcasted_iota(jnp.int32, (n, n), 1)
    return jnp.where(r >= c, 1.0, 0.0).astype(BF16)


def _head_ones(width):
    r = lax.broadcasted_iota(jnp.int32, (LANE, LANE), 0)
    c = lax.broadcasted_iota(jnp.int32, (LANE, LANE), 1)
    return jnp.where((r < width) == (c < width), 1.0, 0.0).astype(BF16)


def _head_sq_norms(x_bf16, ones_blk):
    xf = x_bf16.astype(F32)
    return _dot((xf * xf).astype(BF16), ones_blk)


def _sb_prompt_kernel(q_ref, k_ref, v_ref, o_ref, kmax_ref, *, tq, t, ng):
    qi = pl.program_id(2)
    lane = lax.broadcasted_iota(jnp.int32, (tq, LANE), 1)
    tri = _tri(tq)
    ones_blk = _head_ones(DH_SB)
    gs = [slice(g * LANE, (g + 1) * LANE) for g in range(ng)]

    @pl.when(qi == 0)
    def _():
        rows = math.gcd(t, 4 * tq)
        for g in range(ng):
            def body(i, mx, g=g):
                kc = k_ref[0, pl.ds(pl.multiple_of(i * rows, rows), rows), gs[g]]
                return jnp.maximum(mx, _head_sq_norms(kc, ones_blk))
            mx = lax.fori_loop(0, t // rows, body, jnp.zeros((rows, LANE), F32))
            kmax_ref[:, gs[g]] = jnp.max(mx, axis=0, keepdims=True)

    qs, bzs = [], []
    for g in range(ng):
        q = q_ref[0, :, gs[g]]
        zero = jnp.zeros_like(q)
        qs.append(jnp.concatenate([jnp.where(lane < DH_SB, q, zero), jnp.where(lane >= DH_SB, q, zero)], axis=0))
        bound = jnp.sqrt(_head_sq_norms(q, ones_blk) * kmax_ref[:, gs[g]]) * SB_BOUND_SLACK
        bzs += [bound[:, 0:1], bound[:, DH_SB:DH_SB + 1]]
    bz = jnp.concatenate(bzs, axis=0)

    def alive(ccar):
        return (jnp.max(bz - ccar) > -SB_DEAD).astype(jnp.int32)

    def blk(ref, j, g):
        return ref[0, pl.ds(pl.multiple_of(j * tq, tq), tq), gs[g]]

    def scores(j):
        return jnp.concatenate([_dot_nt(qs[g], blk(k_ref, j, g)) for g in range(ng)], axis=0)

    def weighted(w, vals):
        return jnp.concatenate([_dot(w[2 * g * tq:2 * (g + 1) * tq], vals(g)) for g in range(ng)], axis=0)

    has_prev = qi > 0
    jp = jnp.maximum(qi - 1, 0)
    z_r = scores(qi)
    z_l = scores(jp)
    vis = lax.broadcasted_iota(jnp.int32, (2 * ng * tq, tq), 1) < _row_query_index(2 * ng, tq, tq)
    sp_r = jnp.where(vis, _softplus(z_r), 0.0)
    cl_r = _rev_cumsum(sp_r, tri)
    total_r = cl_r[:, 0:1]
    cl_l = _rev_cumsum(_softplus(z_l), tri) + jnp.where(has_prev, total_r, -NEG_INF)
    w_r = jnp.where(vis, jnp.exp2(z_r - cl_r), 0.0).astype(BF16)
    w_l = jnp.exp2(z_l - cl_l).astype(BF16)
    acc = weighted(jnp.concatenate([w_l, w_r], axis=1),
                   lambda g: jnp.concatenate([blk(v_ref, jp, g), blk(v_ref, qi, g)], axis=0))
    ccar = jnp.where(has_prev, cl_l[:, 0:1], total_r)

    def body(c):
        j, _, ccar, acc = c
        z = scores(j)
        cl = _rev_cumsum(_softplus(z), tri)
        w = jnp.exp2(z - ccar - cl).astype(BF16)
        ccar = ccar + cl[:, 0:1]
        return j - 1, alive(ccar), ccar, acc + weighted(w, lambda g: blk(v_ref, j, g))

    out = lax.while_loop(lambda c: jnp.logical_and(c[0] >= 0, c[1] > 0), body,
                         (qi - 2, alive(ccar), ccar, acc))
    acc = out[3]
    for g in range(ng):
        a = acc[2 * g * tq:2 * (g + 1) * tq]
        o_ref[0, :, gs[g]] = jnp.where(lane < DH_SB, a[:tq], a[tq:]).astype(BF16)


def _sb_prompt(q, kb, vb, tq, ng):
    b, t, _ = q.shape
    wg = ng * LANE
    return pl.pallas_call(
        functools.partial(_sb_prompt_kernel, tq=tq, t=t, ng=ng),
        scratch_shapes=[pltpu.VMEM((1, wg), F32)],
        grid=(b, D_SB // wg, t // tq),
        in_specs=[
            pl.BlockSpec((1, tq, wg), lambda i, h, j: (i, j, h)),
            pl.BlockSpec((1, t, wg), lambda i, h, j: (i, 0, h)),
            pl.BlockSpec((1, t, wg), lambda i, h, j: (i, 0, h)),
        ],
        out_specs=pl.BlockSpec((1, tq, wg), lambda i, h, j: (i, j, h)),
        out_shape=jax.ShapeDtypeStruct((b, t, D_SB), BF16),
        compiler_params=_params("parallel", "parallel", "arbitrary"),
        name="sb_prompt",
    )(q, kb, vb)


def _sb_kmax_kernel(k_ref, o_ref):
    j = pl.program_id(1)
    x = k_ref[0]
    ones_blk = _head_ones(DH_SB)
    blk = jnp.concatenate(
        [jnp.max(_head_sq_norms(x[:, g * LANE:(g + 1) * LANE], ones_blk), axis=0, keepdims=True)
         for g in range(D_SB // LANE)], axis=1)

    @pl.when(j == 0)
    def _():
        o_ref[0] = blk

    @pl.when(j > 0)
    def _():
        o_ref[0] = jnp.maximum(o_ref[0], blk)


def _sb_kmax(k_past, tk):
    b, past_len, _ = k_past.shape
    return pl.pallas_call(
        _sb_kmax_kernel,
        grid=(b, past_len // tk),
        in_specs=[pl.BlockSpec((1, tk, D_SB), lambda i, j: (i, j, 0))],
        out_specs=pl.BlockSpec((1, 1, D_SB), lambda i, j: (i, 0, 0)),
        out_shape=jax.ShapeDtypeStruct((b, 1, D_SB), F32),
        compiler_params=_params("parallel", "arbitrary"),
        name="sb_kmax",
    )(k_past)


def _heads(x):
    return [x[:, h * DH_SB:(h + 1) * DH_SB] for h in range(H_SB)]


def _sb_rows_attend(qh, kh, vh, t, width, ccar, acc, vis):
    z = jnp.concatenate([_dot_nt(qh[h], kh[h]) for h in range(H_SB)], axis=0)
    sp = _softplus(z)
    if vis is not None:
        sp = jnp.where(vis, sp, 0.0)
    cl = _rev_cumsum(sp, _tri(width))
    w = jnp.exp2(z - ccar - cl)
    if vis is not None:
        w = jnp.where(vis, w, 0.0)
    w = w.astype(BF16)
    pv = jnp.concatenate([_dot(w[h * t:(h + 1) * t], vh[h]) for h in range(H_SB)], axis=0)
    return ccar + cl[:, 0:1], acc + pv


def _sb_rows_bound(qh, kmax2):
    return jnp.concatenate(
        [jnp.sqrt(jnp.sum(jnp.square(qh[h].astype(F32)), axis=1, keepdims=True)
                  * kmax2[:, h * DH_SB:h * DH_SB + 1]) for h in range(H_SB)], axis=0) * SB_BOUND_SLACK


def _sb_rows_alive(bz, ccar):
    return (jnp.max(bz - ccar) > -SB_DEAD).astype(jnp.int32)


def _sb_rows_store(o_ref, acc, t):
    for h in range(H_SB):
        o_ref[0, :, h * DH_SB:(h + 1) * DH_SB] = acc[h * t:(h + 1) * t, :].astype(BF16)


def _sb_decode_head_kernel(q_ref, kn_ref, vn_ref, kt_ref, vt_ref, kmax_ref,
                           o_ref, c_ref, acc_ref, alive_ref, *, t, tk):
    rows = H_SB * t
    qh = _heads(q_ref[0])
    vis = lax.broadcasted_iota(jnp.int32, (rows, t), 1) < _row_query_index(H_SB, t, t)
    ccar, acc = _sb_rows_attend(qh, _heads(kn_ref[0]), _heads(vn_ref[0]), t, t,
                                jnp.zeros((rows, 1), F32), jnp.zeros((rows, DH_SB), F32), vis)
    ccar, acc = _sb_rows_attend(qh, _heads(kt_ref[0].astype(BF16)), _heads(vt_ref[0].astype(BF16)),
                                t, tk, ccar, acc, None)
    _sb_rows_store(o_ref, acc, t)
    c_ref[0] = ccar
    acc_ref[0] = acc
    alive_ref[0] = jnp.zeros((1, LANE), jnp.int32) + _sb_rows_alive(_sb_rows_bound(qh, kmax_ref[0]), ccar)


def _sb_decode_rest_kernel(q_ref, kmax_ref, c_ref, acc_ref, kp_hbm, vp_hbm, o_ref, kbuf, vbuf, sem,
                           *, t, tk, nk):
    b = pl.program_id(0)
    qh = _heads(q_ref[0])
    bz = _sb_rows_bound(qh, kmax_ref[0])

    def body(c):
        j, _, ccar, acc = c
        start = pl.multiple_of(j * tk, tk)
        ck = pltpu.make_async_copy(kp_hbm.at[b, pl.ds(start, tk), :], kbuf, sem.at[0])
        cv = pltpu.make_async_copy(vp_hbm.at[b, pl.ds(start, tk), :], vbuf, sem.at[1])
        ck.start()
        cv.start()
        ck.wait()
        cv.wait()
        ccar, acc = _sb_rows_attend(qh, _heads(kbuf[...].astype(BF16)), _heads(vbuf[...].astype(BF16)),
                                    t, tk, ccar, acc, None)
        return j - 1, _sb_rows_alive(bz, ccar), ccar, acc

    ccar = c_ref[0]
    out = lax.while_loop(lambda c: jnp.logical_and(c[0] >= 0, c[1] > 0), body,
                         (jnp.int32(nk - 2), _sb_rows_alive(bz, ccar), ccar, acc_ref[0]))
    _sb_rows_store(o_ref, out[3], t)


def _sb_decode(q, k_past, v_past, kn, vn, tk):
    b, t, _ = q.shape
    past_len = k_past.shape[1]
    nk = past_len // tk
    rows = H_SB * t
    k_flat = k_past.reshape(b, past_len, D_SB)
    kmax2 = _sb_kmax(k_flat, min(2048, past_len))
    k_tail = k_past[:, past_len - tk:].reshape(b, tk, D_SB)
    v_tail = v_past[:, past_len - tk:].reshape(b, tk, D_SB)
    tokb = lambda: pl.BlockSpec((1, t, D_SB), lambda i: (i, 0, 0))
    tail = lambda: pl.BlockSpec((1, tk, D_SB), lambda i: (i, 0, 0))
    kmx = lambda: pl.BlockSpec((1, 1, D_SB), lambda i: (i, 0, 0))
    car = lambda: pl.BlockSpec((1, rows, 1), lambda i: (i, 0, 0))
    accs = lambda: pl.BlockSpec((1, rows, DH_SB), lambda i: (i, 0, 0))
    o, ccar, acc, alive = pl.pallas_call(
        functools.partial(_sb_decode_head_kernel, t=t, tk=tk),
        grid=(b,),
        in_specs=[tokb(), tokb(), tokb(), tail(), tail(), kmx()],
        out_specs=[tokb(), car(), accs(), pl.BlockSpec((1, 1, LANE), lambda i: (i, 0, 0))],
        out_shape=[jax.ShapeDtypeStruct((b, t, D_SB), BF16), jax.ShapeDtypeStruct((b, rows, 1), F32),
                   jax.ShapeDtypeStruct((b, rows, DH_SB), F32),
                   jax.ShapeDtypeStruct((b, 1, LANE), jnp.int32)],
        compiler_params=_params("parallel"),
        name="sb_decode_head",
    )(q, kn, vn, k_tail, v_tail, kmax2)
    if nk < 2:
        return o

    def rest(args):
        q_, kmax2_, ccar_, acc_, k_flat_, v_past_, _ = args
        return pl.pallas_call(
            functools.partial(_sb_decode_rest_kernel, t=t, tk=tk, nk=nk),
            grid=(b,),
            in_specs=[tokb(), kmx(), car(), accs(), pl.BlockSpec(memory_space=pl.ANY),
                      pl.BlockSpec(memory_space=pl.ANY)],
            out_specs=tokb(),
            out_shape=jax.ShapeDtypeStruct((b, t, D_SB), BF16),
            scratch_shapes=[pltpu.VMEM((tk, D_SB), F32), pltpu.VMEM((tk, D_SB), F32),
                            pltpu.SemaphoreType.DMA((2,))],
            compiler_params=_params("arbitrary"),
            name="sb_decode_rest",
        )(q_, kmax2_, ccar_, acc_, k_flat_, v_past_.reshape(b, past_len, D_SB))

    return lax.cond(jnp.max(alive) > 0, rest, lambda args: args[-1],
                    (q, kmax2, ccar, acc, k_flat, v_past, o))


def _post_mix_kernel(*refs, n_in):
    x_ref = refs[0]
    a_refs = refs[1:1 + n_in]
    w_ref, g_ref, wq_ref, mk_ref, mv_ref, wo_ref, o_ref = refs[1 + n_in:]
    a = a_refs[0][0] if n_in == 1 else jnp.concatenate([r[0] for r in a_refs], axis=1)
    x = x_ref[0] + _dot(a, w_ref[...])
    hq = _rms(x, g_ref[...], EPS).astype(BF16)
    q = (_dot(hq, wq_ref[...]) * (DH_MEM ** -0.5 * LOG2E)).astype(BF16)
    hs = [slice(h * DH_MEM, (h + 1) * DH_MEM) for h in range(H_MEM)]
    ss = [_dot_nt(q[:, hs[h]], mk_ref[0, :, hs[h]]) for h in range(H_MEM)]
    ps = [jnp.exp2(s - jnp.max(s, axis=1, keepdims=True)) for s in ss]
    ls = [jnp.sum(p, axis=1, keepdims=True) for p in ps]
    oh = [(_dot(ps[h].astype(BF16), mv_ref[0, :, hs[h]]) / ls[h]).astype(BF16) for h in range(H_MEM)]
    o_ref[0] = x + _dot(jnp.concatenate(oh, axis=1), wo_ref[...])


def _post_mix(x, acts, w, g, wq, mk, mv, wo, tm):
    b, t, d = x.shape
    n_in = len(acts)
    tok = lambda w: pl.BlockSpec((1, tm, w), lambda i, j: (i, j, 0))
    mem = pl.BlockSpec((1, N_MEM, D_MODEL), lambda i, j: (i, 0, 0))
    return pl.pallas_call(
        functools.partial(_post_mix_kernel, n_in=n_in),
        grid=(b, t // tm),
        in_specs=([tok(d)] + [tok(a.shape[-1]) for a in acts]
                  + [_full(w.shape), _full((1, d)), _full((d, d)), mem, mem, _full((d, d))]),
        out_specs=tok(d),
        out_shape=jax.ShapeDtypeStruct((b, t, d), F32),
        compiler_params=_params("parallel", "parallel"),
        name="post_mix",
    )(x, *acts, w, g, wq, mk, mv, wo)


def _ffn_kernel(x_ref, g_ref, wu_ref, wd_ref, gf_ref, o_ref, *, fc, final):
    x = x_ref[...]
    h = _rms(x, g_ref[...], EPS).astype(BF16)
    y = x
    for c in range(D_FF // fc):
        u = jnp.maximum(_dot(h, wu_ref[:, c * fc:(c + 1) * fc]), 0.0)
        y = y + _dot((u * u).astype(BF16), wd_ref[c * fc:(c + 1) * fc, :])
    if final:
        y = _rms(y, gf_ref[...], EPS)
    o_ref[...] = y


def _ffn(x2d, g, wu, wd, gf, final, tm, fc):
    m, d = x2d.shape
    return pl.pallas_call(
        functools.partial(_ffn_kernel, fc=fc, final=final),
        grid=(m // tm,),
        in_specs=[pl.BlockSpec((tm, d), lambda i: (i, 0)), _full((1, d)), _full((d, D_FF)),
                  _full((D_FF, d)), _full((1, d))],
        out_specs=pl.BlockSpec((tm, d), lambda i: (i, 0)),
        out_shape=jax.ShapeDtypeStruct((m, d), F32),
        compiler_params=_params("parallel"),
        name="ffn",
    )(x2d, g, wu, wd, gf)


def _rope_tables(pos):
    half = DH_QK // 2
    inv = jnp.power(ROPE_THETA, -jnp.arange(half, dtype=F32) * (2.0 / DH_QK))
    ang = pos.astype(F32)[:, None] * inv[None, :]
    cos, sin = jnp.cos(ang), jnp.sin(ang)
    reps = LANE // DH_QK
    return (jnp.tile(jnp.concatenate([cos, cos], axis=1), (1, reps)),
            jnp.tile(jnp.concatenate([-sin, sin], axis=1), (1, reps)))


def _row(v):
    return v.reshape(1, -1)


def _run(x, mem_kb, mem_vb, past, p, tm, tq):
    b, t, d = x.shape
    past_len = 0 if past is None else past[0].shape[2]
    pos = jnp.arange(past_len, past_len + t)
    cos, sin = _rope_tables(pos)
    outs = {}
    for i in range(DEPTH):
        if i % 2 == 0:
            e = i // 2
            lam_init = 0.8 - 0.6 * math.exp(-0.3 * i)
            state = jnp.zeros((b, CONV_W - 1, D_CONV), F32) if past is None else past[2][e]
            q, k, kb, v, vb, gc, nc = _even_in(x, _row(p['norm_mix'][i]), p['w_in_even_b'][e], cos, sin,
                                               p['conv_w'][e], state, tm)
            lam_args = (_row(p['lambda_q1'][e]), _row(p['lambda_k1'][e]), _row(p['lambda_q2'][e]),
                        _row(p['lambda_k2'][e]), _row(p['subln_gain'][e]))
            if past is None:
                o = _diff_prompt(q, kb, vb, *lam_args, lam_init, min(4 * tq, t), min(1024, t))
            else:
                o = _diff_decode(q, past[0][e].reshape(b, past_len, Q_DIFF),
                                 past[1][e].reshape(b, past_len * H_DIFF, DV_DIFF), kb, vb, *lam_args,
                                 lam_init, min(2048, past_len))
            outs['dk'] = k.reshape(1, b, t, 2 * H_DIFF, DH_QK)
            outs['dv'] = v.reshape(1, b, t, H_DIFF, DV_DIFF)
            outs['conv'] = nc[None]
            acts = [o, gc]
            w_out = p['w_out_even_b'][e]
        else:
            o_idx = i // 2
            segs = ((0, D_SB, False, True, DH_SB ** -0.5 * LOG2E), (D_SB, D_SB, True, True, 1.0),
                    (2 * D_SB, D_SB, True, True, 1.0))
            q, k, kb, v, vb = _norm_proj(x, p['norm_mix'][i].reshape(1, 1, d),
                                         p['w_in_odd_b'][o_idx][None], segs, tm)
            if past is None:
                o = _sb_prompt(q, kb, vb, tq, 2)
            else:
                o = _sb_decode(q, past[3][o_idx], past[4][o_idx], kb, vb, min(256, past_len))
            outs['sk'] = k.reshape(1, b, t, H_SB, DH_SB)
            outs['sv'] = v.reshape(1, b, t, H_SB, DH_SB)
            acts = [o]
            w_out = p['w_out_odd_b'][o_idx]
        x = _post_mix(x, acts, w_out, _row(p['norm_cross'][i]), p['w_q_mem_b'][i], mem_kb[i], mem_vb[i],
                      p['w_o_mem_b'][i], tm)
        m = b * t
        x = _ffn(x.reshape(m, d), _row(p['norm_ffn'][i]), p['w_ffn_up_b'][i], p['w_ffn_down_b'][i],
                 _row(p['norm_final']), i == DEPTH - 1, min(512, m), 1024).reshape(b, t, d)
    return x, outs


def kernel(x_prompt, x_sample, cache_diff_k, cache_diff_v, state_conv, cache_sb_k, cache_sb_v, cache_mem_k, cache_mem_v, mem_prompt, w_in_even, w_out_even, lambda_q1, lambda_k1, lambda_q2, lambda_k2, subln_gain, conv_w, w_in_odd, w_out_odd, norm_mix, norm_mem, norm_cross, w_q_mem, w_k_mem, w_v_mem, w_o_mem, norm_ffn, w_ffn_up, w_ffn_down, norm_final):
    p = dict(norm_mix=norm_mix, norm_cross=norm_cross, norm_ffn=norm_ffn, norm_final=norm_final,
             lambda_q1=lambda_q1, lambda_k1=lambda_k1, lambda_q2=lambda_q2, lambda_k2=lambda_k2,
             subln_gain=subln_gain, conv_w=conv_w)
    for name, w in (('w_in_even', w_in_even), ('w_out_even', w_out_even), ('w_in_odd', w_in_odd),
                    ('w_out_odd', w_out_odd), ('w_q_mem', w_q_mem), ('w_o_mem', w_o_mem),
                    ('w_ffn_up', w_ffn_up), ('w_ffn_down', w_ffn_down)):
        p[name + '_b'] = w.astype(BF16)

    bp = mem_prompt.shape[0]
    w_kv = jnp.concatenate([w_k_mem, w_v_mem], axis=-1).astype(BF16)
    segs = ((0, D_MODEL, True, True, 1.0), (D_MODEL, D_MODEL, True, True, 1.0))
    mk, mkb, mv, mvb = _norm_proj(mem_prompt, norm_mem.reshape(DEPTH, 1, D_MODEL), w_kv, segs, N_MEM)
    p_mem_k = mk.reshape(DEPTH, bp, N_MEM, H_MEM, DH_MEM)
    p_mem_v = mv.reshape(DEPTH, bp, N_MEM, H_MEM, DH_MEM)
    mkb = mkb.reshape(DEPTH, bp, N_MEM, D_MODEL)
    mvb = mvb.reshape(DEPTH, bp, N_MEM, D_MODEL)

    t_p = x_prompt.shape[1]
    y_prompt, po = _run(x_prompt, mkb, mvb, None, p, min(512, t_p), min(256, t_p))

    bs = x_sample.shape[0]
    t_s = x_sample.shape[1]
    cmk = cache_mem_k.reshape(DEPTH, bs, N_MEM, D_MODEL).astype(BF16)
    cmv = cache_mem_v.reshape(DEPTH, bs, N_MEM, D_MODEL).astype(BF16)
    y_sample, so = _run(x_sample, cmk, cmv,
                        (cache_diff_k, cache_diff_v, state_conv, cache_sb_k, cache_sb_v), p, t_s, t_s)

    return (y_prompt, y_sample, po['dk'], po['dv'], po['conv'], po['sk'], po['sv'], p_mem_k, p_mem_v,
            so['dk'], so['dv'], so['conv'], so['sk'], so['sv'])
```
